```python
import math
import jax
import jax.numpy as jnp
from jax import lax
import numpy as np

D_MODEL = 1024
BATCH = 2
SEQ = 16384
DEPTH = 1
DEC_BATCH = 128
DEC_SEQ = 4
PAST_LEN = 8192
PAGE_SIZE = 128

HEAD_DIM = 64
ATT_WIDTH = D_MODEL // 2
ATT_HEADS = ATT_WIDTH // HEAD_DIM
SSD_WIDTH = D_MODEL - ATT_WIDTH
SSD_HEAD_DIM = 64
SSD_HEADS = SSD_WIDTH // SSD_HEAD_DIM
SSD_GROUPS = 2
SSD_HEADS_PER_GROUP = SSD_HEADS // SSD_GROUPS
SSD_STATE = 128
SSD_CHUNK = 256
CONV_WIDTH = 4
CONV_DIM = SSD_WIDTH + 2 * SSD_GROUPS * SSD_STATE
MIX_WIDTH = ATT_WIDTH + SSD_WIDTH
IN_WIDTH = 3 * ATT_WIDTH + SSD_WIDTH + CONV_DIM + SSD_HEADS
MOBA_BLOCK = 256
MOBA_TOPK = 3
ATT_QBLOCK = 128
N_EXPERTS = 256
TOP_K = 8
EXPERT_DIM = D_MODEL // 4
SHARED_DIM = D_MODEL // 4
N_EXPERT_GROUPS = 8
TOPK_GROUPS = 4
ROUTED_SCALE = 2.5
NORM_EPS = 1e-6

kernel_name = 'hymba_moba_ssd_moe_step'


def rms_norm(x, g):
    xf = x.astype(jnp.float32)
    y = xf * lax.rsqrt(jnp.mean(xf * xf, axis=-1, keepdims=True) + NORM_EPS)
    return y * g.astype(jnp.float32)


def adaln(c, w_ada, b_ada):
    m = (jax.nn.silu(c) @ w_ada + b_ada).astype(jnp.float32)
    return [u[:, None, :] for u in jnp.split(m, 6, axis=-1)]


def modulate(h, shift, scale, dtype):
    return (h * (1.0 + scale) + shift).astype(dtype)


def split_proj(p):
    a = ATT_WIDTH
    cuts = [a, 2 * a, 3 * a, 3 * a + SSD_WIDTH, 3 * a + SSD_WIDTH + CONV_DIM]
    return jnp.split(p, cuts, axis=-1)


def split_heads(u):
    bn, length, _ = u.shape
    return u.reshape(bn, length, ATT_HEADS, HEAD_DIM).transpose(0, 2, 1, 3).astype(jnp.float32)


def to_pages(kh):
    bn, h, s, dh = kh.shape
    return kh.reshape(bn, h, s // PAGE_SIZE, PAGE_SIZE, dh).transpose(0, 2, 1, 3, 4)


def alibi_slopes():
    return jnp.exp2(-8.0 * (jnp.arange(ATT_HEADS, dtype=jnp.float32) + 1.0) / ATT_HEADS)


def moba_prompt(q, k, v):
    bn, h, s, dh = q.shape
    nb = -(-s // MOBA_BLOCK)
    pad = nb * MOBA_BLOCK - s
    padw = ((0, 0), (0, 0), (0, pad), (0, 0))
    kb = jnp.pad(k, padw).reshape(bn, h, nb, MOBA_BLOCK, dh)
    vb = jnp.pad(v, padw).reshape(bn, h, nb, MOBA_BLOCK, dh)
    k_mean = jnp.mean(kb, axis=3)
    n_sel = min(MOBA_TOPK, nb)
    nq = s // ATT_QBLOCK
    slopes = alibi_slopes()[None, :, None, None]
    scale = HEAD_DIM ** -0.5
    take = jax.vmap(jax.vmap(lambda blocks, idx: blocks[idx]))
    q_blocks = jnp.moveaxis(q.reshape(bn, h, nq, ATT_QBLOCK, dh), 2, 0)

    def query_block(args):
        qi, qb = args
        t = qi * ATT_QBLOCK + jnp.arange(ATT_QBLOCK)
        cur = (qi * ATT_QBLOCK) // MOBA_BLOCK
        gate = jnp.einsum('bhqd,bhnd->bhqn', qb, k_mean)
        gate = jnp.where(jnp.arange(nb) < cur, gate, -jnp.inf)
        _, sel = lax.top_k(gate, n_sel)
        k_sel = take(kb, sel)
        v_sel = take(vb, sel)
        s_pos = sel[..., None] * MOBA_BLOCK + jnp.arange(MOBA_BLOCK)
        dist = (t[None, None, :, None, None] - s_pos).astype(jnp.float32)
        s_sel = jnp.einsum('bhqd,bhqnkd->bhqnk', qb, k_sel) * scale - slopes[..., None] * dist
        s_sel = jnp.where((sel < cur)[..., None], s_sel, -jnp.inf)
        k_own = lax.dynamic_index_in_dim(kb, cur, axis=2, keepdims=False)
        v_own = lax.dynamic_index_in_dim(vb, cur, axis=2, keepdims=False)
        o_pos = cur * MOBA_BLOCK + jnp.arange(MOBA_BLOCK)
        d_own = (t[:, None] - o_pos[None, :]).astype(jnp.float32)
        s_own = jnp.einsum('bhqd,bhkd->bhqk', qb, k_own) * scale - slopes * d_own
        s_own = jnp.where(d_own >= 0, s_own, -jnp.inf)
        logits = jnp.concatenate([s_sel.reshape(bn, h, ATT_QBLOCK, n_sel * MOBA_BLOCK), s_own], axis=-1)
        p = jax.nn.softmax(logits, axis=-1)
        p_sel = p[..., :n_sel * MOBA_BLOCK].reshape(bn, h, ATT_QBLOCK, n_sel, MOBA_BLOCK)
        p_own = p[..., n_sel * MOBA_BLOCK:]
        return (jnp.einsum('bhqnk,bhqnkd->bhqd', p_sel, v_sel)
                + jnp.einsum('bhqk,bhkd->bhqd', p_own, v_own))

    out = lax.map(query_block, (jnp.arange(nq), q_blocks))
    return jnp.moveaxis(out, 0, 2).reshape(bn, h, s, dh)


def moba_sample(q, k_new, v_new, cache_k, cache_v, page_table, layer):
    bn, h, t_len, dh = q.shape
    f32 = jnp.float32
    ppb = MOBA_BLOCK // PAGE_SIZE
    n_pages = page_table.shape[1]
    past = n_pages * PAGE_SIZE
    cur = past // MOBA_BLOCK
    first_own = cur * ppb
    slopes = alibi_slopes()[None, :, None]
    scale = HEAD_DIM ** -0.5
    t_pos = past + jnp.arange(t_len)

    def rows(pages):
        return jnp.moveaxis(pages, 2, 1).reshape(bn, h, pages.shape[1] * PAGE_SIZE, dh).astype(f32)

    own_pages = page_table[:, first_own:]
    k_own = jnp.concatenate([rows(cache_k[layer, own_pages]), k_new], axis=2)
    v_own = jnp.concatenate([rows(cache_v[layer, own_pages]), v_new], axis=2)
    o_pos = cur * MOBA_BLOCK + jnp.arange(k_own.shape[2])
    d_own = (t_pos[:, None] - o_pos[None, :]).astype(f32)
    s_own = jnp.einsum('bhtd,bhkd->bhtk', q, k_own) * scale - slopes[..., None] * d_own
    s_own = jnp.where(d_own >= 0, s_own, -jnp.inf)
    n_sel = min(MOBA_TOPK, cur)
    if n_sel == 0:
        p = jax.nn.softmax(s_own, axis=-1)
        return jnp.einsum('bhtk,bhkd->bhtd', p, v_own)
    k_mean = cache_k[layer, page_table[:, :first_own]].astype(f32).reshape(
        bn, cur, ppb, h, PAGE_SIZE, dh).mean(axis=(2, 4))
    gate = jnp.einsum('bhtd,bnhd->bhtn', q, k_mean)
    _, sel = lax.top_k(gate, n_sel)
    page_idx = sel[..., None] * ppb + jnp.arange(ppb)
    phys = jax.vmap(lambda pt, ix: pt[ix])(page_table, page_idx)
    head_idx = jnp.arange(h)[None, :, None, None]

    def token(args):
        q_t, phys_t, sel_t, s_own_t, t = args
        k_sel = cache_k[layer, phys_t, head_idx].astype(f32).reshape(bn, h, n_sel, MOBA_BLOCK, dh)
        v_sel = cache_v[layer, phys_t, head_idx].astype(f32).reshape(bn, h, n_sel, MOBA_BLOCK, dh)
        s_pos = sel_t[..., None] * MOBA_BLOCK + jnp.arange(MOBA_BLOCK)
        dist = (t - s_pos).astype(f32)
        s_sel = jnp.einsum('bhd,bhnkd->bhnk', q_t, k_sel) * scale - slopes[..., None] * dist
        logits = jnp.concatenate([s_sel.reshape(bn, h, n_sel * MOBA_BLOCK), s_own_t], axis=-1)
        p = jax.nn.softmax(logits, axis=-1)
        p_sel = p[..., :n_sel * MOBA_BLOCK].reshape(bn, h, n_sel, MOBA_BLOCK)
        p_own = p[..., n_sel * MOBA_BLOCK:]
        return (jnp.einsum('bhnk,bhnkd->bhd', p_sel, v_sel)
                + jnp.einsum('bhk,bhkd->bhd', p_own, v_own))

    out = lax.map(token, (jnp.moveaxis(q, 2, 0), jnp.moveaxis(phys, 2, 0), jnp.moveaxis(sel, 2, 0),
                          jnp.moveaxis(s_own, 2, 0), t_pos))
    return jnp.moveaxis(out, 0, 2)


def ssd_chunked(xdt, a, bm, cm, h0):
    bn, length = xdt.shape[0], xdt.shape[1]
    cs = math.gcd(length, SSD_CHUNK)
    nc = length // cs
    tril = jnp.tril(jnp.ones((cs, cs), dtype=bool))[None, :, :, None, None]

    def to_chunks(u):
        return jnp.moveaxis(u.reshape(bn, nc, cs, *u.shape[2:]), 1, 0)

    def step(h, inp):
        xc, ac, bc, cc = inp
        acs = jnp.cumsum(ac, axis=1)
        seg = acs[:, :, None] - acs[:, None, :]
        decay = jnp.exp(jnp.where(tril, seg, -jnp.inf))
        cb = jnp.einsum('btgn,bsgn->btsg', cc, bc)
        y = jnp.einsum('btsg,btsgr,bsgrp->btgrp', cb, decay, xc)
        y = y + jnp.einsum('btgn,bgrpn->btgrp', cc, h) * jnp.exp(acs)[..., None]
        h = (h * jnp.exp(acs[:, -1])[..., None, None]
             + jnp.einsum('bsgr,bsgn,bsgrp->bgrpn', jnp.exp(acs[:, -1:] - acs), bc, xc))
        return h, y

    h, ys = lax.scan(step, h0, (to_chunks(xdt), to_chunks(a), to_chunks(bm), to_chunks(cm)))
    return jnp.moveaxis(ys, 0, 1).reshape(xdt.shape), h


def ssd_branch(z, xbc_raw, dt_raw, conv_prev, h0, conv_w, conv_b, dt_bias, a_log, d_skip, g_ssd):
    f32 = jnp.float32
    bn, length, _ = xbc_raw.shape
    g, r, p, n = SSD_GROUPS, SSD_HEADS_PER_GROUP, SSD_HEAD_DIM, SSD_STATE
    full = jnp.concatenate([conv_prev.astype(xbc_raw.dtype), xbc_raw], axis=1)
    acc = conv_b.astype(f32)
    for i in range(CONV_WIDTH):
        acc = acc + full[:, i:i + length].astype(f32) * conv_w[i].astype(f32)
    conv_new = full[:, full.shape[1] - (CONV_WIDTH - 1):]
    xbc = jax.nn.silu(acc)
    xs, bm, cm = jnp.split(xbc, [SSD_WIDTH, SSD_WIDTH + g * n], axis=-1)
    xs = xs.reshape(bn, length, g, r, p)
    bm = bm.reshape(bn, length, g, n)
    cm = cm.reshape(bn, length, g, n)
    dt = jax.nn.softplus(dt_raw.astype(f32) + dt_bias.astype(f32)).reshape(bn, length, g, r)
    a = -jnp.exp(a_log.astype(f32)).reshape(g, r)
    y, h_new = ssd_chunked(xs * dt[..., None], dt * a, bm, cm,
                           h0.astype(f32).reshape(bn, g, r, p, n))
    y = y + d_skip.astype(f32).reshape(g, r)[..., None] * xs
    y = y * jax.nn.silu(z.astype(f32)).reshape(bn, length, g, r, p)
    y = rms_norm(y.reshape(bn, length, g, r * p), g_ssd.reshape(g, r * p))
    return y.reshape(bn, length, SSD_WIDTH), conv_new, h_new.reshape(bn, SSD_HEADS, p, n)


def merge_groups(att, ssd, g_att, w_out, dtype):
    bn, h, length, dh = att.shape
    att = rms_norm(att.transpose(0, 2, 1, 3).reshape(bn, length, h * dh), g_att)
    return jnp.concatenate([att, ssd], axis=-1).astype(dtype) @ w_out


def route(t, w_router, router_bias):
    s = jax.nn.sigmoid((t @ w_router).astype(jnp.float32))
    sb = s + router_bias.astype(jnp.float32)
    n_tok = s.shape[0]
    per_group = N_EXPERTS // N_EXPERT_GROUPS
    gscore = lax.top_k(sb.reshape(n_tok, N_EXPERT_GROUPS, per_group), 2)[0].sum(-1)
    _, gsel = lax.top_k(gscore, TOPK_GROUPS)
    gmask = jnp.sum(jax.nn.one_hot(gsel, N_EXPERT_GROUPS), axis=1) > 0
    emask = jnp.repeat(gmask, per_group, axis=1)
    _, sel = lax.top_k(jnp.where(emask, sb, -jnp.inf), TOP_K)
    w = jnp.take_along_axis(s, sel, axis=1)
    return sel, w / jnp.sum(w, axis=-1, keepdims=True) * ROUTED_SCALE


def routed_experts(t, sel, wts, w_gate, w_up, w_down):
    n_tok, d = t.shape
    tk = n_tok * TOP_K
    mb = max(8, min(256, tk // N_EXPERTS))
    n_rows = -(-(tk + N_EXPERTS * (mb - 1)) // mb) * mb
    n_blk = n_rows // mb
    flat_e = sel.reshape(-1)
    order = jnp.argsort(flat_e)
    e_sorted = flat_e[order]
    tok_sorted = (jnp.arange(tk, dtype=jnp.int32) // TOP_K)[order]
    w_sorted = wts.reshape(-1)[order]
    counts = jnp.bincount(flat_e, length=N_EXPERTS)
    start = jnp.cumsum(counts) - counts
    padded = (counts + mb - 1) // mb * mb
    pend = jnp.cumsum(padded)
    pstart = pend - padded
    dest = pstart[e_sorted] + jnp.arange(tk) - start[e_sorted]
    row_tok = jnp.full((n_rows,), n_tok, jnp.int32).at[dest].set(tok_sorted)
    row_w = jnp.zeros((n_rows,), jnp.float32).at[dest].set(w_sorted)
    blk_e = jnp.minimum(jnp.searchsorted(pend, jnp.arange(n_blk) * mb, side='right'), N_EXPERTS - 1)
    t_pad = jnp.concatenate([t, jnp.zeros((1, d), t.dtype)], axis=0)

    def expert_block(args):
        e, toks = args
        xb = t_pad[toks]
        return (jax.nn.silu(xb @ w_gate[e]) * (xb @ w_up[e])) @ w_down[e]

    out = lax.map(expert_block, (blk_e, row_tok.reshape(n_blk, mb)))
    contrib = out.reshape(n_rows, d).astype(jnp.float32) * row_w[:, None]
    return jax.ops.segment_sum(contrib, row_tok, num_segments=n_tok + 1)[:n_tok]


def moe_ffn(h, w_router, router_bias, w_gate, w_up, w_down, ws_gate, ws_up, ws_down):
    bn, length, d = h.shape
    t = h.reshape(bn * length, d)
    sel, wts = route(t, w_router, router_bias)
    shared = (jax.nn.silu(t @ ws_gate) * (t @ ws_up)) @ ws_down
    y = routed_experts(t, sel, wts, w_gate, w_up, w_down) + shared.astype(jnp.float32)
    return y.reshape(bn, length, d)


def setup_inputs(seed: int = 0) -> dict:
    key = jax.random.key(seed)
    ks = jax.random.split(key, 40)
    f32 = jnp.float32
    n_pages = PAST_LEN // PAGE_SIZE
    n_used = DEC_BATCH * n_pages
    n_pool = n_used + n_used // 4

    def nrm(k, shape, s):
        return jax.random.normal(k, shape, f32) * s

    def gain(k, shape):
        return 1.0 + 0.02 * jax.random.normal(k, shape, f32)

    dt0 = jnp.exp(jax.random.uniform(ks[12], (DEPTH, SSD_HEADS), f32, math.log(1e-3), math.log(1e-1)))
    return {
        'x_prompt': nrm(ks[0], (BATCH, SEQ, D_MODEL), 1.0),
        'x_sample': nrm(ks[1], (DEC_BATCH, DEC_SEQ, D_MODEL), 1.0),
        'cache_k': nrm(ks[2], (DEPTH, n_pool, ATT_HEADS, PAGE_SIZE, HEAD_DIM), 1.0),
        'cache_v': nrm(ks[3], (DEPTH, n_pool, ATT_HEADS, PAGE_SIZE, HEAD_DIM), 1.0),
        'page_table': jax.random.permutation(ks[4], n_pool)[:n_used].reshape(DEC_BATCH, n_pages).astype(jnp.int32),
        'state_conv': nrm(ks[5], (DEPTH, DEC_BATCH, CONV_WIDTH - 1, CONV_DIM), 1.0),
        'state_ssm': nrm(ks[6], (DEPTH, DEC_BATCH, SSD_HEADS, SSD_HEAD_DIM, SSD_STATE), 0.5),
        'c_prompt': nrm(ks[7], (BATCH, D_MODEL), 1.0),
        'c_sample': nrm(ks[8], (DEC_BATCH, D_MODEL), 1.0),
        'w_ada': nrm(ks[9], (DEPTH, D_MODEL, 6 * D_MODEL), 0.5 * D_MODEL ** -0.5),
        'b_ada': nrm(ks[10], (DEPTH, 6 * D_MODEL), 0.02),
        'g_mix': gain(ks[11], (DEPTH, D_MODEL)),
        'w_in': nrm(ks[13], (DEPTH, D_MODEL, IN_WIDTH), D_MODEL ** -0.5),
        'conv_w': nrm(ks[14], (DEPTH, CONV_WIDTH, CONV_DIM), CONV_WIDTH ** -0.5),
        'conv_b': nrm(ks[15], (DEPTH, CONV_DIM), 0.02),
        'dt_bias': dt0 + jnp.log(-jnp.expm1(-dt0)),
        'a_log': jnp.log(jax.random.uniform(ks[16], (DEPTH, SSD_HEADS), f32, 1.0, 16.0)),
        'd_skip': gain(ks[17], (DEPTH, SSD_HEADS)),
        'g_ssd': gain(ks[18], (DEPTH, SSD_WIDTH)),
        'g_att': gain(ks[19], (DEPTH, ATT_WIDTH)),
        'w_out': nrm(ks[20], (DEPTH, MIX_WIDTH, D_MODEL), MIX_WIDTH ** -0.5),
        'g_ffn': gain(ks[21], (DEPTH, D_MODEL)),
        'w_router': nrm(ks[22], (DEPTH, D_MODEL, N_EXPERTS), D_MODEL ** -0.5),
        'router_bias': nrm(ks[23], (DEPTH, N_EXPERTS), 0.01),
        'w_gate': nrm(ks[24], (DEPTH, N_EXPERTS, D_MODEL, EXPERT_DIM), D_MODEL ** -0.5),
        'w_up': nrm(ks[25], (DEPTH, N_EXPERTS, D_MODEL, EXPERT_DIM), D_MODEL ** -0.5),
        'w_down': nrm(ks[26], (DEPTH, N_EXPERTS, EXPERT_DIM, D_MODEL), EXPERT_DIM ** -0.5),
        'ws_gate': nrm(ks[27], (DEPTH, D_MODEL, SHARED_DIM), D_MODEL ** -0.5),
        'ws_up': nrm(ks[28], (DEPTH, D_MODEL, SHARED_DIM), D_MODEL ** -0.5),
        'ws_down': nrm(ks[29], (DEPTH, SHARED_DIM, D_MODEL), SHARED_DIM ** -0.5),
        'g_final': gain(ks[30], (D_MODEL,)),
    }


def reference(x_prompt, x_sample, cache_k, cache_v, page_table, state_conv, state_ssm, c_prompt, c_sample,
              w_ada, b_ada, g_mix, w_in, conv_w, conv_b, dt_bias, a_log, d_skip, g_ssd, g_att, w_out,
              g_ffn, w_router, router_bias, w_gate, w_up, w_down, ws_gate, ws_up, ws_down, g_final):
    dtype = x_prompt.dtype
    xp, xs = x_prompt, x_sample
    kp_l, vp_l, ks_l, vs_l, cp_l, cs_l, hp_l, hs_l = [], [], [], [], [], [], [], []
    for l in range(DEPTH):
        ssd_w = (conv_w[l], conv_b[l], dt_bias[l], a_log[l], d_skip[l], g_ssd[l])
        moe_w = (w_router[l], router_bias[l], w_gate[l], w_up[l], w_down[l], ws_gate[l], ws_up[l], ws_down[l])
        sh1p, sc1p, gt1p, sh2p, sc2p, gt2p = adaln(c_prompt, w_ada[l], b_ada[l])
        sh1s, sc1s, gt1s, sh2s, sc2s, gt2s = adaln(c_sample, w_ada[l], b_ada[l])

        hp = modulate(rms_norm(xp, g_mix[l]), sh1p, sc1p, dtype)
        q, k, v, z, xbc, dt = split_proj(hp @ w_in[l])
        qh, kh, vh = split_heads(q), split_heads(k), split_heads(v)
        att = moba_prompt(qh, kh, vh)
        conv0 = jnp.zeros((xp.shape[0], CONV_WIDTH - 1, CONV_DIM), xbc.dtype)
        h0 = jnp.zeros((xp.shape[0], SSD_HEADS, SSD_HEAD_DIM, SSD_STATE), jnp.float32)
        ssd, conv_p, ssm_p = ssd_branch(z, xbc, dt, conv0, h0, *ssd_w)
        xp = xp + (gt1p * merge_groups(att, ssd, g_att[l], w_out[l], dtype)).astype(dtype)
        kp_l.append(to_pages(kh))
        vp_l.append(to_pages(vh))
        cp_l.append(conv_p)
        hp_l.append(ssm_p)

        hs = modulate(rms_norm(xs, g_mix[l]), sh1s, sc1s, dtype)
        q, k, v, z, xbc, dt = split_proj(hs @ w_in[l])
        qh, kh, vh = split_heads(q), split_heads(k), split_heads(v)
        att = moba_sample(qh, kh, vh, cache_k, cache_v, page_table, l)
        ssd, conv_s, ssm_s = ssd_branch(z, xbc, dt, state_conv[l], state_ssm[l], *ssd_w)
        xs = xs + (gt1s * merge_groups(att, ssd, g_att[l], w_out[l], dtype)).astype(dtype)
        ks_l.append(kh)
        vs_l.append(vh)
        cs_l.append(conv_s)
        hs_l.append(ssm_s)

        xp = xp + (gt2p * moe_ffn(modulate(rms_norm(xp, g_ffn[l]), sh2p, sc2p, dtype), *moe_w)).astype(dtype)
        xs = xs + (gt2s * moe_ffn(modulate(rms_norm(xs, g_ffn[l]), sh2s, sc2s, dtype), *moe_w)).astype(dtype)

    y_prompt = rms_norm(xp, g_final).astype(dtype)
    y_sample = rms_norm(xs, g_final).astype(dtype)
    k_prompt = jnp.stack(kp_l)
    v_prompt = jnp.stack(vp_l)
    k_sample = jnp.stack(ks_l)
    v_sample = jnp.stack(vs_l)
    conv_prompt = jnp.stack(cp_l)
    conv_sample = jnp.stack(cs_l)
    ssm_prompt = jnp.stack(hp_l)
    ssm_sample = jnp.stack(hs_l)
    return (y_prompt, y_sample, k_prompt, v_prompt, k_sample, v_sample, conv_prompt, conv_sample, ssm_prompt, ssm_sample)
```

```python
import functools

import jax
import jax.numpy as jnp
from jax import lax
from jax.experimental import pallas as pl
from jax.experimental.pallas import tpu as pltpu

F32 = jnp.float32
BF16 = jnp.bfloat16
I32 = jnp.int32

NORM_EPS = 1e-6
HEAD_DIM = 64
MOBA_BLOCK = 256
MOBA_TOPK = 3
PAGE_SIZE = 128
SSD_HEAD_DIM = 64
SSD_GROUPS = 2
SSD_STATE = 128
CONV_WIDTH = 4
N_EXPERT_GROUPS = 8
TOPK_GROUPS = 4
TOP_K = 8
ROUTED_SCALE = 2.5
MASKED = -1e30
TOKEN_TILE = 256
EXPERT_TILE = 256
VMEM_LIMIT = 56 * 1024 * 1024


def _params(sem, vmem=VMEM_LIMIT):
    return pltpu.CompilerParams(dimension_semantics=sem, vmem_limit_bytes=vmem)


def _silu(x):
    return x * jax.nn.sigmoid(x)


def _softplus(x):
    return jnp.maximum(x, 0.0) + jnp.log1p(jnp.exp(-jnp.abs(x)))


def _rms(x, g):
    return x * lax.rsqrt(jnp.mean(x * x, axis=-1, keepdims=True) + NORM_EPS) * g


def _dot(a, b):
    return jnp.dot(a, b, preferred_element_type=F32)


def _dot_nt(a, b):
    return lax.dot_general(a, b, (((1,), (1,)), ((), ())), preferred_element_type=F32)


def _dot_exact(a, b):
    return jnp.dot(a, b, preferred_element_type=F32, precision=lax.Precision.HIGHEST)


def _adaln_body(c_ref, w_ref, b_ref, o_ref):
    a = _silu(c_ref[...]).astype(BF16)
    o_ref[...] = _dot(a, w_ref[...].astype(BF16)) + b_ref[...]


def _adaln(c, w_ada, b_ada):
    r, d = c.shape
    n = w_ada.shape[1]
    tn = 512
    return pl.pallas_call(
        _adaln_body,
        out_shape=jax.ShapeDtypeStruct((r, n), F32),
        grid=(n // tn,),
        in_specs=[pl.BlockSpec((r, d), lambda j: (0, 0)),
                  pl.BlockSpec((d, tn), lambda j: (0, j)),
                  pl.BlockSpec((1, tn), lambda j: (0, j))],
        out_specs=pl.BlockSpec((r, tn), lambda j: (0, j)),
        compiler_params=_params(("arbitrary",)),
        name="adaln",
    )(c, w_ada, b_ada.reshape(1, n))


def _inproj_body(x_ref, sh_ref, sc_ref, g_ref, w_ref, *outs, att_w, ssd_w, conv_dim, prompt):
    x = x_ref[...]
    h = (_rms(x, g_ref[...]) * (1.0 + sc_ref[0]) + sh_ref[0]).astype(BF16)
    tm = x.shape[0]
    a = att_w
    c0 = 3 * a
    c1 = c0 + ssd_w
    c2 = c1 + conv_dim
    c3 = c2 + ssd_w

    def proj(lo, hi):
        return _dot(h, w_ref[:, lo:hi])

    q, k, v = proj(0, a), proj(a, 2 * a), proj(2 * a, c0)
    if prompt:
        (qt_ref, kaug_ref, vt_ref, kp_ref, vp_ref, km_ref, z_ref, xbc_ref, dte_ref, dtc_ref) = outs
        nh = a // HEAD_DIM
        qt_ref[0] = q.T.reshape(nh, HEAD_DIM, tm)
        vt_ref[0] = v.T.reshape(nh, HEAD_DIM, tm).astype(BF16)
        km_ref[0] = jnp.mean(k, axis=0, keepdims=True)
        lane = lax.broadcasted_iota(I32, (tm, 2 * HEAD_DIM), 1)
        row = lax.broadcasted_iota(I32, (tm, 2 * HEAD_DIM), 0)
        pos = jnp.where(lane == HEAD_DIM, (row % MOBA_BLOCK).astype(F32), 0.0)
        low = lane < HEAD_DIM
        for hp in range(nh // 2):
            kc = k[:, hp * 128:(hp + 1) * 128]
            vc = v[:, hp * 128:(hp + 1) * 128]
            kr = pltpu.roll(kc, HEAD_DIM, 1)
            vr = pltpu.roll(vc, HEAD_DIM, 1)
            kaug_ref[0, 2 * hp] = jnp.where(low, kc, pos).astype(BF16)
            kaug_ref[0, 2 * hp + 1] = jnp.where(low, kr, pos).astype(BF16)
            for pg in range(tm // PAGE_SIZE):
                rs = slice(pg * PAGE_SIZE, (pg + 1) * PAGE_SIZE)
                kp_ref[0, pg, 2 * hp] = kc[rs, :HEAD_DIM]
                kp_ref[0, pg, 2 * hp + 1] = kr[rs, :HEAD_DIM]
                vp_ref[0, pg, 2 * hp] = vc[rs, :HEAD_DIM]
                vp_ref[0, pg, 2 * hp + 1] = vr[rs, :HEAD_DIM]
    else:
        (q_ref, k_ref, v_ref, z_ref, xbc_ref, dte_ref, dtc_ref) = outs
        q_ref[...] = q
        k_ref[...] = k
        v_ref[...] = v
    z_ref[...] = proj(c0, c1)
    xbc_ref[...] = proj(c1, c2)
    dte_ref[...] = proj(c2, c3)
    dtc_ref[...] = proj(c3, c3 + 128)


def _inproj(x2d, shift, scale, g, w_all, *, att_w, ssd_w, conv_dim, prompt_batch=None):
    t, d = x2d.shape
    tm = TOKEN_TILE
    nw = w_all.shape[1]
    nt = t // tm
    prompt = prompt_batch is not None
    tok = lambda width: pl.BlockSpec((tm, width), lambda i: (i, 0))
    common_shapes = [jax.ShapeDtypeStruct((t, ssd_w), F32), jax.ShapeDtypeStruct((t, conv_dim), F32),
                     jax.ShapeDtypeStruct((t, ssd_w), F32), jax.ShapeDtypeStruct((t, 128), F32)]
    common_specs = [tok(ssd_w), tok(conv_dim), tok(ssd_w), tok(128)]
    if prompt:
        b, s = prompt_batch
        tps = s // tm
        nh = att_w // HEAD_DIM
        ppt = tm // PAGE_SIZE
        mod_map = lambda i: (i // tps, 0, 0)
        out_shape = [jax.ShapeDtypeStruct((b, nh, HEAD_DIM, s), F32),
                     jax.ShapeDtypeStruct((b, nh, s, 2 * HEAD_DIM), BF16),
                     jax.ShapeDtypeStruct((b, nh, HEAD_DIM, s), BF16),
                     jax.ShapeDtypeStruct((b, s // PAGE_SIZE, nh, PAGE_SIZE, HEAD_DIM), F32),
                     jax.ShapeDtypeStruct((b, s // PAGE_SIZE, nh, PAGE_SIZE, HEAD_DIM), F32),
                     jax.ShapeDtypeStruct((nt, 1, att_w), F32)] + common_shapes
        out_specs = [pl.BlockSpec((1, nh, HEAD_DIM, tm), lambda i: (i // tps, 0, 0, i % tps)),
                     pl.BlockSpec((1, nh, tm, 2 * HEAD_DIM), lambda i: (i // tps, 0, i % tps, 0)),
                     pl.BlockSpec((1, nh, HEAD_DIM, tm), lambda i: (i // tps, 0, 0, i % tps)),
                     pl.BlockSpec((1, ppt, nh, PAGE_SIZE, HEAD_DIM), lambda i: (i // tps, i % tps, 0, 0, 0)),
                     pl.BlockSpec((1, ppt, nh, PAGE_SIZE, HEAD_DIM), lambda i: (i // tps, i % tps, 0, 0, 0)),
                     pl.BlockSpec((1, 1, att_w), lambda i: (i, 0, 0))] + common_specs
        mod_block = (1, 1, d)
    else:
        mod_map = lambda i: (i, 0, 0)
        out_shape = [jax.ShapeDtypeStruct((t, att_w), F32)] * 3 + common_shapes
        out_specs = [tok(att_w)] * 3 + common_specs
        mod_block = (1, tm, d)
    body = functools.partial(_inproj_body, att_w=att_w, ssd_w=ssd_w, conv_dim=conv_dim, prompt=prompt)
    return pl.pallas_call(
        body,
        out_shape=out_shape,
        grid=(nt,),
        in_specs=[tok(d),
                  pl.BlockSpec(mod_block, mod_map),
                  pl.BlockSpec(mod_block, mod_map),
                  pl.BlockSpec((1, d), lambda i: (0, 0)),
                  pl.BlockSpec((d, nw), lambda i: (0, 0))],
        out_specs=out_specs,
        compiler_params=_params(("arbitrary",)),
        name="inproj_prompt" if prompt else "inproj_sample",
    )(x2d, shift, scale, g.reshape(1, d), w_all)


def _select_topk_rows(gate, n_valid, k):
    nrow = gate.shape[0]
    row = lax.broadcasted_iota(I32, gate.shape, 0)
    g = jnp.where(row < n_valid, gate, -jnp.inf)
    chosen = jnp.zeros(gate.shape, I32)
    for _ in range(k):
        m = jnp.max(g, axis=0, keepdims=True)
        cand = jnp.logical_and(g == m, g > -jnp.inf)
        first = jnp.min(jnp.where(cand, row, nrow), axis=0, keepdims=True)
        pick = row == first
        chosen = jnp.where(pick, 1, chosen)
        g = jnp.where(pick, -jnp.inf, g)
    return chosen


def _attn_prompt_body(slope_ref, qt_ref, k_ref, vt_ref, km_ref, o_ref, bias_ref):
    h = pl.program_id(1)
    t = pl.program_id(2)
    slope = slope_ref[h]
    blk = MOBA_BLOCK
    qt = qt_ref[0, 0]
    gate = _dot_exact(km_ref[0, 0], qt)
    chosen = _select_topk_rows(gate, t, MOBA_TOPK)
    jrow = lax.broadcasted_iota(I32, gate.shape, 0)
    iq = lax.broadcasted_iota(I32, gate.shape, 1)
    dist0 = ((t - jrow) * blk + iq).astype(F32)
    bias_ref[...] = jnp.where(chosen > 0, -slope * dist0, MASKED)

    r2 = lax.broadcasted_iota(I32, qt.shape, 0)
    extra = jnp.where(r2 == 0, slope, 0.0)
    qaug = jnp.concatenate([qt * (HEAD_DIM ** -0.5), extra], axis=0).astype(BF16)

    start = pl.multiple_of(t * blk, blk)
    s = _dot(k_ref[0, 0, pl.ds(start, blk), :], qaug)
    ik = lax.broadcasted_iota(I32, s.shape, 0)
    iqq = lax.broadcasted_iota(I32, s.shape, 1)
    s = jnp.where(ik <= iqq, s - slope * iqq.astype(F32), MASKED)
    m = jnp.max(s, axis=0, keepdims=True)
    p = jnp.exp(s - m)
    l = jnp.sum(p, axis=0, keepdims=True)
    acc = _dot(vt_ref[0, 0, :, pl.ds(start, blk)], p.astype(BF16))

    def body(j, carry):
        m, l, acc = carry
        st = pl.multiple_of(j * blk, blk)
        s = _dot(k_ref[0, 0, pl.ds(st, blk), :], qaug) + bias_ref[pl.ds(j, 1), :]
        m_new = jnp.maximum(m, jnp.max(s, axis=0, keepdims=True))
        alpha = jnp.exp(m - m_new)
        p = jnp.exp(s - m_new)
        l = alpha * l + jnp.sum(p, axis=0, keepdims=True)
        acc = alpha * acc + _dot(vt_ref[0, 0, :, pl.ds(st, blk)], p.astype(BF16))
        return m_new, l, acc

    m, l, acc = lax.fori_loop(0, t, body, (m, l, acc))
    o_ref[0, 0] = acc / l


def _attn_prompt(slopes, qt, kaug, vt, kmean):
    b, nh, dh, s = qt.shape
    blk = MOBA_BLOCK
    nb = s // blk
    return pl.pallas_call(
        _attn_prompt_body,
        out_shape=jax.ShapeDtypeStruct((b, nh, dh, s), F32),
        grid_spec=pltpu.PrefetchScalarGridSpec(
            num_scalar_prefetch=1,
            grid=(b, nh, nb),
            in_specs=[pl.BlockSpec((1, 1, dh, blk), lambda bi, hi, ti, sl: (bi, hi, 0, ti)),
                      pl.BlockSpec((1, 1, s, 2 * dh), lambda bi, hi, ti, sl: (bi, hi, 0, 0)),
                      pl.BlockSpec((1, 1, dh, s), lambda bi, hi, ti, sl: (bi, hi, 0, 0)),
                      pl.BlockSpec((1, 1, nb, dh), lambda bi, hi, ti, sl: (bi, hi, 0, 0))],
            out_specs=pl.BlockSpec((1, 1, dh, blk), lambda bi, hi, ti, sl: (bi, hi, 0, ti)),
            scratch_shapes=[pltpu.VMEM((nb, blk), F32)]),
        compiler_params=_params(("arbitrary", "arbitrary", "arbitrary")),
        name="moba_prompt",
    )(slopes, qt, kaug, vt, kmean)


def _ssd_body(xbc_ref, z_ref, dte_ref, dtc_ref, cprev_ref, h0_ref, cw_ref, cb_ref, dtbe_ref, dtbc_ref,
              ale_ref, alc_ref, dsk_ref, g_ref, y_ref, hout_ref, buf_ref, st_ref, *, rows, length):
    c = pl.program_id(1)
    ng = SSD_GROUPS
    n = SSD_STATE
    width = z_ref.shape[-1]
    gw = width // ng

    @pl.when(c == 0)
    def _():
        if rows < length:
            buf_ref[...] = jnp.zeros(buf_ref.shape, F32)
        buf_ref[0:8, :] = cprev_ref[0]
        for g in range(ng):
            st_ref[g] = h0_ref[0, g * gw:(g + 1) * gw, :].T

    buf_ref[8:8 + rows, :] = xbc_ref[0]
    acc = cb_ref[...]
    for i in range(CONV_WIDTH):
        off = 8 - (CONV_WIDTH - 1) + i
        acc = acc + buf_ref[off:off + length, :] * cw_ref[i:i + 1, :]
    if rows == length:
        buf_ref[0:8, :] = buf_ref[length:length + 8, :]
    xc = _silu(acc)
    xs = xc[:, :width]
    bm = xc[:, width:width + ng * n]
    cm = xc[:, width + ng * n:]

    if rows < length:
        dte_full = jnp.concatenate([dte_ref[0], jnp.zeros((length - rows, width), F32)], axis=0)
        dtc_full = jnp.concatenate([dtc_ref[0], jnp.zeros((length - rows, 128), F32)], axis=0)
        live_e = lax.broadcasted_iota(I32, (length, width), 0) < rows
        live_c = lax.broadcasted_iota(I32, (length, 128), 0) < rows
        dt = jnp.where(live_e, _softplus(dte_full + dtbe_ref[...]), 0.0)
        dtc = jnp.where(live_c, _softplus(dtc_full + dtbc_ref[...]), 0.0)
    else:
        dt = _softplus(dte_ref[0] + dtbe_ref[...])
        dtc = _softplus(dtc_ref[0] + dtbc_ref[...])
    da = dt * (-jnp.exp(ale_ref[...]))
    dac = dtc * (-jnp.exp(alc_ref[...]))
    xdt = xs * dt

    ti = lax.broadcasted_iota(I32, (length, length), 0)
    si = lax.broadcasted_iota(I32, (length, length), 1)
    causal = ti >= si
    tri = jnp.where(causal, 1.0, 0.0)
    acs = _dot_exact(tri, da)
    acs_t = _dot_exact(tri, dac).T
    acs_last = acs[length - 1:length, :]
    e_acs = jnp.exp(acs)
    w_tail = jnp.exp(acs_last - acs)
    lane = lax.broadcasted_iota(I32, (length, 128), 1)

    ys = []
    for g in range(ng):
        bg = bm[:, g * n:(g + 1) * n]
        cg = cm[:, g * n:(g + 1) * n].astype(BF16)
        cb = _dot_nt(cg, bg.astype(BF16))
        st = st_ref[g]
        yg = _dot(cg, st.astype(BF16)) * e_acs[:, g * gw:(g + 1) * gw]
        pieces = []
        for pr in range(gw // 128):
            x_pair = xdt[:, g * gw + pr * 128: g * gw + (pr + 1) * 128].astype(BF16)
            halves = []
            for hh in range(2):
                head = (g * gw + pr * 128) // SSD_HEAD_DIM + hh
                col = acs[:, head * SSD_HEAD_DIM: head * SSD_HEAD_DIM + 1]
                rowv = acs_t[head:head + 1, :]
                decay = jnp.where(causal, jnp.exp(col - rowv), 0.0)
                halves.append(_dot((cb * decay).astype(BF16), x_pair))
            pieces.append(jnp.where(lane < SSD_HEAD_DIM, halves[0], halves[1]))
        ys.append(yg + jnp.concatenate(pieces, axis=1))
        xw = (xdt[:, g * gw:(g + 1) * gw] * w_tail[:, g * gw:(g + 1) * gw]).astype(BF16)
        st_ref[g] = st * jnp.exp(acs_last[:, g * gw:(g + 1) * gw]) + _dot(bg.T.astype(BF16), xw)
    y = jnp.concatenate(ys, axis=1)
    y = y + dsk_ref[...] * xs
    y = y[:rows] * _silu(z_ref[0])
    outs = []
    for g in range(ng):
        outs.append(_rms(y[:, g * gw:(g + 1) * gw], g_ref[:, g * gw:(g + 1) * gw]))
    y_ref[0] = jnp.concatenate(outs, axis=1)

    @pl.when(c == pl.num_programs(1) - 1)
    def _():
        for g in range(ng):
            hout_ref[0, g * gw:(g + 1) * gw, :] = st_ref[g].T


def _ssd(xbc, z, dte, dtc, conv_prev, h0, conv_w, conv_b, dt_bias, a_log, d_skip, g_ssd, *, length):
    bn, lt, cdim = xbc.shape
    width = z.shape[-1]
    n = h0.shape[-1]
    rows = min(lt, length)
    nc = lt // rows
    rep = width // dt_bias.shape[0]
    expand = lambda u: jnp.repeat(u, rep).reshape(1, width)
    lane_pad = lambda u: jnp.pad(u, (0, 128 - u.shape[0])).reshape(1, 128)
    seq = lambda w_: pl.BlockSpec((1, rows, w_), lambda b, c: (b, c, 0))
    const = lambda shape: pl.BlockSpec(shape, lambda b, c: (0,) * len(shape))
    body = functools.partial(_ssd_body, rows=rows, length=length)
    return pl.pallas_call(
        body,
        out_shape=[jax.ShapeDtypeStruct((bn, lt, width), F32), jax.ShapeDtypeStruct((bn, width, n), F32)],
        grid=(bn, nc),
        in_specs=[seq(cdim), seq(width), seq(width), seq(128),
                  pl.BlockSpec((1, 8, cdim), lambda b, c: (b, 0, 0)),
                  pl.BlockSpec((1, width, n), lambda b, c: (b, 0, 0)),
                  const((CONV_WIDTH, cdim)), const((1, cdim)), const((1, width)), const((1, 128)),
                  const((1, width)), const((1, 128)), const((1, width)), const((1, width))],
        out_specs=[seq(width), pl.BlockSpec((1, width, n), lambda b, c: (b, 0, 0))],
        scratch_shapes=[pltpu.VMEM((length + 8, cdim), F32),
                        pltpu.VMEM((SSD_GROUPS, n, width // SSD_GROUPS), F32)],
        compiler_params=_params(("arbitrary", "arbitrary")),
        name="ssd_scan",
    )(xbc, z, dte, dtc, conv_prev, h0, conv_w, conv_b.reshape(1, cdim), expand(dt_bias), lane_pad(dt_bias),
      expand(a_log), lane_pad(a_log), expand(d_skip), g_ssd.reshape(1, width))


def _outproj_body(att_ref, ssd_ref, x_ref, gt_ref, ga_ref, w_ref, o_ref, *, transposed):
    if transposed:
        a = att_ref[0].reshape(-1, att_ref.shape[-1]).T
    else:
        a = att_ref[...]
    mix = jnp.concatenate([_rms(a, ga_ref[...]), ssd_ref[...]], axis=-1).astype(BF16)
    o_ref[...] = x_ref[...] + gt_ref[0] * _dot(mix, w_ref[...])


def _outproj(att, ssd, x2d, gate, g_att, w_out, *, prompt_batch=None):
    t, d = x2d.shape
    tm = TOKEN_TILE
    tok = lambda width: pl.BlockSpec((tm, width), lambda i: (i, 0))
    sw = ssd.shape[-1]
    if prompt_batch is not None:
        b, s = prompt_batch
        tps = s // tm
        nh, dh = att.shape[1], att.shape[2]
        aw = nh * dh
        att_spec = pl.BlockSpec((1, nh, dh, tm), lambda i: (i // tps, 0, 0, i % tps))
        gate_spec = pl.BlockSpec((1, 1, d), lambda i: (i // tps, 0, 0))
    else:
        aw = att.shape[-1]
        att_spec = tok(aw)
        gate_spec = pl.BlockSpec((1, tm, d), lambda i: (i, 0, 0))
    return pl.pallas_call(
        functools.partial(_outproj_body, transposed=prompt_batch is not None),
        out_shape=jax.ShapeDtypeStruct((t, d), F32),
        grid=(t // tm,),
        in_specs=[att_spec, tok(sw), tok(d), gate_spec,
                  pl.BlockSpec((1, aw), lambda i: (0, 0)),
                  pl.BlockSpec((aw + sw, d), lambda i: (0, 0))],
        out_specs=tok(d),
        compiler_params=_params(("arbitrary",)),
        name="outproj",
    )(att, ssd, x2d, gate, g_att.reshape(1, aw), w_out)


def _route_t(st, sbt):
    e, tm = st.shape
    per = e // N_EXPERT_GROUPS
    row = lax.broadcasted_iota(I32, (e, tm), 0)
    rg = lax.broadcasted_iota(I32, (per, tm), 0)
    gs = []
    for g in range(N_EXPERT_GROUPS):
        blk = sbt[g * per:(g + 1) * per]
        m1 = jnp.max(blk, axis=0, keepdims=True)
        first = jnp.min(jnp.where(blk == m1, rg, per), axis=0, keepdims=True)
        m2 = jnp.max(jnp.where(rg == first, -jnp.inf, blk), axis=0, keepdims=True)
        gs.append(m1 + m2)
    gscore = jnp.concatenate(gs, axis=0)
    gch = _select_topk_rows(gscore, N_EXPERT_GROUPS, TOPK_GROUPS)
    emask = jnp.concatenate([jnp.broadcast_to(gch[g:g + 1], (per, tm)) for g in range(N_EXPERT_GROUPS)], axis=0)
    cur = jnp.where(emask > 0, sbt, -jnp.inf)
    sel, wts = [], []
    for _ in range(TOP_K):
        m = jnp.max(cur, axis=0, keepdims=True)
        first = jnp.min(jnp.where(cur == m, row, e), axis=0, keepdims=True)
        pick = row == first
        sel.append(first)
        wts.append(jnp.sum(jnp.where(pick, st, 0.0), axis=0, keepdims=True))
        cur = jnp.where(pick, -jnp.inf, cur)
    sel = jnp.concatenate(sel, axis=0)
    w = jnp.concatenate(wts, axis=0)
    return sel, w / jnp.sum(w, axis=0, keepdims=True) * ROUTED_SCALE


def _moe_pre_body(x_ref, sh_ref, sc_ref, g_ref, wr_ref, rb_ref, wsg_ref, wsu_ref, wsd_ref,
                  t_ref, shared_ref, sel_ref, wt_ref):
    t = _rms(x_ref[...], g_ref[...]) * (1.0 + sc_ref[0]) + sh_ref[0]
    t_ref[...] = t
    tb = t.astype(BF16)
    st = jax.nn.sigmoid(_dot(tb, wr_ref[...])).T
    sel, w = _route_t(st, st + rb_ref[...])
    sel_ref[0] = sel
    wt_ref[0] = w
    hs = _silu(_dot(tb, wsg_ref[...])) * _dot(tb, wsu_ref[...])
    shared_ref[...] = _dot(hs.astype(BF16), wsd_ref[...])


def _moe_pre(x2d, shift, scale, mod_of_tile, g, w_router, router_bias, ws_gate, ws_up, ws_down):
    t, d = x2d.shape
    tm = TOKEN_TILE
    nt = t // tm
    e = w_router.shape[1]
    sd = ws_gate.shape[1]
    tok = lambda width: pl.BlockSpec((tm, width), lambda i: (i, 0))
    mod = pl.BlockSpec((1, tm, d), lambda i: (mod_of_tile(i), 0, 0))
    full = lambda shape: pl.BlockSpec(shape, lambda i: (0,) * len(shape))
    return pl.pallas_call(
        _moe_pre_body,
        out_shape=[jax.ShapeDtypeStruct((t, d), F32), jax.ShapeDtypeStruct((t, d), F32),
                   jax.ShapeDtypeStruct((nt, TOP_K, tm), I32), jax.ShapeDtypeStruct((nt, TOP_K, tm), F32)],
        grid=(nt,),
        in_specs=[tok(d), mod, mod, full((1, d)), full((d, e)), full((e, 1)),
                  full((d, sd)), full((d, sd)), full((sd, d))],
        out_specs=[tok(d), tok(d),
                   pl.BlockSpec((1, TOP_K, tm), lambda i: (i, 0, 0)),
                   pl.BlockSpec((1, TOP_K, tm), lambda i: (i, 0, 0))],
        compiler_params=_params(("arbitrary",)),
        name="moe_pre",
    )(x2d, shift, scale, g.reshape(1, d), w_router, router_bias.reshape(e, 1), ws_gate, ws_up, ws_down)


def _experts_body(blke_ref, nact_ref, tok_ref, dst_ref, roww_ref, t_hbm, wg_ref, wu_ref, wd_ref, out_hbm,
                  xbuf, ybuf, wgb, wub, wdb, gsem, ssem):
    i = pl.program_id(0)
    nact = nact_ref[0]
    tile = xbuf.shape[1]

    def gather(r, slot):
        return pltpu.make_async_copy(t_hbm.at[pl.ds(tok_ref[0, 0, r], 1)], xbuf.at[slot, pl.ds(r, 1)], gsem.at[slot])

    def scatter(r, slot):
        return pltpu.make_async_copy(ybuf.at[slot, pl.ds(r, 1)], out_hbm.at[pl.ds(dst_ref[0, 0, r], 1)], ssem.at[slot])

    def all_rows_done(buf, sem, slot):
        pltpu.make_async_copy(buf.at[slot], buf.at[slot], sem.at[slot]).wait()

    @pl.when(i < nact)
    def _():
        slot = i % 2

        def issue(r, _):
            gather(r, slot).start()
            return 0
        lax.fori_loop(0, tile, issue, 0, unroll=8)

    j = i - 1

    @pl.when(jnp.logical_and(j >= 0, j < nact))
    def _():
        slot = j % 2
        all_rows_done(xbuf, gsem, slot)

        @pl.when(j >= 2)
        def _():
            all_rows_done(ybuf, ssem, slot)

        changed = jnp.logical_or(j == 0, blke_ref[j] != blke_ref[jnp.maximum(j - 1, 0)])

        @pl.when(changed)
        def _():
            wgb[...] = wg_ref[0].astype(BF16)
            wub[...] = wu_ref[0].astype(BF16)
            wdb[...] = wd_ref[0].astype(BF16)

        x = xbuf[slot].astype(BF16)
        hmid = _silu(_dot(x, wgb[...])) * _dot(x, wub[...])
        ybuf[slot] = _dot(hmid.astype(BF16), wdb[...]) * roww_ref[0]

        def issue(r, _):
            scatter(r, slot).start()
            return 0
        lax.fori_loop(0, tile, issue, 0, unroll=8)

        @pl.when(j == nact - 1)
        def _():
            all_rows_done(ybuf, ssem, slot)

            @pl.when(j >= 1)
            def _():
                all_rows_done(ybuf, ssem, 1 - slot)


def _experts(blk_e, n_act, row_tok, row_dst, row_w, t2d, w_gate, w_up, w_down, n_out_rows):
    n_blk, _, tile = row_tok.shape
    d = t2d.shape[1]
    e, _, ed = w_gate.shape
    smem_blk = lambda fn: pl.BlockSpec((1, 1, tile), fn, memory_space=pltpu.SMEM)
    prev = lambda i, be, na: (jnp.maximum(i - 1, 0), 0, 0)
    wmap = lambda i, be, na: (be[jnp.maximum(i - 1, 0)], 0, 0)
    return pl.pallas_call(
        _experts_body,
        out_shape=jax.ShapeDtypeStruct((n_out_rows, d), F32),
        grid_spec=pltpu.PrefetchScalarGridSpec(
            num_scalar_prefetch=2,
            grid=(n_blk + 1,),
            in_specs=[smem_blk(lambda i, be, na: (jnp.minimum(i, n_blk - 1), 0, 0)),
                      smem_blk(prev),
                      pl.BlockSpec((1, tile, 1), prev),
                      pl.BlockSpec(memory_space=pl.ANY),
                      pl.BlockSpec((1, d, ed), wmap),
                      pl.BlockSpec((1, d, ed), wmap),
                      pl.BlockSpec((1, ed, d), wmap)],
            out_specs=pl.BlockSpec(memory_space=pl.ANY),
            scratch_shapes=[pltpu.VMEM((2, tile, d), F32), pltpu.VMEM((2, tile, d), F32),
                            pltpu.VMEM((d, ed), BF16), pltpu.VMEM((d, ed), BF16), pltpu.VMEM((ed, d), BF16),
                            pltpu.SemaphoreType.DMA((2,)), pltpu.SemaphoreType.DMA((2,))]),
        compiler_params=_params(("arbitrary",)),
        name="moe_experts",
    )(blk_e, n_act, row_tok, row_dst, row_w.reshape(n_blk, tile, 1), t2d, w_gate, w_up, w_down)


def _routing_plan(sel, wts, n_experts, tile):
    n_tok, k = sel.shape
    tk = n_tok * k
    n_rows = -(-(tk + n_experts * (tile - 1)) // tile) * tile
    n_blk = n_rows // tile
    flat_e = sel.reshape(-1)
    order = jnp.argsort(flat_e).astype(I32)
    e_sorted = flat_e[order]
    counts = jnp.bincount(flat_e, length=n_experts).astype(I32)
    start = jnp.cumsum(counts) - counts
    padded = (counts + tile - 1) // tile * tile
    pend = jnp.cumsum(padded)
    pstart = pend - padded
    dest = pstart[e_sorted] + jnp.arange(tk, dtype=I32) - start[e_sorted]
    row_pair = jnp.full((n_rows,), -1, I32).at[dest].set(order)
    live = row_pair >= 0
    safe = jnp.maximum(row_pair, 0)
    row_tok = jnp.where(live, safe // k, 0)
    row_dst = jnp.where(live, row_pair, tk + jnp.arange(n_rows, dtype=I32) % tile)
    row_w = jnp.where(live, wts.reshape(-1)[safe], 0.0)
    blk_e = jnp.minimum(jnp.searchsorted(pend, jnp.arange(n_blk, dtype=I32) * tile, side='right'),
                        n_experts - 1).astype(I32)
    n_act = (pend[-1] // tile).astype(I32).reshape(1)
    shape = (n_blk, 1, tile)
    return blk_e, n_act, row_tok.reshape(shape), row_dst.reshape(shape), row_w.reshape(shape)


def _combine_body(x_ref, r8_ref, shared_ref, gt_ref, gf_ref, o_ref, *, final_norm):
    d = x_ref.shape[-1]
    routed = r8_ref[:, 0:d]
    for k in range(1, TOP_K):
        routed = routed + r8_ref[:, k * d:(k + 1) * d]
    y = x_ref[...] + gt_ref[0] * (routed + shared_ref[...])
    o_ref[...] = _rms(y, gf_ref[...]) if final_norm else y


def _combine(x2d, routed8, shared, gate, mod_of_tile, g_final, *, first_tile, n_tiles, final_norm):
    d = x2d.shape[1]
    tm = TOKEN_TILE
    tok = lambda width: pl.BlockSpec((tm, width), lambda i: (i + first_tile, 0))
    return pl.pallas_call(
        functools.partial(_combine_body, final_norm=final_norm),
        out_shape=jax.ShapeDtypeStruct((n_tiles * tm, d), F32),
        grid=(n_tiles,),
        in_specs=[tok(d), tok(TOP_K * d), tok(d),
                  pl.BlockSpec((1, tm, d), lambda i: (mod_of_tile(i + first_tile), 0, 0)),
                  pl.BlockSpec((1, d), lambda i: (0, 0))],
        out_specs=pl.BlockSpec((tm, d), lambda i: (i, 0)),
        compiler_params=_params(("arbitrary",)),
        name="moe_combine",
    )(x2d, routed8, shared, gate, g_final.reshape(1, d))


SELECT_PAGES = 8


def _sample_select_body(pt_ref, *refs, n_sel):
    pages = refs[:SELECT_PAGES]
    q2_ref, sel_ref, ks_ref = refs[SELECT_PAGES:]
    c = pl.program_id(1)
    ppb = MOBA_BLOCK // PAGE_SIZE
    nh = q2_ref.shape[1]
    for n in range(SELECT_PAGES // ppb):
        tot = pages[n * ppb][0]
        for i in range(1, ppb):
            tot = tot + pages[n * ppb + i][0]
        ks_ref[:, pl.ds(c * (SELECT_PAGES // ppb) + n, 1), :] = jnp.sum(tot, axis=1, keepdims=True)

    @pl.when(c == pl.num_programs(1) - 1)
    def _():
        nblk = ks_ref.shape[1]
        for h in range(nh):
            gate = lax.dot_general(ks_ref[h], q2_ref[0, h], (((1,), (1,)), ((), ())),
                                   preferred_element_type=F32, precision=lax.Precision.HIGHEST)
            row = lax.broadcasted_iota(I32, gate.shape, 0)
            g = gate
            picks = []
            for _ in range(n_sel):
                m = jnp.max(g, axis=0, keepdims=True)
                first = jnp.min(jnp.where(g == m, row, nblk), axis=0, keepdims=True)
                picks.append(first)
                g = jnp.where(row == first, -jnp.inf, g)
            picks.append(jnp.zeros((8 - n_sel, gate.shape[1]), I32))
            sel_ref[0, h] = jnp.concatenate(picks, axis=0)


def _sample_select(page_table, cache_k4, q2, n_sel):
    bd, n_pages = page_table.shape
    _, nh, half, dh2 = cache_k4.shape
    tpad = q2.shape[2]
    nblk = n_pages * PAGE_SIZE // MOBA_BLOCK
    page_spec = lambda i: pl.BlockSpec((1, nh, half, dh2),
                                       lambda b, c, pt: (pt[b, c * SELECT_PAGES + i], 0, 0, 0))
    return pl.pallas_call(
        functools.partial(_sample_select_body, n_sel=n_sel),
        out_shape=jax.ShapeDtypeStruct((bd, nh, 8, tpad), I32),
        grid_spec=pltpu.PrefetchScalarGridSpec(
            num_scalar_prefetch=1,
            grid=(bd, n_pages // SELECT_PAGES),
            in_specs=[page_spec(i) for i in range(SELECT_PAGES)]
                     + [pl.BlockSpec((1, nh, tpad, dh2), lambda b, c, pt: (b, 0, 0, 0))],
            out_specs=pl.BlockSpec((1, nh, 8, tpad), lambda b, c, pt: (b, 0, 0, 0)),
            scratch_shapes=[pltpu.VMEM((nh, nblk, dh2), F32)]),
        compiler_params=_params(("arbitrary", "arbitrary")),
        name="moba_sample_select",
    )(page_table, *([cache_k4] * SELECT_PAGES), q2)


def _sample_attend_body(phys_ref, selb_ref, slope_ref, *refs, n_tok, n_sel, past):
    ppb = MOBA_BLOCK // PAGE_SIZE
    per_tok = n_sel * ppb
    nkv = n_tok * per_tok
    k_refs = refs[:nkv]
    v_refs = refs[nkv:2 * nkv]
    q2_ref, kn_ref, vn_ref, o_ref = refs[2 * nkv:]
    b = pl.program_id(0)
    h = pl.program_id(1)
    nh = pl.num_programs(1)
    slope = slope_ref[h]
    scale = HEAD_DIM ** -0.5
    q2 = q2_ref[0, 0]
    tpad = q2.shape[0]
    half = PAGE_SIZE // 2
    ncol = per_tok * half
    q1 = (q2[:, :HEAD_DIM] * scale).astype(BF16)
    s_new = _dot_nt(q1, kn_ref[0, 0].astype(BF16))
    ti = lax.broadcasted_iota(I32, s_new.shape, 0)
    oi = lax.broadcasted_iota(I32, s_new.shape, 1)
    s_new = jnp.where(jnp.logical_and(oi <= ti, oi < n_tok), s_new - slope * (ti - oi).astype(F32), MASKED)
    vn = vn_ref[0, 0].astype(BF16)
    rr = lax.broadcasted_iota(I32, (8, 2 * HEAD_DIM), 0)
    ll = lax.broadcasted_iota(I32, (8, 2 * HEAD_DIM), 1)
    parity_sel = jnp.logical_or(jnp.logical_and(rr == 0, ll < HEAD_DIM), jnp.logical_and(rr == 1, ll >= HEAD_DIM))
    col = lax.broadcasted_iota(I32, (8, ncol), 1)
    par = lax.broadcasted_iota(I32, (8, ncol), 0)
    outs = []
    for t in range(n_tok):
        kv = jnp.concatenate([k_refs[t * per_tok + i][0, 0] for i in range(per_tok)], axis=0).astype(BF16)
        vv = jnp.concatenate([v_refs[t * per_tok + i][0, 0] for i in range(per_tok)], axis=0).astype(BF16)
        qn = jnp.where(parity_sel, jnp.broadcast_to(q2[t:t + 1] * scale, (8, 2 * HEAD_DIM)), 0.0).astype(BF16)
        s = _dot_nt(qn, kv)
        blk_of_col = jnp.zeros((8, ncol), I32)
        for r in range(n_sel):
            sb = selb_ref[((b * nh + h) * n_tok + t) * n_sel + r]
            blk_of_col = jnp.where(col // (ppb * half) == r, sb, blk_of_col)
        spos = blk_of_col * MOBA_BLOCK + 2 * (col % (ppb * half)) + par
        s = jnp.where(par < 2, s - slope * (past + t - spos).astype(F32), MASKED)
        so = s_new[t:t + 1]
        m = jnp.maximum(jnp.max(jnp.max(s, axis=1, keepdims=True), axis=0, keepdims=True),
                        jnp.max(so, axis=1, keepdims=True))
        p = jnp.exp(s - m)
        po = jnp.exp(so - m)
        l = jnp.sum(jnp.sum(p, axis=1, keepdims=True), axis=0, keepdims=True) + jnp.sum(po, axis=1, keepdims=True)
        res = _dot(p.astype(BF16), vv)
        o_sel = res[0:1, :HEAD_DIM] + res[1:2, HEAD_DIM:]
        o_new = _dot(jnp.broadcast_to(po, (8, tpad)).astype(BF16), vn)[0:1]
        outs.append((o_sel + o_new) / l)
    outs.append(jnp.zeros((tpad - n_tok, HEAD_DIM), F32))
    o_ref[0, 0] = jnp.concatenate(outs, axis=0)


def _sample_attend(phys, selb, slopes, cache_k4, cache_v4, q2, k_new, v_new, *, n_tok, n_sel, past):
    bd, nh, tpad, dh2 = q2.shape
    half = cache_k4.shape[2]
    ppb = MOBA_BLOCK // PAGE_SIZE
    nkv = n_tok * n_sel * ppb
    kv_spec = lambda i: pl.BlockSpec((1, 1, half, dh2),
                                     lambda b, h, ph, sb, sl: (ph[(b * nh + h) * nkv + i], h, 0, 0))
    tok_spec = lambda w: pl.BlockSpec((1, 1, tpad, w), lambda b, h, ph, sb, sl: (b, h, 0, 0))
    return pl.pallas_call(
        functools.partial(_sample_attend_body, n_tok=n_tok, n_sel=n_sel, past=past),
        out_shape=jax.ShapeDtypeStruct((bd, nh, tpad, HEAD_DIM), F32),
        grid_spec=pltpu.PrefetchScalarGridSpec(
            num_scalar_prefetch=3,
            grid=(bd, nh),
            in_specs=[kv_spec(i) for i in range(nkv)] * 2 + [tok_spec(dh2), tok_spec(HEAD_DIM), tok_spec(HEAD_DIM)],
            out_specs=tok_spec(HEAD_DIM)),
        compiler_params=_params(("arbitrary", "arbitrary")),
        name="moba_sample_attend",
    )(phys, selb, slopes, *([cache_k4] * nkv), *([cache_v4] * nkv), q2, k_new, v_new)


def _moba_sample(q, k_new, v_new, cache_k, cache_v, page_table, slopes):
    bd, nh, n_tok, dh = q.shape
    n_pages = page_table.shape[1]
    past = n_pages * PAGE_SIZE
    ppb = MOBA_BLOCK // PAGE_SIZE
    assert past % MOBA_BLOCK == 0 and n_pages % SELECT_PAGES == 0 and n_tok <= 8
    n_sel = min(MOBA_TOPK, past // MOBA_BLOCK)
    assert n_sel > 0
    tpad = 8
    padt = lambda u: jnp.pad(u, ((0, 0), (0, 0), (0, tpad - n_tok), (0, 0)))
    q2 = padt(jnp.concatenate([q, q], axis=-1))
    view = lambda cch: cch.reshape(cch.shape[0], nh, PAGE_SIZE // 2, 2 * dh)
    ck4, cv4 = view(cache_k), view(cache_v)
    sel = _sample_select(page_table, ck4, q2, n_sel)[:, :, :n_sel, :n_tok]
    sel = jnp.transpose(sel, (0, 1, 3, 2))
    page_idx = sel[..., None] * ppb + jnp.arange(ppb, dtype=I32)
    phys = jnp.take_along_axis(page_table, page_idx.reshape(bd, -1), axis=1)
    out = _sample_attend(phys.reshape(-1), sel.reshape(-1), slopes, ck4, cv4, q2, padt(k_new), padt(v_new),
                         n_tok=n_tok, n_sel=n_sel, past=past)
    return out[:, :, :n_tok]


def _alibi_slopes(n_heads):
    return jnp.exp2(-8.0 * (jnp.arange(n_heads, dtype=F32) + 1.0) / n_heads)


def kernel(x_prompt, x_sample, cache_k, cache_v, page_table, state_conv, state_ssm, c_prompt, c_sample,
           w_ada, b_ada, g_mix, w_in, conv_w, conv_b, dt_bias, a_log, d_skip, g_ssd, g_att, w_out,
           g_ffn, w_router, router_bias, w_gate, w_up, w_down, ws_gate, ws_up, ws_down, g_final):
    depth = w_ada.shape[0]
    b, s, d = x_prompt.shape
    bd, ts, _ = x_sample.shape
    tm = TOKEN_TILE
    tp, tsn = b * s, bd * ts
    assert s % tm == 0 and tsn % tm == 0 and tm == MOBA_BLOCK
    n_ssd_heads = dt_bias.shape[1]
    ssd_w = n_ssd_heads * SSD_HEAD_DIM
    att_w = d - ssd_w
    nh = att_w // HEAD_DIM
    conv_dim = conv_w.shape[2]
    n_state = state_ssm.shape[-1]
    n_experts = w_router.shape[2]
    qkvz = 3 * att_w + ssd_w + conv_dim
    slopes = _alibi_slopes(nh)
    prompt_tiles, sample_tiles = tp // tm, tsn // tm
    tps = s // tm
    mod_of_tile = lambda i: jnp.where(i < prompt_tiles, i // tps, b + i - prompt_tiles)

    xp = x_prompt.reshape(tp, d)
    xs = x_sample.reshape(tsn, d)
    kp_l, vp_l, ks_l, vs_l, cp_l, cs_l, hp_l, hs_l = [], [], [], [], [], [], [], []
    for l in range(depth):
        c_all = jnp.concatenate([c_prompt, c_sample], axis=0)
        n_c = c_all.shape[0]
        c_all = jnp.pad(c_all, ((0, -n_c % 8), (0, 0)))
        mods = jnp.split(_adaln(c_all, w_ada[l], b_ada[l])[:n_c], 6, axis=-1)
        mod_p = [u[:b, None, :] for u in mods]
        mod_s = [jnp.repeat(u[b:], ts, axis=0).reshape(sample_tiles, tm, d) for u in mods]
        mod_u = [jnp.concatenate([jnp.broadcast_to(p_, (b, tm, d)), s_], axis=0) for p_, s_ in zip(mod_p, mod_s)]

        dt_cols = w_in[l][:, qkvz:]
        w_all = jnp.concatenate([w_in[l][:, :qkvz], jnp.repeat(dt_cols, SSD_HEAD_DIM, axis=1),
                                 jnp.pad(dt_cols, ((0, 0), (0, 128 - n_ssd_heads)))], axis=1).astype(BF16)
        w_out_b = w_out[l].astype(BF16)
        ssd_par = (conv_w[l], conv_b[l], dt_bias[l], a_log[l], d_skip[l], g_ssd[l])
        dims = dict(att_w=att_w, ssd_w=ssd_w, conv_dim=conv_dim)

        qt, kaug, vt, kpg, vpg, km, z_p, xbc_p, dte_p, dtc_p = _inproj(
            xp, mod_p[0], mod_p[1], g_mix[l], w_all, prompt_batch=(b, s), **dims)
        kmean = km.reshape(b, s // MOBA_BLOCK, nh, HEAD_DIM).transpose(0, 2, 1, 3)
        att_p = _attn_prompt(slopes, qt, kaug, vt, kmean)
        xbc_p3 = xbc_p.reshape(b, s, conv_dim)
        y_p, h_p = _ssd(xbc_p3, z_p.reshape(b, s, ssd_w), dte_p.reshape(b, s, ssd_w), dtc_p.reshape(b, s, 128),
                        jnp.zeros((b, 8, conv_dim), F32), jnp.zeros((b, ssd_w, n_state), F32), *ssd_par,
                        length=MOBA_BLOCK)
        x1_p = _outproj(att_p, y_p.reshape(tp, ssd_w), xp, mod_p[2], g_att[l], w_out_b, prompt_batch=(b, s))
        kp_l.append(kpg)
        vp_l.append(vpg)
        cp_l.append(xbc_p3[:, s - (CONV_WIDTH - 1):])
        hp_l.append(h_p.reshape(b, n_ssd_heads, SSD_HEAD_DIM, n_state))

        q_s, k_s, v_s, z_s, xbc_s, dte_s, dtc_s = _inproj(xs, mod_s[0], mod_s[1], g_mix[l], w_all, **dims)
        heads = lambda u: u.reshape(bd, ts, nh, HEAD_DIM).transpose(0, 2, 1, 3)
        qh, kh, vh = heads(q_s), heads(k_s), heads(v_s)
        att_s = _moba_sample(qh, kh, vh, cache_k[l], cache_v[l], page_table, slopes)
        att_s = att_s.transpose(0, 2, 1, 3).reshape(tsn, att_w)
        xbc_s3 = xbc_s.reshape(bd, ts, conv_dim)
        conv_prev = jnp.pad(state_conv[l], ((0, 0), (8 - (CONV_WIDTH - 1), 0), (0, 0)))
        y_s, h_s = _ssd(xbc_s3, z_s.reshape(bd, ts, ssd_w), dte_s.reshape(bd, ts, ssd_w), dtc_s.reshape(bd, ts, 128),
                        conv_prev, state_ssm[l].reshape(bd, ssd_w, n_state), *ssd_par, length=128)
        x1_s = _outproj(att_s, y_s.reshape(tsn, ssd_w), xs, mod_s[2], g_att[l], w_out_b)
        ks_l.append(kh)
        vs_l.append(vh)
        cs_l.append(jnp.concatenate([state_conv[l], xbc_s3], axis=1)[:, ts:])
        hs_l.append(h_s.reshape(bd, n_ssd_heads, SSD_HEAD_DIM, n_state))

        x1 = jnp.concatenate([x1_p, x1_s], axis=0)
        t_all, shared, sel_t, wts_t = _moe_pre(x1, mod_u[3], mod_u[4], mod_of_tile, g_ffn[l], w_router[l].astype(BF16),
                                               router_bias[l], ws_gate[l].astype(BF16), ws_up[l].astype(BF16),
                                               ws_down[l].astype(BF16))
        n_all = tp + tsn
        sel = sel_t.transpose(0, 2, 1).reshape(n_all, TOP_K)
        wts = wts_t.transpose(0, 2, 1).reshape(n_all, TOP_K)
        plan = _routing_plan(sel, wts, n_experts, EXPERT_TILE)
        spare_tokens = EXPERT_TILE // TOP_K
        routed = _experts(*plan, t_all, w_gate[l], w_up[l], w_down[l], (n_all + spare_tokens) * TOP_K)
        routed8 = routed.reshape(n_all + spare_tokens, TOP_K * d)
        last = l == depth - 1
        fin = functools.partial(_combine, x1, routed8, shared, mod_u[5], mod_of_tile, g_final, final_norm=last)
        xp = fin(first_tile=0, n_tiles=prompt_tiles)
        xs = fin(first_tile=prompt_tiles, n_tiles=sample_tiles)

    return (xp.reshape(b, s, d), xs.reshape(bd, ts, d), jnp.stack(kp_l), jnp.stack(vp_l), jnp.stack(ks_l),
            jnp.stack(vs_l), jnp.stack(cp_l), jnp.stack(cs_l), jnp.stack(hp_l), jnp.stack(hs_l))
```

```python
import functools

import jax
import jax.numpy as jnp
from jax import lax
from jax.experimental import pallas as pl
from jax.experimental.pallas import tpu as pltpu

F32 = jnp.float32
BF16 = jnp.bfloat16
I32 = jnp.int32

NORM_EPS = 1e-6
HEAD_DIM = 64
MOBA_BLOCK = 256
MOBA_TOPK = 3
PAGE_SIZE = 128
SSD_HEAD_DIM = 64
SSD_GROUPS = 2
SSD_STATE = 128
CONV_WIDTH = 4
N_EXPERT_GROUPS = 8
TOPK_GROUPS = 4
TOP_K = 8
ROUTED_SCALE = 2.5
MASKED = -1e30
TOKEN_TILE = 256
EXPERT_TILE = 256
PAST_UNROLL = 4
VMEM_LIMIT = 56 * 1024 * 1024


def _params(sem, vmem=VMEM_LIMIT):
    return pltpu.CompilerParams(dimension_semantics=sem, vmem_limit_bytes=vmem)


def _silu(x):
    return x * jax.nn.sigmoid(x)


def _softplus(x):
    return jnp.maximum(x, 0.0) + jnp.log1p(jnp.exp(-jnp.abs(x)))


def _rms(x, g):
    return x * lax.rsqrt(jnp.mean(x * x, axis=-1, keepdims=True) + NORM_EPS) * g


def _dot(a, b):
    return jnp.dot(a, b, preferred_element_type=F32)


def _dot_nt(a, b):
    return lax.dot_general(a, b, (((1,), (1,)), ((), ())), preferred_element_type=F32)


def _dot_exact(a, b):
    return jnp.dot(a, b, preferred_element_type=F32, precision=lax.Precision.HIGHEST)


def _adaln_body(c_ref, w_ref, b_ref, o_ref):
    a = _silu(c_ref[...]).astype(BF16)
    o_ref[...] = _dot(a, w_ref[...].astype(BF16)) + b_ref[...]


def _adaln(c, w_ada, b_ada):
    r, d = c.shape
    n = w_ada.shape[1]
    tn = 512
    return pl.pallas_call(
        _adaln_body,
        out_shape=jax.ShapeDtypeStruct((r, n), F32),
        grid=(n // tn,),
        in_specs=[pl.BlockSpec((r, d), lambda j: (0, 0)),
                  pl.BlockSpec((d, tn), lambda j: (0, j)),
                  pl.BlockSpec((1, tn), lambda j: (0, j))],
        out_specs=pl.BlockSpec((r, tn), lambda j: (0, j)),
        compiler_params=_params(("arbitrary",)),
        name="adaln",
    )(c, w_ada, b_ada.reshape(1, n))


def _inproj_body(x_ref, sh_ref, sc_ref, g_ref, w_ref, *outs, att_w, ssd_w, conv_dim, prompt):
    x = x_ref[...]
    h = (_rms(x, g_ref[...]) * (1.0 + sc_ref[0]) + sh_ref[0]).astype(BF16)
    tm = x.shape[0]
    a = att_w
    c0 = 3 * a
    c1 = c0 + ssd_w
    c2 = c1 + conv_dim
    c3 = c2 + ssd_w

    def proj(lo, hi):
        return _dot(h, w_ref[:, lo:hi])

    q, k, v = proj(0, a), proj(a, 2 * a), proj(2 * a, c0)
    if prompt:
        (qt_ref, kaug_ref, vt_ref, kp_ref, vp_ref, km_ref, z_ref, xbc_ref, dte_ref, dtc_ref) = outs
        nh = a // HEAD_DIM
        qt_ref[0] = q.T.reshape(nh, HEAD_DIM, tm)
        vt_ref[0] = v.T.reshape(nh, HEAD_DIM, tm).astype(BF16)
        km_ref[0] = jnp.mean(k, axis=0, keepdims=True)
        lane = lax.broadcasted_iota(I32, (tm, 2 * HEAD_DIM), 1)
        row = lax.broadcasted_iota(I32, (tm, 2 * HEAD_DIM), 0)
        pos = jnp.where(lane == HEAD_DIM, (row % MOBA_BLOCK).astype(F32), 0.0)
        low = lane < HEAD_DIM
        for hp in range(nh // 2):
            kc = k[:, hp * 128:(hp + 1) * 128]
            vc = v[:, hp * 128:(hp + 1) * 128]
            kr = pltpu.roll(kc, HEAD_DIM, 1)
            vr = pltpu.roll(vc, HEAD_DIM, 1)
            kaug_ref[0, 2 * hp] = jnp.where(low, kc, pos).astype(BF16)
            kaug_ref[0, 2 * hp + 1] = jnp.where(low, kr, pos).astype(BF16)
            for pg in range(tm // PAGE_SIZE):
                rs = slice(pg * PAGE_SIZE, (pg + 1) * PAGE_SIZE)
                kp_ref[0, pg, 2 * hp] = kc[rs, :HEAD_DIM]
                kp_ref[0, pg, 2 * hp + 1] = kr[rs, :HEAD_DIM]
                vp_ref[0, pg, 2 * hp] = vc[rs, :HEAD_DIM]
                vp_ref[0, pg, 2 * hp + 1] = vr[rs, :HEAD_DIM]
    else:
        (q_ref, k_ref, v_ref, z_ref, xbc_ref, dte_ref, dtc_ref) = outs
        q_ref[...] = q
        k_ref[...] = k
        v_ref[...] = v
    z_ref[...] = proj(c0, c1)
    xbc_ref[...] = proj(c1, c2)
    dte_ref[...] = proj(c2, c3)
    dtc_ref[...] = proj(c3, c3 + 128)


def _inproj(x2d, shift, scale, g, w_all, *, att_w, ssd_w, conv_dim, prompt_batch=None):
    t, d = x2d.shape
    tm = TOKEN_TILE
    nw = w_all.shape[1]
    nt = t // tm
    prompt = prompt_batch is not None
    tok = lambda width: pl.BlockSpec((tm, width), lambda i: (i, 0))
    common_shapes = [jax.ShapeDtypeStruct((t, ssd_w), F32), jax.ShapeDtypeStruct((t, conv_dim), F32),
                     jax.ShapeDtypeStruct((t, ssd_w), F32), jax.ShapeDtypeStruct((t, 128), F32)]
    common_specs = [tok(ssd_w), tok(conv_dim), tok(ssd_w), tok(128)]
    if prompt:
        b, s = prompt_batch
        tps = s // tm
        nh = att_w // HEAD_DIM
        ppt = tm // PAGE_SIZE
        mod_map = lambda i: (i // tps, 0, 0)
        out_shape = [jax.ShapeDtypeStruct((b, nh, HEAD_DIM, s), F32),
                     jax.ShapeDtypeStruct((b, nh, s, 2 * HEAD_DIM), BF16),
                     jax.ShapeDtypeStruct((b, nh, HEAD_DIM, s), BF16),
                     jax.ShapeDtypeStruct((b, s // PAGE_SIZE, nh, PAGE_SIZE, HEAD_DIM), F32),
                     jax.ShapeDtypeStruct((b, s // PAGE_SIZE, nh, PAGE_SIZE, HEAD_DIM), F32),
                     jax.ShapeDtypeStruct((nt, 1, att_w), F32)] + common_shapes
        out_specs = [pl.BlockSpec((1, nh, HEAD_DIM, tm), lambda i: (i // tps, 0, 0, i % tps)),
                     pl.BlockSpec((1, nh, tm, 2 * HEAD_DIM), lambda i: (i // tps, 0, i % tps, 0)),
                     pl.BlockSpec((1, nh, HEAD_DIM, tm), lambda i: (i // tps, 0, 0, i % tps)),
                     pl.BlockSpec((1, ppt, nh, PAGE_SIZE, HEAD_DIM), lambda i: (i // tps, i % tps, 0, 0, 0)),
                     pl.BlockSpec((1, ppt, nh, PAGE_SIZE, HEAD_DIM), lambda i: (i // tps, i % tps, 0, 0, 0)),
                     pl.BlockSpec((1, 1, att_w), lambda i: (i, 0, 0))] + common_specs
        mod_block = (1, 1, d)
    else:
        mod_map = lambda i: (i, 0, 0)
        out_shape = [jax.ShapeDtypeStruct((t, att_w), F32)] * 3 + common_shapes
        out_specs = [tok(att_w)] * 3 + common_specs
        mod_block = (1, tm, d)
    body = functools.partial(_inproj_body, att_w=att_w, ssd_w=ssd_w, conv_dim=conv_dim, prompt=prompt)
    return pl.pallas_call(
        body,
        out_shape=out_shape,
        grid=(nt,),
        in_specs=[tok(d),
                  pl.BlockSpec(mod_block, mod_map),
                  pl.BlockSpec(mod_block, mod_map),
                  pl.BlockSpec((1, d), lambda i: (0, 0)),
                  pl.BlockSpec((d, nw), lambda i: (0, 0))],
        out_specs=out_specs,
        compiler_params=_params(("arbitrary",)),
        name="inproj_prompt" if prompt else "inproj_sample",
    )(x2d, shift, scale, g.reshape(1, d), w_all)


def _select_topk_rows(gate, n_valid, k):
    nrow = gate.shape[0]
    row = lax.broadcasted_iota(I32, gate.shape, 0)
    g = jnp.where(row < n_valid, gate, -jnp.inf)
    chosen = jnp.zeros(gate.shape, I32)
    for _ in range(k):
        m = jnp.max(g, axis=0, keepdims=True)
        cand = jnp.logical_and(g == m, g > -jnp.inf)
        first = jnp.min(jnp.where(cand, row, nrow), axis=0, keepdims=True)
        pick = row == first
        chosen = jnp.where(pick, 1, chosen)
        g = jnp.where(pick, -jnp.inf, g)
    return chosen


def _attn_prompt_body(slope_ref, qt_ref, k_ref, vt_ref, km_ref, o_ref, bias_ref):
    h = pl.program_id(1)
    t = pl.program_id(2)
    slope = slope_ref[h]
    blk = MOBA_BLOCK
    qt = qt_ref[0, 0]
    gate = _dot_exact(km_ref[0, 0], qt)
    chosen = _select_topk_rows(gate, t, MOBA_TOPK)
    jrow = lax.broadcasted_iota(I32, gate.shape, 0)
    iq = lax.broadcasted_iota(I32, gate.shape, 1)
    dist0 = ((t - jrow) * blk + iq).astype(F32)
    bias_ref[...] = jnp.where(chosen > 0, -slope * dist0, MASKED)

    r2 = lax.broadcasted_iota(I32, qt.shape, 0)
    extra = jnp.where(r2 == 0, slope, 0.0)
    qaug = jnp.concatenate([qt * (HEAD_DIM ** -0.5), extra], axis=0).astype(BF16)

    start = pl.multiple_of(t * blk, blk)
    s = _dot(k_ref[0, 0, pl.ds(start, blk), :], qaug)
    ik = lax.broadcasted_iota(I32, s.shape, 0)
    iqq = lax.broadcasted_iota(I32, s.shape, 1)
    s = jnp.where(ik <= iqq, s - slope * iqq.astype(F32), MASKED)
    m = jnp.max(s, axis=0, keepdims=True)
    p = jnp.exp(s - m)
    l = jnp.sum(p, axis=0, keepdims=True)
    acc = _dot(vt_ref[0, 0, :, pl.ds(start, blk)], p.astype(BF16))

    def body(jj, carry):
        m, l, acc = carry
        starts = [pl.multiple_of((jj * PAST_UNROLL + u) * blk, blk) for u in range(PAST_UNROLL)]
        ss = [_dot(k_ref[0, 0, pl.ds(starts[u], blk), :], qaug) + bias_ref[pl.ds(jj * PAST_UNROLL + u, 1), :]
              for u in range(PAST_UNROLL)]
        m_new = m
        for s_u in ss:
            m_new = jnp.maximum(m_new, jnp.max(s_u, axis=0, keepdims=True))
        alpha = jnp.exp(m - m_new)
        l = alpha * l
        acc = alpha * acc
        for u in range(PAST_UNROLL):
            p = jnp.exp(ss[u] - m_new)
            l = l + jnp.sum(p, axis=0, keepdims=True)
            acc = acc + _dot(vt_ref[0, 0, :, pl.ds(starts[u], blk)], p.astype(BF16))
        return m_new, l, acc

    m, l, acc = lax.fori_loop(0, (t + PAST_UNROLL - 1) // PAST_UNROLL, body, (m, l, acc))
    o_ref[0, 0] = acc / l


def _attn_prompt(slopes, qt, kaug, vt, kmean):
    b, nh, dh, s = qt.shape
    blk = MOBA_BLOCK
    nb = s // blk
    assert nb % PAST_UNROLL == 0
    return pl.pallas_call(
        _attn_prompt_body,
        out_shape=jax.ShapeDtypeStruct((b, nh, dh, s), F32),
        grid_spec=pltpu.PrefetchScalarGridSpec(
            num_scalar_prefetch=1,
            grid=(b, nh, nb),
            in_specs=[pl.BlockSpec((1, 1, dh, blk), lambda bi, hi, ti, sl: (bi, hi, 0, ti)),
                      pl.BlockSpec((1, 1, s, 2 * dh), lambda bi, hi, ti, sl: (bi, hi, 0, 0)),
                      pl.BlockSpec((1, 1, dh, s), lambda bi, hi, ti, sl: (bi, hi, 0, 0)),
                      pl.BlockSpec((1, 1, nb, dh), lambda bi, hi, ti, sl: (bi, hi, 0, 0))],
            out_specs=pl.BlockSpec((1, 1, dh, blk), lambda bi, hi, ti, sl: (bi, hi, 0, ti)),
            scratch_shapes=[pltpu.VMEM((nb, blk), F32)]),
        compiler_params=_params(("arbitrary", "arbitrary", "arbitrary")),
        name="moba_prompt",
    )(slopes, qt, kaug, vt, kmean)


def _ssd_body(xbc_ref, z_ref, dte_ref, dtc_ref, cprev_ref, h0_ref, cw_ref, cb_ref, dtbe_ref, dtbc_ref,
              ale_ref, alc_ref, dsk_ref, g_ref, y_ref, hout_ref, buf_ref, st_ref, *, rows, length):
    c = pl.program_id(1)
    ng = SSD_GROUPS
    n = SSD_STATE
    width = z_ref.shape[-1]
    gw = width // ng

    @pl.when(c == 0)
    def _():
        if rows < length:
            buf_ref[...] = jnp.zeros(buf_ref.shape, F32)
        buf_ref[0:8, :] = cprev_ref[0]
        for g in range(ng):
            st_ref[g] = h0_ref[0, g * gw:(g + 1) * gw, :].T

    buf_ref[8:8 + rows, :] = xbc_ref[0]
    acc = cb_ref[...]
    for i in range(CONV_WIDTH):
        off = 8 - (CONV_WIDTH - 1) + i
        acc = acc + buf_ref[off:off + length, :] * cw_ref[i:i + 1, :]
    if rows == length:
        buf_ref[0:8, :] = buf_ref[length:length + 8, :]
    xc = _silu(acc)
    xs = xc[:, :width]
    bm = xc[:, width:width + ng * n]
    cm = xc[:, width + ng * n:]

    if rows < length:
        dte_full = jnp.concatenate([dte_ref[0], jnp.zeros((length - rows, width), F32)], axis=0)
        dtc_full = jnp.concatenate([dtc_ref[0], jnp.zeros((length - rows, 128), F32)], axis=0)
        live_e = lax.broadcasted_iota(I32, (length, width), 0) < rows
        live_c = lax.broadcasted_iota(I32, (length, 128), 0) < rows
        dt = jnp.where(live_e, _softplus(dte_full + dtbe_ref[...]), 0.0)
        dtc = jnp.where(live_c, _softplus(dtc_full + dtbc_ref[...]), 0.0)
    else:
        dt = _softplus(dte_ref[0] + dtbe_ref[...])
        dtc = _softplus(dtc_ref[0] + dtbc_ref[...])
    da = dt * (-jnp.exp(ale_ref[...]))
    dac = dtc * (-jnp.exp(alc_ref[...]))
    xdt = xs * dt

    ti = lax.broadcasted_iota(I32, (length, length), 0)
    si = lax.broadcasted_iota(I32, (length, length), 1)
    causal = ti >= si
    tri = jnp.where(causal, 1.0, 0.0)
    acs = _dot_exact(tri, da)
    acs_t = _dot_exact(tri, dac).T
    acs_last = acs[length - 1:length, :]
    e_acs = jnp.exp(acs)
    w_tail = jnp.exp(acs_last - acs)
    lane = lax.broadcasted_iota(I32, (length, 128), 1)

    ys = []
    for g in range(ng):
        bg = bm[:, g * n:(g + 1) * n]
        cg = cm[:, g * n:(g + 1) * n].astype(BF16)
        cb = _dot_nt(cg, bg.astype(BF16))
        st = st_ref[g]
        yg = _dot(cg, st.astype(BF16)) * e_acs[:, g * gw:(g + 1) * gw]
        pieces = []
        for pr in range(gw // 128):
            x_pair = xdt[:, g * gw + pr * 128: g * gw + (pr + 1) * 128].astype(BF16)
            halves = []
            for hh in range(2):
                head = (g * gw + pr * 128) // SSD_HEAD_DIM + hh
                col = acs[:, head * SSD_HEAD_DIM: head * SSD_HEAD_DIM + 1]
                rowv = acs_t[head:head + 1, :]
                decay = jnp.where(causal, jnp.exp(col - rowv), 0.0)
                halves.append(_dot((cb * decay).astype(BF16), x_pair))
            pieces.append(jnp.where(lane < SSD_HEAD_DIM, halves[0], halves[1]))
        ys.append(yg + jnp.concatenate(pieces, axis=1))
        xw = (xdt[:, g * gw:(g + 1) * gw] * w_tail[:, g * gw:(g + 1) * gw]).astype(BF16)
        st_ref[g] = st * jnp.exp(acs_last[:, g * gw:(g + 1) * gw]) + _dot(bg.T.astype(BF16), xw)
    y = jnp.concatenate(ys, axis=1)
    y = y + dsk_ref[...] * xs
    y = y[:rows] * _silu(z_ref[0])
    outs = []
    for g in range(ng):
        outs.append(_rms(y[:, g * gw:(g + 1) * gw], g_ref[:, g * gw:(g + 1) * gw]))
    y_ref[0] = jnp.concatenate(outs, axis=1)

    @pl.when(c == pl.num_programs(1) - 1)
    def _():
        for g in range(ng):
            hout_ref[0, g * gw:(g + 1) * gw, :] = st_ref[g].T


def _ssd(xbc, z, dte, dtc, conv_prev, h0, conv_w, conv_b, dt_bias, a_log, d_skip, g_ssd, *, length):
    bn, lt, cdim = xbc.shape
    width = z.shape[-1]
    n = h0.shape[-1]
    rows = min(lt, length)
    nc = lt // rows
    rep = width // dt_bias.shape[0]
    expand = lambda u: jnp.repeat(u, rep).reshape(1, width)
    lane_pad = lambda u: jnp.pad(u, (0, 128 - u.shape[0])).reshape(1, 128)
    seq = lambda w_: pl.BlockSpec((1, rows, w_), lambda b, c: (b, c, 0))
    const = lambda shape: pl.BlockSpec(shape, lambda b, c: (0,) * len(shape))
    body = functools.partial(_ssd_body, rows=rows, length=length)
    return pl.pallas_call(
        body,
        out_shape=[jax.ShapeDtypeStruct((bn, lt, width), F32), jax.ShapeDtypeStruct((bn, width, n), F32)],
        grid=(bn, nc),
        in_specs=[seq(cdim), seq(width), seq(width), seq(128),
                  pl.BlockSpec((1, 8, cdim), lambda b, c: (b, 0, 0)),
                  pl.BlockSpec((1, width, n), lambda b, c: (b, 0, 0)),
                  const((CONV_WIDTH, cdim)), const((1, cdim)), const((1, width)), const((1, 128)),
                  const((1, width)), const((1, 128)), const((1, width)), const((1, width))],
        out_specs=[seq(width), pl.BlockSpec((1, width, n), lambda b, c: (b, 0, 0))],
        scratch_shapes=[pltpu.VMEM((length + 8, cdim), F32),
                        pltpu.VMEM((SSD_GROUPS, n, width // SSD_GROUPS), F32)],
        compiler_params=_params(("arbitrary", "arbitrary")),
        name="ssd_scan",
    )(xbc, z, dte, dtc, conv_prev, h0, conv_w, conv_b.reshape(1, cdim), expand(dt_bias), lane_pad(dt_bias),
      expand(a_log), lane_pad(a_log), expand(d_skip), g_ssd.reshape(1, width))


def _outproj_body(att_ref, ssd_ref, x_ref, gt_ref, ga_ref, w_ref, *rest, transposed):
    o_ref = rest[-1]
    if transposed:
        a = att_ref[0].reshape(-1, att_ref.shape[-1]).T
    else:
        a = att_ref[...]
    mix = jnp.concatenate([_rms(a, ga_ref[...]), ssd_ref[...]], axis=-1).astype(BF16)
    o_ref[...] = x_ref[...] + gt_ref[0] * _dot(mix, w_ref[...])


def _outproj(att, ssd, x2d, gate, g_att, w_out, unified, *, first_tile, prompt_batch=None):
    t, d = x2d.shape
    tm = TOKEN_TILE
    tok = lambda width: pl.BlockSpec((tm, width), lambda i: (i, 0))
    sw = ssd.shape[-1]
    if prompt_batch is not None:
        b, s = prompt_batch
        tps = s // tm
        nh, dh = att.shape[1], att.shape[2]
        aw = nh * dh
        att_spec = pl.BlockSpec((1, nh, dh, tm), lambda i: (i // tps, 0, 0, i % tps))
        gate_spec = pl.BlockSpec((1, 1, d), lambda i: (i // tps, 0, 0))
    else:
        aw = att.shape[-1]
        att_spec = tok(aw)
        gate_spec = pl.BlockSpec((1, tm, d), lambda i: (i, 0, 0))
    return pl.pallas_call(
        functools.partial(_outproj_body, transposed=prompt_batch is not None),
        out_shape=jax.ShapeDtypeStruct(unified.shape, F32),
        grid=(t // tm,),
        in_specs=[att_spec, tok(sw), tok(d), gate_spec,
                  pl.BlockSpec((1, aw), lambda i: (0, 0)),
                  pl.BlockSpec((aw + sw, d), lambda i: (0, 0)),
                  pl.BlockSpec(memory_space=pl.ANY)],
        out_specs=pl.BlockSpec((tm, d), lambda i: (i + first_tile, 0)),
        input_output_aliases={6: 0},
        compiler_params=_params(("arbitrary",)),
        name="outproj",
    )(att, ssd, x2d, gate, g_att.reshape(1, aw), w_out, unified)


def _route_t(st, sbt):
    e, tm = st.shape
    per = e // N_EXPERT_GROUPS
    row = lax.broadcasted_iota(I32, (e, tm), 0)
    rg = lax.broadcasted_iota(I32, (per, tm), 0)
    gs = []
    for g in range(N_EXPERT_GROUPS):
        blk = sbt[g * per:(g + 1) * per]
        m1 = jnp.max(blk, axis=0, keepdims=True)
        first = jnp.min(jnp.where(blk == m1, rg, per), axis=0, keepdims=True)
        m2 = jnp.max(jnp.where(rg == first, -jnp.inf, blk), axis=0, keepdims=True)
        gs.append(m1 + m2)
    gscore = jnp.concatenate(gs, axis=0)
    gch = _select_topk_rows(gscore, N_EXPERT_GROUPS, TOPK_GROUPS)
    emask = jnp.concatenate([jnp.broadcast_to(gch[g:g + 1], (per, tm)) for g in range(N_EXPERT_GROUPS)], axis=0)
    cur = jnp.where(emask > 0, sbt, -jnp.inf)
    sel, wts = [], []
    for _ in range(TOP_K):
        m = jnp.max(cur, axis=0, keepdims=True)
        first = jnp.min(jnp.where(cur == m, row, e), axis=0, keepdims=True)
        pick = row == first
        sel.append(first)
        wts.append(jnp.sum(jnp.where(pick, st, 0.0), axis=0, keepdims=True))
        cur = jnp.where(pick, -jnp.inf, cur)
    sel = jnp.concatenate(sel, axis=0)
    w = jnp.concatenate(wts, axis=0)
    return sel, w / jnp.sum(w, axis=0, keepdims=True) * ROUTED_SCALE


def _moe_pre_body(x_ref, sh_ref, sc_ref, g_ref, wr_ref, rb_ref, wsg_ref, wsu_ref, wsd_ref,
                  t_ref, shared_ref, sel_ref, wt_ref):
    t = _rms(x_ref[...], g_ref[...]) * (1.0 + sc_ref[0]) + sh_ref[0]
    t_ref[...] = t
    tb = t.astype(BF16)
    st = jax.nn.sigmoid(_dot(tb, wr_ref[...])).T
    sel, w = _route_t(st, st + rb_ref[...])
    sel_ref[0] = sel
    wt_ref[0] = w
    hs = _silu(_dot(tb, wsg_ref[...])) * _dot(tb, wsu_ref[...])
    shared_ref[...] = _dot(hs.astype(BF16), wsd_ref[...])


def _moe_pre(x2d, shift, scale, mod_of_tile, g, w_router, router_bias, ws_gate, ws_up, ws_down):
    t, d = x2d.shape
    tm = TOKEN_TILE
    nt = t // tm
    e = w_router.shape[1]
    sd = ws_gate.shape[1]
    tok = lambda width: pl.BlockSpec((tm, width), lambda i: (i, 0))
    mod = pl.BlockSpec((1, tm, d), lambda i: (mod_of_tile(i), 0, 0))
    full = lambda shape: pl.BlockSpec(shape, lambda i: (0,) * len(shape))
    return pl.pallas_call(
        _moe_pre_body,
        out_shape=[jax.ShapeDtypeStruct((t, d), F32), jax.ShapeDtypeStruct((t, d), F32),
                   jax.ShapeDtypeStruct((nt, TOP_K, tm), I32), jax.ShapeDtypeStruct((nt, TOP_K, tm), F32)],
        grid=(nt,),
        in_specs=[tok(d), mod, mod, full((1, d)), full((d, e)), full((e, 1)),
                  full((d, sd)), full((d, sd)), full((sd, d))],
        out_specs=[tok(d), tok(d),
                   pl.BlockSpec((1, TOP_K, tm), lambda i: (i, 0, 0)),
                   pl.BlockSpec((1, TOP_K, tm), lambda i: (i, 0, 0))],
        compiler_params=_params(("arbitrary",)),
        name="moe_pre",
    )(x2d, shift, scale, g.reshape(1, d), w_router, router_bias.reshape(e, 1), ws_gate, ws_up, ws_down)


def _experts_body(tile_ref, exp_ref, lo_ref, hi_ref, n_ref, tok_ref, dst_ref, roww_ref, t_hbm, wg_ref, wu_ref, wd_ref,
                  out_hbm, xbuf, ybuf, wgb, wub, wdb, gsem, ssem):
    s = pl.program_id(0)
    n = n_ref[0]
    groups = xbuf.shape[1]
    rows = groups * 8

    def all_rows_done(buf, sem, slot):
        pltpu.make_async_copy(buf.at[slot], buf.at[slot], sem.at[slot]).wait()

    opens_tile = jnp.logical_or(s == 0, tile_ref[s] != tile_ref[jnp.maximum(s - 1, 0)])

    @pl.when(jnp.logical_and(s < n, opens_tile))
    def _():
        slot = tile_ref[s] % 2

        def issue(g, _):
            for u in range(8):
                pltpu.make_async_copy(t_hbm.at[pl.ds(tok_ref[0, 0, g * 8 + u], 1)],
                                      xbuf.at[slot, g, pl.ds(u, 1)], gsem.at[slot]).start()
            return 0
        lax.fori_loop(0, groups, issue, 0)

    j = s - 1

    @pl.when(jnp.logical_and(j >= 0, j < n))
    def _():
        jc = jnp.maximum(j, 0)
        tl = tile_ref[jc]
        slot = tl % 2
        first = jnp.logical_or(jc == 0, tile_ref[jnp.maximum(jc - 1, 0)] != tl)
        last = jnp.logical_or(jc == n - 1, tile_ref[jc + 1] != tl)

        @pl.when(first)
        def _():
            all_rows_done(xbuf, gsem, slot)

            @pl.when(tl >= 2)
            def _():
                all_rows_done(ybuf, ssem, slot)

        changed = jnp.logical_or(jc == 0, exp_ref[jc] != exp_ref[jnp.maximum(jc - 1, 0)])

        @pl.when(changed)
        def _():
            wgb[...] = wg_ref[0].astype(BF16)
            wub[...] = wu_ref[0].astype(BF16)
            wdb[...] = wd_ref[0].astype(BF16)

        x = xbuf[slot].reshape(rows, -1).astype(BF16)
        hmid = _silu(_dot(x, wgb[...])) * _dot(x, wub[...])
        y = (_dot(hmid.astype(BF16), wdb[...]) * roww_ref[0]).reshape(ybuf.shape[1:])

        @pl.when(first)
        def _():
            ybuf[slot] = y

        @pl.when(jnp.logical_not(first))
        def _():
            r = (lax.broadcasted_iota(I32, y.shape, 0) * 8 + lax.broadcasted_iota(I32, y.shape, 1))
            mine = jnp.logical_and(r >= lo_ref[jc], r < hi_ref[jc])
            ybuf[slot] = jnp.where(mine, y, ybuf[slot])

        @pl.when(last)
        def _():
            def issue(g, _):
                for u in range(8):
                    pltpu.make_async_copy(ybuf.at[slot, g, pl.ds(u, 1)],
                                          out_hbm.at[pl.ds(dst_ref[0, 0, g * 8 + u], 1)], ssem.at[slot]).start()
                return 0
            lax.fori_loop(0, groups, issue, 0)

        @pl.when(jc == n - 1)
        def _():
            all_rows_done(ybuf, ssem, slot)

            @pl.when(tl >= 1)
            def _():
                all_rows_done(ybuf, ssem, 1 - slot)


def _experts(plan, t2d, w_gate, w_up, w_down):
    item_tile, item_e, item_lo, item_hi, n_items, row_tok, row_dst, row_w = plan
    n_tiles, _, tile = row_tok.shape
    d = t2d.shape[1]
    e, _, ed = w_gate.shape
    n_steps = item_tile.shape[0]
    smem_blk = lambda fn: pl.BlockSpec((1, 1, tile), fn, memory_space=pltpu.SMEM)
    cur = lambda s, tl, ex, lo, hi, n: (tl[jnp.minimum(s, n_steps - 1)], 0, 0)
    prev = lambda s, tl, ex, lo, hi, n: (tl[jnp.maximum(s - 1, 0)], 0, 0)
    wmap = lambda s, tl, ex, lo, hi, n: (ex[jnp.maximum(s - 1, 0)], 0, 0)
    return pl.pallas_call(
        _experts_body,
        out_shape=jax.ShapeDtypeStruct((n_tiles * tile, d), F32),
        grid_spec=pltpu.PrefetchScalarGridSpec(
            num_scalar_prefetch=5,
            grid=(n_steps,),
            in_specs=[smem_blk(cur), smem_blk(prev),
                      pl.BlockSpec((1, tile, 1), prev),
                      pl.BlockSpec(memory_space=pl.ANY),
                      pl.BlockSpec((1, d, ed), wmap),
                      pl.BlockSpec((1, d, ed), wmap),
                      pl.BlockSpec((1, ed, d), wmap)],
            out_specs=pl.BlockSpec(memory_space=pl.ANY),
            scratch_shapes=[pltpu.VMEM((2, tile // 8, 8, d), F32), pltpu.VMEM((2, tile // 8, 8, d), F32),
                            pltpu.VMEM((d, ed), BF16), pltpu.VMEM((d, ed), BF16), pltpu.VMEM((ed, d), BF16),
                            pltpu.SemaphoreType.DMA((2,)), pltpu.SemaphoreType.DMA((2,))]),
        compiler_params=_params(("arbitrary",)),
        name="moe_experts",
    )(item_tile, item_e, item_lo, item_hi, n_items, row_tok, row_dst, row_w.reshape(n_tiles, tile, 1),
      t2d, w_gate, w_up, w_down)


def _routing_plan(sel, wts, n_experts, tile):
    n_tok, k = sel.shape
    tk = n_tok * k
    assert tk % tile == 0
    n_tiles = tk // tile
    e_sorted, order, w_sorted = lax.sort((sel.reshape(-1), jnp.arange(tk, dtype=I32), wts.reshape(-1)), num_keys=1)
    bounds = jnp.searchsorted(e_sorted, jnp.arange(n_experts + 1, dtype=I32), side='left').astype(I32)
    start, end = bounds[:-1], bounds[1:]
    first_tile = start // tile
    n_items_e = jnp.where(end > start, (end - 1) // tile - first_tile + 1, 0)
    item_end = jnp.cumsum(n_items_e)
    n_items = item_end[-1]
    w_idx = jnp.arange(n_tiles + n_experts + 1, dtype=I32)
    live = w_idx < n_items
    item_e = jnp.minimum(jnp.searchsorted(item_end, w_idx, side='right'), n_experts - 1).astype(I32)
    item_tile = first_tile[item_e] + w_idx - (item_end - n_items_e)[item_e]
    item_lo = jnp.maximum(start[item_e], item_tile * tile) - item_tile * tile
    item_hi = jnp.minimum(end[item_e], (item_tile + 1) * tile) - item_tile * tile
    item_e = jnp.where(live, item_e, e_sorted[-1])
    item_tile = jnp.where(live, item_tile, n_tiles - 1)
    shape = (n_tiles, 1, tile)
    row_tok = (order // k).reshape(shape)
    row_dst = ((order % k) * n_tok + order // k).reshape(shape)
    return (item_tile.astype(I32), item_e.astype(I32), item_lo.astype(I32), item_hi.astype(I32),
            n_items.astype(I32).reshape(1), row_tok, row_dst, w_sorted.reshape(shape))


def _combine_body(x_ref, *refs, final_norm):
    r_refs = refs[:TOP_K]
    shared_ref, gt_ref, gf_ref, o_ref = refs[TOP_K:]
    routed = r_refs[0][...]
    for r_ref in r_refs[1:]:
        routed = routed + r_ref[...]
    y = x_ref[...] + gt_ref[0] * (routed + shared_ref[...])
    o_ref[...] = _rms(y, gf_ref[...]) if final_norm else y


def _combine(x2d, routed, shared, gate, mod_of_tile, g_final, *, first_tile, n_tiles, final_norm):
    t, d = x2d.shape
    tm = TOKEN_TILE
    tiles_all = t // tm
    tok = lambda off: pl.BlockSpec((tm, d), lambda i: (i + first_tile + off, 0))
    return pl.pallas_call(
        functools.partial(_combine_body, final_norm=final_norm),
        out_shape=jax.ShapeDtypeStruct((n_tiles * tm, d), F32),
        grid=(n_tiles,),
        in_specs=[tok(0)] + [tok(k * tiles_all) for k in range(TOP_K)] + [
            tok(0),
            pl.BlockSpec((1, tm, d), lambda i: (mod_of_tile(i + first_tile), 0, 0)),
            pl.BlockSpec((1, d), lambda i: (0, 0))],
        out_specs=pl.BlockSpec((tm, d), lambda i: (i, 0)),
        compiler_params=_params(("arbitrary",)),
        name="moe_combine",
    )(x2d, *([routed] * TOP_K), shared, gate, g_final.reshape(1, d))


SELECT_PAGES = 8


def _sample_select_body(pt_ref, *refs, n_sel):
    pages = refs[:SELECT_PAGES]
    q_ref, sel_ref, ks_ref = refs[SELECT_PAGES:]
    c = pl.program_id(1)
    ppb = MOBA_BLOCK // PAGE_SIZE
    nh = q_ref.shape[1]
    for n in range(SELECT_PAGES // ppb):
        tot = pages[n * ppb][0]
        for i in range(1, ppb):
            tot = tot + pages[n * ppb + i][0]
        ks_ref[:, pl.ds(c * (SELECT_PAGES // ppb) + n, 1), :] = jnp.sum(tot, axis=1, keepdims=True)

    @pl.when(c == pl.num_programs(1) - 1)
    def _():
        nblk = ks_ref.shape[1]
        for h in range(nh):
            gate = lax.dot_general(ks_ref[h], q_ref[0, h], (((1,), (1,)), ((), ())),
                                   preferred_element_type=F32, precision=lax.Precision.HIGHEST)
            row = lax.broadcasted_iota(I32, gate.shape, 0)
            g = gate
            picks = []
            for _ in range(n_sel):
                m = jnp.max(g, axis=0, keepdims=True)
                first = jnp.min(jnp.where(g == m, row, nblk), axis=0, keepdims=True)
                picks.append(first)
                g = jnp.where(row == first, -jnp.inf, g)
            picks.append(jnp.zeros((8 - n_sel, gate.shape[1]), I32))
            sel_ref[0, h] = jnp.concatenate(picks, axis=0)


def _sample_select(page_table, cache_k, q, n_sel):
    bd, n_pages = page_table.shape
    _, nh, page, dh = cache_k.shape
    tpad = q.shape[2]
    nblk = n_pages * PAGE_SIZE // MOBA_BLOCK
    page_spec = lambda i: pl.BlockSpec((1, nh, page, dh),
                                       lambda b, c, pt: (pt[b, c * SELECT_PAGES + i], 0, 0, 0))
    return pl.pallas_call(
        functools.partial(_sample_select_body, n_sel=n_sel),
        out_shape=jax.ShapeDtypeStruct((bd, nh, 8, tpad), I32),
        grid_spec=pltpu.PrefetchScalarGridSpec(
            num_scalar_prefetch=1,
            grid=(bd, n_pages // SELECT_PAGES),
            in_specs=[page_spec(i) for i in range(SELECT_PAGES)]
                     + [pl.BlockSpec((1, nh, tpad, dh), lambda b, c, pt: (b, 0, 0, 0))],
            out_specs=pl.BlockSpec((1, nh, 8, tpad), lambda b, c, pt: (b, 0, 0, 0)),
            scratch_shapes=[pltpu.VMEM((nh, nblk, dh), F32)]),
        compiler_params=_params(("arbitrary", "arbitrary")),
        name="moba_sample_select",
    )(page_table, *([cache_k] * SELECT_PAGES), q)


def _sample_attend_body(phys_ref, selb_ref, slope_ref, *refs, n_tok, n_sel, past):
    ppb = MOBA_BLOCK // PAGE_SIZE
    per_tok = n_sel * ppb
    nkv = n_tok * per_tok
    k_refs = refs[:nkv]
    v_refs = refs[nkv:2 * nkv]
    q_ref, kn_ref, vn_ref, o_ref = refs[2 * nkv:]
    b = pl.program_id(0)
    h = pl.program_id(1)
    nh = pl.num_programs(1)
    slope = slope_ref[h]
    q = (q_ref[0, 0] * (HEAD_DIM ** -0.5)).astype(BF16)
    tpad = q.shape[0]
    ncol = per_tok * PAGE_SIZE
    s_new = _dot_nt(q, kn_ref[0, 0].astype(BF16))
    ti = lax.broadcasted_iota(I32, s_new.shape, 0)
    oi = lax.broadcasted_iota(I32, s_new.shape, 1)
    s_new = jnp.where(jnp.logical_and(oi <= ti, oi < n_tok), s_new - slope * (ti - oi).astype(F32), MASKED)
    vn = vn_ref[0, 0].astype(BF16)
    col = lax.broadcasted_iota(I32, (tpad, ncol), 1)
    rowi = lax.broadcasted_iota(I32, (tpad, ncol), 0)
    outs = []
    for t in range(n_tok):
        kv = jnp.concatenate([k_refs[t * per_tok + i][0, 0] for i in range(per_tok)], axis=0).astype(BF16)
        vv = jnp.concatenate([v_refs[t * per_tok + i][0, 0] for i in range(per_tok)], axis=0).astype(BF16)
        s = _dot_nt(q, kv)
        blk_of_col = jnp.zeros((tpad, ncol), I32)
        for r in range(n_sel):
            sb = selb_ref[((b * nh + h) * n_tok + t) * n_sel + r]
            blk_of_col = jnp.where(col // MOBA_BLOCK == r, sb, blk_of_col)
        spos = blk_of_col * MOBA_BLOCK + col % MOBA_BLOCK
        s = jnp.where(rowi == t, s - slope * (past + t - spos).astype(F32), MASKED)
        so = s_new[t:t + 1]
        m = jnp.maximum(jnp.max(jnp.max(s, axis=1, keepdims=True), axis=0, keepdims=True),
                        jnp.max(so, axis=1, keepdims=True))
        p = jnp.exp(s - m)
        po = jnp.exp(so - m)
        l = jnp.sum(jnp.sum(p, axis=1, keepdims=True), axis=0, keepdims=True) + jnp.sum(po, axis=1, keepdims=True)
        o_sel = _dot(p.astype(BF16), vv)[t:t + 1]
        o_new = _dot(jnp.broadcast_to(po, (tpad, tpad)).astype(BF16), vn)[0:1]
        outs.append((o_sel + o_new) / l)
    outs.append(jnp.zeros((tpad - n_tok, HEAD_DIM), F32))
    o_ref[0, 0] = jnp.concatenate(outs, axis=0)


def _sample_attend(phys, selb, slopes, cache_k, cache_v, q, k_new, v_new, *, n_tok, n_sel, past):
    bd, nh, tpad, dh = q.shape
    page = cache_k.shape[2]
    ppb = MOBA_BLOCK // PAGE_SIZE
    nkv = n_tok * n_sel * ppb
    kv_spec = lambda i: pl.BlockSpec((1, 1, page, dh),
                                     lambda b, h, ph, sb, sl: (ph[(b * nh + h) * nkv + i], h, 0, 0))
    tok_spec = pl.BlockSpec((1, 1, tpad, dh), lambda b, h, ph, sb, sl: (b, h, 0, 0))
    return pl.pallas_call(
        functools.partial(_sample_attend_body, n_tok=n_tok, n_sel=n_sel, past=past),
        out_shape=jax.ShapeDtypeStruct((bd, nh, tpad, dh), F32),
        grid_spec=pltpu.PrefetchScalarGridSpec(
            num_scalar_prefetch=3,
            grid=(bd, nh),
            in_specs=[kv_spec(i) for i in range(nkv)] * 2 + [tok_spec] * 3,
            out_specs=tok_spec),
        compiler_params=_params(("arbitrary", "arbitrary")),
        name="moba_sample_attend",
    )(phys, selb, slopes, *([cache_k] * nkv), *([cache_v] * nkv), q, k_new, v_new)


def _moba_sample(q, k_new, v_new, cache_k, cache_v, page_table, slopes):
    bd, nh, n_tok, dh = q.shape
    n_pages = page_table.shape[1]
    past = n_pages * PAGE_SIZE
    ppb = MOBA_BLOCK // PAGE_SIZE
    assert past % MOBA_BLOCK == 0 and n_pages % SELECT_PAGES == 0 and n_tok <= 8
    n_sel = min(MOBA_TOPK, past // MOBA_BLOCK)
    assert n_sel > 0
    tpad = 8
    padt = lambda u: jnp.pad(u, ((0, 0), (0, 0), (0, tpad - n_tok), (0, 0)))
    qp = padt(q)
    sel = _sample_select(page_table, cache_k, qp, n_sel)[:, :, :n_sel, :n_tok]
    sel = jnp.transpose(sel, (0, 1, 3, 2))
    page_idx = sel[..., None] * ppb + jnp.arange(ppb, dtype=I32)
    phys = jnp.take_along_axis(page_table, page_idx.reshape(bd, -1), axis=1)
    out = _sample_attend(phys.reshape(-1), sel.reshape(-1), slopes, cache_k, cache_v, qp, padt(k_new), padt(v_new),
                         n_tok=n_tok, n_sel=n_sel, past=past)
    return out[:, :, :n_tok]


def _alibi_slopes(n_heads):
    return jnp.exp2(-8.0 * (jnp.arange(n_heads, dtype=F32) + 1.0) / n_heads)


def kernel(x_prompt, x_sample, cache_k, cache_v, page_table, state_conv, state_ssm, c_prompt, c_sample,
           w_ada, b_ada, g_mix, w_in, conv_w, conv_b, dt_bias, a_log, d_skip, g_ssd, g_att, w_out,
           g_ffn, w_router, router_bias, w_gate, w_up, w_down, ws_gate, ws_up, ws_down, g_final):
    depth = w_ada.shape[0]
    b, s, d = x_prompt.shape
    bd, ts, _ = x_sample.shape
    tm = TOKEN_TILE
    tp, tsn = b * s, bd * ts
    assert s % tm == 0 and tsn % tm == 0 and tm == MOBA_BLOCK
    n_ssd_heads = dt_bias.shape[1]
    ssd_w = n_ssd_heads * SSD_HEAD_DIM
    att_w = d - ssd_w
    nh = att_w // HEAD_DIM
    conv_dim = conv_w.shape[2]
    n_state = state_ssm.shape[-1]
    n_experts = w_router.shape[2]
    qkvz = 3 * att_w + ssd_w + conv_dim
    slopes = _alibi_slopes(nh)
    prompt_tiles, sample_tiles = tp // tm, tsn // tm
    tps = s // tm
    mod_of_tile = lambda i: jnp.where(i < prompt_tiles, i // tps, b + i - prompt_tiles)

    xp = x_prompt.reshape(tp, d)
    xs = x_sample.reshape(tsn, d)
    kp_l, vp_l, ks_l, vs_l, cp_l, cs_l, hp_l, hs_l = [], [], [], [], [], [], [], []
    for l in range(depth):
        c_all = jnp.concatenate([c_prompt, c_sample], axis=0)
        n_c = c_all.shape[0]
        c_all = jnp.pad(c_all, ((0, -n_c % 8), (0, 0)))
        mods = jnp.split(_adaln(c_all, w_ada[l], b_ada[l])[:n_c], 6, axis=-1)
        mod_p = [u[:b, None, :] for u in mods]
        mod_s = [jnp.repeat(u[b:], ts, axis=0).reshape(sample_tiles, tm, d) for u in mods]
        mod_u = [jnp.concatenate([jnp.broadcast_to(p_, (b, tm, d)), s_], axis=0) for p_, s_ in zip(mod_p, mod_s)]

        dt_cols = w_in[l][:, qkvz:]
        w_all = jnp.concatenate([w_in[l][:, :qkvz], jnp.repeat(dt_cols, SSD_HEAD_DIM, axis=1),
                                 jnp.pad(dt_cols, ((0, 0), (0, 128 - n_ssd_heads)))], axis=1).astype(BF16)
        w_out_b = w_out[l].astype(BF16)
        ssd_par = (conv_w[l], conv_b[l], dt_bias[l], a_log[l], d_skip[l], g_ssd[l])
        dims = dict(att_w=att_w, ssd_w=ssd_w, conv_dim=conv_dim)

        qt, kaug, vt, kpg, vpg, km, z_p, xbc_p, dte_p, dtc_p = _inproj(
            xp, mod_p[0], mod_p[1], g_mix[l], w_all, prompt_batch=(b, s), **dims)
        kmean = km.reshape(b, s // MOBA_BLOCK, nh, HEAD_DIM).transpose(0, 2, 1, 3)
        att_p = _attn_prompt(slopes, qt, kaug, vt, kmean)
        xbc_p3 = xbc_p.reshape(b, s, conv_dim)
        y_p, h_p = _ssd(xbc_p3, z_p.reshape(b, s, ssd_w), dte_p.reshape(b, s, ssd_w), dtc_p.reshape(b, s, 128),
                        jnp.zeros((b, 8, conv_dim), F32), jnp.zeros((b, ssd_w, n_state), F32), *ssd_par,
                        length=MOBA_BLOCK)
        x1 = _outproj(att_p, y_p.reshape(tp, ssd_w), xp, mod_p[2], g_att[l], w_out_b,
                      jnp.zeros((tp + tsn, d), F32), first_tile=0, prompt_batch=(b, s))
        kp_l.append(kpg)
        vp_l.append(vpg)
        cp_l.append(xbc_p3[:, s - (CONV_WIDTH - 1):])
        hp_l.append(h_p.reshape(b, n_ssd_heads, SSD_HEAD_DIM, n_state))

        q_s, k_s, v_s, z_s, xbc_s, dte_s, dtc_s = _inproj(xs, mod_s[0], mod_s[1], g_mix[l], w_all, **dims)
        heads = lambda u: u.reshape(bd, ts, nh, HEAD_DIM).transpose(0, 2, 1, 3)
        qh, kh, vh = heads(q_s), heads(k_s), heads(v_s)
        att_s = _moba_sample(qh, kh, vh, cache_k[l], cache_v[l], page_table, slopes)
        att_s = att_s.transpose(0, 2, 1, 3).reshape(tsn, att_w)
        xbc_s3 = xbc_s.reshape(bd, ts, conv_dim)
        conv_prev = jnp.pad(state_conv[l], ((0, 0), (8 - (CONV_WIDTH - 1), 0), (0, 0)))
        y_s, h_s = _ssd(xbc_s3, z_s.reshape(bd, ts, ssd_w), dte_s.reshape(bd, ts, ssd_w), dtc_s.reshape(bd, ts, 128),
                        conv_prev, state_ssm[l].reshape(bd, ssd_w, n_state), *ssd_par, length=128)
        x1 = _outproj(att_s, y_s.reshape(tsn, ssd_w), xs, mod_s[2], g_att[l], w_out_b, x1, first_tile=prompt_tiles)
        ks_l.append(kh)
        vs_l.append(vh)
        cs_l.append(jnp.concatenate([state_conv[l], xbc_s3], axis=1)[:, ts:])
        hs_l.append(h_s.reshape(bd, n_ssd_heads, SSD_HEAD_DIM, n_state))

        t_all, shared, sel_t, wts_t = _moe_pre(x1, mod_u[3], mod_u[4], mod_of_tile, g_ffn[l], w_router[l].astype(BF16),
                                               router_bias[l], ws_gate[l].astype(BF16), ws_up[l].astype(BF16),
                                               ws_down[l].astype(BF16))
        n_all = tp + tsn
        sel = sel_t.transpose(0, 2, 1).reshape(n_all, TOP_K)
        wts = wts_t.transpose(0, 2, 1).reshape(n_all, TOP_K)
        routed = _experts(_routing_plan(sel, wts, n_experts, EXPERT_TILE), t_all, w_gate[l], w_up[l], w_down[l])
        last = l == depth - 1
        fin = functools.partial(_combine, x1, routed, shared, mod_u[5], mod_of_tile, g_final, final_norm=last)
        xp = fin(first_tile=0, n_tiles=prompt_tiles)
        xs = fin(first_tile=prompt_tiles, n_tiles=sample_tiles)

    return (xp.reshape(b, s, d), xs.reshape(bd, ts, d), jnp.stack(kp_l), jnp.stack(vp_l), jnp.stack(ks_l),
            jnp.stack(vs_l), jnp.stack(cp_l), jnp.stack(cs_l), jnp.stack(hp_l), jnp.stack(hs_l))
```

```python
import functools

import jax
import jax.numpy as jnp
from jax import lax
from jax.experimental import pallas as pl
from jax.experimental.pallas import tpu as pltpu

F32 = jnp.float32
BF16 = jnp.bfloat16
I32 = jnp.int32

NORM_EPS = 1e-6
HEAD_DIM = 64
MOBA_BLOCK = 256
MOBA_TOPK = 3
PAGE_SIZE = 128
SSD_HEAD_DIM = 64
SSD_GROUPS = 2
SSD_STATE = 128
CONV_WIDTH = 4
N_EXPERT_GROUPS = 8
TOPK_GROUPS = 4
TOP_K = 8
ROUTED_SCALE = 2.5
MASKED = -1e30
TOKEN_TILE = 256
EXPERT_TILE = 256
PAST_UNROLL = 4
VMEM_LIMIT = 56 * 1024 * 1024


def _params(sem, vmem=VMEM_LIMIT):
    return pltpu.CompilerParams(dimension_semantics=sem, vmem_limit_bytes=vmem)


def _silu(x):
    return x * jax.nn.sigmoid(x)


def _softplus(x):
    return jnp.maximum(x, 0.0) + jnp.log1p(jnp.exp(-jnp.abs(x)))


def _rms(x, g):
    return x * lax.rsqrt(jnp.mean(x * x, axis=-1, keepdims=True) + NORM_EPS) * g


def _dot(a, b):
    return jnp.dot(a, b, preferred_element_type=F32)


def _dot_nt(a, b):
    return lax.dot_general(a, b, (((1,), (1,)), ((), ())), preferred_element_type=F32)


def _dot_exact(a, b):
    return jnp.dot(a, b, preferred_element_type=F32, precision=lax.Precision.HIGHEST)


def _adaln_body(c_ref, w_ref, b_ref, o_ref):
    a = _silu(c_ref[...]).astype(BF16)
    o_ref[...] = _dot(a, w_ref[...].astype(BF16)) + b_ref[...]


def _adaln(c, w_ada, b_ada):
    r, d = c.shape
    n = w_ada.shape[1]
    tn = 512
    return pl.pallas_call(
        _adaln_body,
        out_shape=jax.ShapeDtypeStruct((r, n), F32),
        grid=(n // tn,),
        in_specs=[pl.BlockSpec((r, d), lambda j: (0, 0)),
                  pl.BlockSpec((d, tn), lambda j: (0, j)),
                  pl.BlockSpec((1, tn), lambda j: (0, j))],
        out_specs=pl.BlockSpec((r, tn), lambda j: (0, j)),
        compiler_params=_params(("arbitrary",)),
        name="adaln",
    )(c, w_ada, b_ada.reshape(1, n))


def _inproj_body(x_ref, sh_ref, sc_ref, g_ref, w_ref, *outs, att_w, ssd_w, conv_dim, prompt):
    x = x_ref[...]
    h = (_rms(x, g_ref[...]) * (1.0 + sc_ref[0]) + sh_ref[0]).astype(BF16)
    tm = x.shape[0]
    a = att_w
    c0 = 3 * a
    c1 = c0 + ssd_w
    c2 = c1 + conv_dim
    c3 = c2 + ssd_w

    def proj(lo, hi):
        return _dot(h, w_ref[:, lo:hi])

    q, k, v = proj(0, a), proj(a, 2 * a), proj(2 * a, c0)
    if prompt:
        (qt_ref, kaug_ref, vt_ref, kp_ref, vp_ref, km_ref, z_ref, xbc_ref, dte_ref, dtc_ref) = outs
        nh = a // HEAD_DIM
        qt_ref[0] = q.T.reshape(nh, HEAD_DIM, tm)
        kt = k.T.reshape(nh, HEAD_DIM, tm)
        vt = v.T.reshape(nh, HEAD_DIM, tm)
        vt_ref[0] = vt.astype(BF16)
        for pg in range(tm // PAGE_SIZE):
            kp_ref[0, pg] = kt[:, :, pg * PAGE_SIZE:(pg + 1) * PAGE_SIZE]
            vp_ref[0, pg] = vt[:, :, pg * PAGE_SIZE:(pg + 1) * PAGE_SIZE]
        km_ref[0] = jnp.mean(k, axis=0, keepdims=True)
        lane = lax.broadcasted_iota(I32, (tm, 2 * HEAD_DIM), 1)
        row = lax.broadcasted_iota(I32, (tm, 2 * HEAD_DIM), 0)
        pos = jnp.where(lane == HEAD_DIM, (row % MOBA_BLOCK).astype(F32), 0.0)
        low = lane < HEAD_DIM
        for hp in range(nh // 2):
            kc = k[:, hp * 128:(hp + 1) * 128]
            kaug_ref[0, 2 * hp] = jnp.where(low, kc, pos).astype(BF16)
            kaug_ref[0, 2 * hp + 1] = jnp.where(low, pltpu.roll(kc, HEAD_DIM, 1), pos).astype(BF16)
    else:
        (q_ref, k_ref, v_ref, z_ref, xbc_ref, dte_ref, dtc_ref) = outs
        q_ref[...] = q
        k_ref[...] = k
        v_ref[...] = v
    z_ref[...] = proj(c0, c1)
    xbc_ref[...] = proj(c1, c2)
    dte_ref[...] = proj(c2, c3)
    dtc_ref[...] = proj(c3, c3 + 128)


def _inproj(x2d, shift, scale, g, w_all, *, att_w, ssd_w, conv_dim, prompt_batch=None):
    t, d = x2d.shape
    tm = TOKEN_TILE
    nw = w_all.shape[1]
    nt = t // tm
    prompt = prompt_batch is not None
    tok = lambda width: pl.BlockSpec((tm, width), lambda i: (i, 0))
    common_shapes = [jax.ShapeDtypeStruct((t, ssd_w), F32), jax.ShapeDtypeStruct((t, conv_dim), F32),
                     jax.ShapeDtypeStruct((t, ssd_w), F32), jax.ShapeDtypeStruct((t, 128), F32)]
    common_specs = [tok(ssd_w), tok(conv_dim), tok(ssd_w), tok(128)]
    if prompt:
        b, s = prompt_batch
        tps = s // tm
        nh = att_w // HEAD_DIM
        ppt = tm // PAGE_SIZE
        mod_map = lambda i: (i // tps, 0, 0)
        out_shape = [jax.ShapeDtypeStruct((b, nh, HEAD_DIM, s), F32),
                     jax.ShapeDtypeStruct((b, nh, s, 2 * HEAD_DIM), BF16),
                     jax.ShapeDtypeStruct((b, nh, HEAD_DIM, s), BF16),
                     jax.ShapeDtypeStruct((b, s // PAGE_SIZE, nh, HEAD_DIM, PAGE_SIZE), F32),
                     jax.ShapeDtypeStruct((b, s // PAGE_SIZE, nh, HEAD_DIM, PAGE_SIZE), F32),
                     jax.ShapeDtypeStruct((nt, 1, att_w), F32)] + common_shapes
        out_specs = [pl.BlockSpec((1, nh, HEAD_DIM, tm), lambda i: (i // tps, 0, 0, i % tps)),
                     pl.BlockSpec((1, nh, tm, 2 * HEAD_DIM), lambda i: (i // tps, 0, i % tps, 0)),
                     pl.BlockSpec((1, nh, HEAD_DIM, tm), lambda i: (i // tps, 0, 0, i % tps)),
                     pl.BlockSpec((1, ppt, nh, HEAD_DIM, PAGE_SIZE), lambda i: (i // tps, i % tps, 0, 0, 0)),
                     pl.BlockSpec((1, ppt, nh, HEAD_DIM, PAGE_SIZE), lambda i: (i // tps, i % tps, 0, 0, 0)),
                     pl.BlockSpec((1, 1, att_w), lambda i: (i, 0, 0))] + common_specs
        mod_block = (1, 1, d)
    else:
        mod_map = lambda i: (i, 0, 0)
        out_shape = [jax.ShapeDtypeStruct((t, att_w), F32)] * 3 + common_shapes
        out_specs = [tok(att_w)] * 3 + common_specs
        mod_block = (1, tm, d)
    body = functools.partial(_inproj_body, att_w=att_w, ssd_w=ssd_w, conv_dim=conv_dim, prompt=prompt)
    return pl.pallas_call(
        body,
        out_shape=out_shape,
        grid=(nt,),
        in_specs=[tok(d),
                  pl.BlockSpec(mod_block, mod_map),
                  pl.BlockSpec(mod_block, mod_map),
                  pl.BlockSpec((1, d), lambda i: (0, 0)),
                  pl.BlockSpec((d, nw), lambda i: (0, 0))],
        out_specs=out_specs,
        compiler_params=_params(("arbitrary",)),
        name="inproj_prompt" if prompt else "inproj_sample",
    )(x2d, shift, scale, g.reshape(1, d), w_all)


def _select_topk_rows(gate, n_valid, k):
    nrow = gate.shape[0]
    row = lax.broadcasted_iota(I32, gate.shape, 0)
    g = jnp.where(row < n_valid, gate, -jnp.inf)
    chosen = jnp.zeros(gate.shape, I32)
    for _ in range(k):
        m = jnp.max(g, axis=0, keepdims=True)
        cand = jnp.logical_and(g == m, g > -jnp.inf)
        first = jnp.min(jnp.where(cand, row, nrow), axis=0, keepdims=True)
        pick = row == first
        chosen = jnp.where(pick, 1, chosen)
        g = jnp.where(pick, -jnp.inf, g)
    return chosen


def _attn_prompt_body(slope_ref, qt_ref, k_ref, vt_ref, km_ref, o_ref, bias_ref, s0_ref, s1_ref):
    h = pl.program_id(1)
    t = pl.program_id(2)
    slope = slope_ref[h]
    blk = MOBA_BLOCK
    qt = qt_ref[0, 0]
    gate = _dot_exact(km_ref[0, 0], qt)
    chosen = _select_topk_rows(gate, t, MOBA_TOPK)
    jrow = lax.broadcasted_iota(I32, gate.shape, 0)
    iq = lax.broadcasted_iota(I32, gate.shape, 1)
    dist0 = ((t - jrow) * blk + iq).astype(F32)
    bias_ref[...] = jnp.where(chosen > 0, -slope * dist0, MASKED)

    r2 = lax.broadcasted_iota(I32, qt.shape, 0)
    extra = jnp.where(r2 == 0, slope, 0.0)
    qaug = jnp.concatenate([qt * (HEAD_DIM ** -0.5), extra], axis=0).astype(BF16)

    start = pl.multiple_of(t * blk, blk)
    s = _dot(k_ref[0, 0, pl.ds(start, blk), :], qaug)
    ik = lax.broadcasted_iota(I32, s.shape, 0)
    iqq = lax.broadcasted_iota(I32, s.shape, 1)
    s = jnp.where(ik <= iqq, s - slope * iqq.astype(F32), MASKED)
    m = jnp.max(s, axis=0, keepdims=True)
    p = jnp.exp(s - m)
    l = jnp.sum(p, axis=0, keepdims=True)
    acc = _dot(vt_ref[0, 0, :, pl.ds(start, blk)], p.astype(BF16))

    nb = bias_ref.shape[0]

    def block_start(trip, u):
        j = jnp.minimum(trip * PAST_UNROLL + u, nb - 1)
        return j, pl.multiple_of(j * blk, blk)

    def scores(trip, dst):
        for u in range(PAST_UNROLL):
            j, st = block_start(trip, u)
            dst[u] = _dot(k_ref[0, 0, pl.ds(st, blk), :], qaug) + bias_ref[pl.ds(j, 1), :]

    def consume(trip, src, carry):
        m, l, acc = carry
        ss = [src[u] for u in range(PAST_UNROLL)]
        m_new = m
        for s_u in ss:
            m_new = jnp.maximum(m_new, jnp.max(s_u, axis=0, keepdims=True))
        alpha = jnp.exp(m - m_new)
        l = alpha * l
        acc = alpha * acc
        for u in range(PAST_UNROLL):
            _, st = block_start(trip, u)
            p = jnp.exp(ss[u] - m_new)
            l = l + jnp.sum(p, axis=0, keepdims=True)
            acc = acc + _dot(vt_ref[0, 0, :, pl.ds(st, blk)], p.astype(BF16))
        return m_new, l, acc

    def body(i, carry):
        scores(2 * i + 1, s1_ref)
        carry = consume(2 * i, s0_ref, carry)
        scores(2 * i + 2, s0_ref)
        return consume(2 * i + 1, s1_ref, carry)

    scores(0, s0_ref)
    n_trips = (t + PAST_UNROLL - 1) // PAST_UNROLL
    m, l, acc = lax.fori_loop(0, (n_trips + 1) // 2, body, (m, l, acc))
    o_ref[0, 0] = acc / l


def _attn_prompt(slopes, qt, kaug, vt, kmean):
    b, nh, dh, s = qt.shape
    blk = MOBA_BLOCK
    nb = s // blk
    trip_scores = pltpu.VMEM((PAST_UNROLL, blk, blk), F32)
    return pl.pallas_call(
        _attn_prompt_body,
        out_shape=jax.ShapeDtypeStruct((b, nh, dh, s), F32),
        grid_spec=pltpu.PrefetchScalarGridSpec(
            num_scalar_prefetch=1,
            grid=(b, nh, nb),
            in_specs=[pl.BlockSpec((1, 1, dh, blk), lambda bi, hi, ti, sl: (bi, hi, 0, ti)),
                      pl.BlockSpec((1, 1, s, 2 * dh), lambda bi, hi, ti, sl: (bi, hi, 0, 0)),
                      pl.BlockSpec((1, 1, dh, s), lambda bi, hi, ti, sl: (bi, hi, 0, 0)),
                      pl.BlockSpec((1, 1, nb, dh), lambda bi, hi, ti, sl: (bi, hi, 0, 0))],
            out_specs=pl.BlockSpec((1, 1, dh, blk), lambda bi, hi, ti, sl: (bi, hi, 0, ti)),
            scratch_shapes=[pltpu.VMEM((nb, blk), F32), trip_scores, trip_scores]),
        compiler_params=_params(("arbitrary", "arbitrary", "arbitrary")),
        name="moba_prompt",
    )(slopes, qt, kaug, vt, kmean)


def _ssd_body(xbc_ref, z_ref, dte_ref, dtc_ref, cprev_ref, h0_ref, cw_ref, cb_ref, dtbe_ref, dtbc_ref,
              ale_ref, alc_ref, dsk_ref, g_ref, y_ref, hout_ref, buf_ref, st_ref, *, rows, length):
    c = pl.program_id(1)
    ng = SSD_GROUPS
    n = SSD_STATE
    width = z_ref.shape[-1]
    gw = width // ng

    @pl.when(c == 0)
    def _():
        if rows < length:
            buf_ref[...] = jnp.zeros(buf_ref.shape, F32)
        buf_ref[0:8, :] = cprev_ref[0]
        for g in range(ng):
            st_ref[g] = h0_ref[0, g * gw:(g + 1) * gw, :].T

    buf_ref[8:8 + rows, :] = xbc_ref[0]
    acc = cb_ref[...]
    for i in range(CONV_WIDTH):
        off = 8 - (CONV_WIDTH - 1) + i
        acc = acc + buf_ref[off:off + length, :] * cw_ref[i:i + 1, :]
    if rows == length:
        buf_ref[0:8, :] = buf_ref[length:length + 8, :]
    xc = _silu(acc)
    xs = xc[:, :width]
    bm = xc[:, width:width + ng * n]
    cm = xc[:, width + ng * n:]

    if rows < length:
        dte_full = jnp.concatenate([dte_ref[0], jnp.zeros((length - rows, width), F32)], axis=0)
        dtc_full = jnp.concatenate([dtc_ref[0], jnp.zeros((length - rows, 128), F32)], axis=0)
        live_e = lax.broadcasted_iota(I32, (length, width), 0) < rows
        live_c = lax.broadcasted_iota(I32, (length, 128), 0) < rows
        dt = jnp.where(live_e, _softplus(dte_full + dtbe_ref[...]), 0.0)
        dtc = jnp.where(live_c, _softplus(dtc_full + dtbc_ref[...]), 0.0)
    else:
        dt = _softplus(dte_ref[0] + dtbe_ref[...])
        dtc = _softplus(dtc_ref[0] + dtbc_ref[...])
    da = dt * (-jnp.exp(ale_ref[...]))
    dac = dtc * (-jnp.exp(alc_ref[...]))
    xdt = xs * dt

    ti = lax.broadcasted_iota(I32, (length, length), 0)
    si = lax.broadcasted_iota(I32, (length, length), 1)
    causal = ti >= si
    tri = jnp.where(causal, 1.0, 0.0)
    acs = _dot_exact(tri, da)
    acs_t = _dot_exact(tri, dac).T
    acs_last = acs[length - 1:length, :]
    e_acs = jnp.exp(acs)
    w_tail = jnp.exp(acs_last - acs)
    lane = lax.broadcasted_iota(I32, (length, 128), 1)

    ys = []
    for g in range(ng):
        bg = bm[:, g * n:(g + 1) * n]
        cg = cm[:, g * n:(g + 1) * n].astype(BF16)
        cb = _dot_nt(cg, bg.astype(BF16))
        st = st_ref[g]
        yg = _dot(cg, st.astype(BF16)) * e_acs[:, g * gw:(g + 1) * gw]
        pieces = []
        for pr in range(gw // 128):
            x_pair = xdt[:, g * gw + pr * 128: g * gw + (pr + 1) * 128].astype(BF16)
            halves = []
            for hh in range(2):
                head = (g * gw + pr * 128) // SSD_HEAD_DIM + hh
                col = acs[:, head * SSD_HEAD_DIM: head * SSD_HEAD_DIM + 1]
                rowv = acs_t[head:head + 1, :]
                decay = jnp.where(causal, jnp.exp(col - rowv), 0.0)
                halves.append(_dot((cb * decay).astype(BF16), x_pair))
            pieces.append(jnp.where(lane < SSD_HEAD_DIM, halves[0], halves[1]))
        ys.append(yg + jnp.concatenate(pieces, axis=1))
        xw = (xdt[:, g * gw:(g + 1) * gw] * w_tail[:, g * gw:(g + 1) * gw]).astype(BF16)
        st_ref[g] = st * jnp.exp(acs_last[:, g * gw:(g + 1) * gw]) + _dot(bg.T.astype(BF16), xw)
    y = jnp.concatenate(ys, axis=1)
    y = y + dsk_ref[...] * xs
    y = y[:rows] * _silu(z_ref[0])
    outs = []
    for g in range(ng):
        outs.append(_rms(y[:, g * gw:(g + 1) * gw], g_ref[:, g * gw:(g + 1) * gw]))
    y_ref[0] = jnp.concatenate(outs, axis=1)

    @pl.when(c == pl.num_programs(1) - 1)
    def _():
        for g in range(ng):
            hout_ref[0, g * gw:(g + 1) * gw, :] = st_ref[g].T


def _ssd(xbc, z, dte, dtc, conv_prev, h0, conv_w, conv_b, dt_bias, a_log, d_skip, g_ssd, *, length):
    bn, lt, cdim = xbc.shape
    width = z.shape[-1]
    n = h0.shape[-1]
    rows = min(lt, length)
    nc = lt // rows
    rep = width // dt_bias.shape[0]
    expand = lambda u: jnp.repeat(u, rep).reshape(1, width)
    lane_pad = lambda u: jnp.pad(u, (0, 128 - u.shape[0])).reshape(1, 128)
    seq = lambda w_: pl.BlockSpec((1, rows, w_), lambda b, c: (b, c, 0))
    const = lambda shape: pl.BlockSpec(shape, lambda b, c: (0,) * len(shape))
    body = functools.partial(_ssd_body, rows=rows, length=length)
    return pl.pallas_call(
        body,
        out_shape=[jax.ShapeDtypeStruct((bn, lt, width), F32), jax.ShapeDtypeStruct((bn, width, n), F32)],
        grid=(bn, nc),
        in_specs=[seq(cdim), seq(width), seq(width), seq(128),
                  pl.BlockSpec((1, 8, cdim), lambda b, c: (b, 0, 0)),
                  pl.BlockSpec((1, width, n), lambda b, c: (b, 0, 0)),
                  const((CONV_WIDTH, cdim)), const((1, cdim)), const((1, width)), const((1, 128)),
                  const((1, width)), const((1, 128)), const((1, width)), const((1, width))],
        out_specs=[seq(width), pl.BlockSpec((1, width, n), lambda b, c: (b, 0, 0))],
        scratch_shapes=[pltpu.VMEM((length + 8, cdim), F32),
                        pltpu.VMEM((SSD_GROUPS, n, width // SSD_GROUPS), F32)],
        compiler_params=_params(("arbitrary", "arbitrary")),
        name="ssd_scan",
    )(xbc, z, dte, dtc, conv_prev, h0, conv_w, conv_b.reshape(1, cdim), expand(dt_bias), lane_pad(dt_bias),
      expand(a_log), lane_pad(a_log), expand(d_skip), g_ssd.reshape(1, width))


def _outproj_body(att_ref, ssd_ref, x_ref, gt_ref, ga_ref, w_ref, *rest, transposed):
    o_ref = rest[-1]
    if transposed:
        a = att_ref[0].reshape(-1, att_ref.shape[-1]).T
    else:
        a = att_ref[...]
    mix = jnp.concatenate([_rms(a, ga_ref[...]), ssd_ref[...]], axis=-1).astype(BF16)
    o_ref[...] = x_ref[...] + gt_ref[0] * _dot(mix, w_ref[...])


def _outproj(att, ssd, x2d, gate, g_att, w_out, unified, *, first_tile, prompt_batch=None):
    t, d = x2d.shape
    tm = TOKEN_TILE
    tok = lambda width: pl.BlockSpec((tm, width), lambda i: (i, 0))
    sw = ssd.shape[-1]
    if prompt_batch is not None:
        b, s = prompt_batch
        tps = s // tm
        nh, dh = att.shape[1], att.shape[2]
        aw = nh * dh
        att_spec = pl.BlockSpec((1, nh, dh, tm), lambda i: (i // tps, 0, 0, i % tps))
        gate_spec = pl.BlockSpec((1, 1, d), lambda i: (i // tps, 0, 0))
    else:
        aw = att.shape[-1]
        att_spec = tok(aw)
        gate_spec = pl.BlockSpec((1, tm, d), lambda i: (i, 0, 0))
    return pl.pallas_call(
        functools.partial(_outproj_body, transposed=prompt_batch is not None),
        out_shape=jax.ShapeDtypeStruct(unified.shape, F32),
        grid=(t // tm,),
        in_specs=[att_spec, tok(sw), tok(d), gate_spec,
                  pl.BlockSpec((1, aw), lambda i: (0, 0)),
                  pl.BlockSpec((aw + sw, d), lambda i: (0, 0)),
                  pl.BlockSpec(memory_space=pl.ANY)],
        out_specs=pl.BlockSpec((tm, d), lambda i: (i + first_tile, 0)),
        input_output_aliases={6: 0},
        compiler_params=_params(("arbitrary",)),
        name="outproj",
    )(att, ssd, x2d, gate, g_att.reshape(1, aw), w_out, unified)


def _route_t(st, sbt):
    e, tm = st.shape
    per = e // N_EXPERT_GROUPS
    row = lax.broadcasted_iota(I32, (e, tm), 0)
    rg = lax.broadcasted_iota(I32, (per, tm), 0)
    gs = []
    for g in range(N_EXPERT_GROUPS):
        blk = sbt[g * per:(g + 1) * per]
        m1 = jnp.max(blk, axis=0, keepdims=True)
        first = jnp.min(jnp.where(blk == m1, rg, per), axis=0, keepdims=True)
        m2 = jnp.max(jnp.where(rg == first, -jnp.inf, blk), axis=0, keepdims=True)
        gs.append(m1 + m2)
    gscore = jnp.concatenate(gs, axis=0)
    gch = _select_topk_rows(gscore, N_EXPERT_GROUPS, TOPK_GROUPS)
    emask = jnp.concatenate([jnp.broadcast_to(gch[g:g + 1], (per, tm)) for g in range(N_EXPERT_GROUPS)], axis=0)
    cur = jnp.where(emask > 0, sbt, -jnp.inf)
    sel, wts = [], []
    for _ in range(TOP_K):
        m = jnp.max(cur, axis=0, keepdims=True)
        first = jnp.min(jnp.where(cur == m, row, e), axis=0, keepdims=True)
        pick = row == first
        sel.append(first)
        wts.append(jnp.sum(jnp.where(pick, st, 0.0), axis=0, keepdims=True))
        cur = jnp.where(pick, -jnp.inf, cur)
    sel = jnp.concatenate(sel, axis=0)
    w = jnp.concatenate(wts, axis=0)
    return sel, w / jnp.sum(w, axis=0, keepdims=True) * ROUTED_SCALE


def _rows_to_tiles(rows, ref, first=0):
    n = rows.shape[0]
    for s in range(8):
        ref[pl.ds(first * 8 + s, n, stride=8), :] = rows[:, s * 128:(s + 1) * 128]


def _tiles_to_rows(ref, n, first=0):
    return jnp.concatenate([ref[pl.ds(first * 8 + s, n, stride=8), :] for s in range(8)], axis=1)


def _moe_pre_body(x_ref, sh_ref, sc_ref, g_ref, wr_ref, rb_ref, wsg_ref, wsu_ref, wsd_ref,
                  t_ref, shared_ref, sel_ref, wt_ref):
    t = _rms(x_ref[...], g_ref[...]) * (1.0 + sc_ref[0]) + sh_ref[0]
    _rows_to_tiles(t, t_ref)
    tb = t.astype(BF16)
    st = jax.nn.sigmoid(_dot(tb, wr_ref[...])).T
    sel, w = _route_t(st, st + rb_ref[...])
    sel_ref[0] = sel
    wt_ref[0] = w
    hs = _silu(_dot(tb, wsg_ref[...])) * _dot(tb, wsu_ref[...])
    shared_ref[...] = _dot(hs.astype(BF16), wsd_ref[...])


def _moe_pre(x2d, shift, scale, mod_of_tile, g, w_router, router_bias, ws_gate, ws_up, ws_down):
    t, d = x2d.shape
    assert d == 8 * 128
    tm = TOKEN_TILE
    nt = t // tm
    e = w_router.shape[1]
    sd = ws_gate.shape[1]
    tok = lambda width: pl.BlockSpec((tm, width), lambda i: (i, 0))
    mod = pl.BlockSpec((1, tm, d), lambda i: (mod_of_tile(i), 0, 0))
    full = lambda shape: pl.BlockSpec(shape, lambda i: (0,) * len(shape))
    return pl.pallas_call(
        _moe_pre_body,
        out_shape=[jax.ShapeDtypeStruct((t * 8, 128), F32), jax.ShapeDtypeStruct((t, d), F32),
                   jax.ShapeDtypeStruct((nt, TOP_K, tm), I32), jax.ShapeDtypeStruct((nt, TOP_K, tm), F32)],
        grid=(nt,),
        in_specs=[tok(d), mod, mod, full((1, d)), full((d, e)), full((e, 1)),
                  full((d, sd)), full((d, sd)), full((sd, d))],
        out_specs=[pl.BlockSpec((tm * 8, 128), lambda i: (i, 0)), tok(d),
                   pl.BlockSpec((1, TOP_K, tm), lambda i: (i, 0, 0)),
                   pl.BlockSpec((1, TOP_K, tm), lambda i: (i, 0, 0))],
        compiler_params=_params(("arbitrary",)),
        name="moe_pre",
    )(x2d, shift, scale, g.reshape(1, d), w_router, router_bias.reshape(e, 1), ws_gate, ws_up, ws_down)


def _experts_body(tile_ref, exp_ref, lo_ref, hi_ref, n_ref, tok_ref, dst_ref, roww_ref, t_hbm, wg_ref, wu_ref, wd_ref,
                  out_hbm, xbuf, ybuf, wgb, wub, wdb, gsem, ssem):
    s = pl.program_id(0)
    n = n_ref[0]
    rows = xbuf.shape[0] // 16
    unroll = 8

    def tile_of(buf, slot, r):
        return buf.at[pl.ds(pl.multiple_of((slot * rows + r) * 8, 8), 8)]

    def all_rows_done(buf, sem, slot):
        whole = buf.at[pl.ds(pl.multiple_of(slot * rows * 8, 8), rows * 8)]
        pltpu.make_async_copy(whole, whole, sem.at[slot]).wait()

    opens_tile = jnp.logical_or(s == 0, tile_ref[s] != tile_ref[jnp.maximum(s - 1, 0)])

    @pl.when(jnp.logical_and(s < n, opens_tile))
    def _():
        slot = tile_ref[s] % 2

        def issue(g, _):
            for u in range(unroll):
                r = g * unroll + u
                pltpu.make_async_copy(t_hbm.at[tok_ref[0, 0, r]], tile_of(xbuf, slot, r), gsem.at[slot]).start()
            return 0
        lax.fori_loop(0, rows // unroll, issue, 0)

    j = s - 1

    @pl.when(jnp.logical_and(j >= 0, j < n))
    def _():
        jc = jnp.maximum(j, 0)
        tl = tile_ref[jc]
        slot = tl % 2
        first = jnp.logical_or(jc == 0, tile_ref[jnp.maximum(jc - 1, 0)] != tl)
        last = jnp.logical_or(jc == n - 1, tile_ref[jc + 1] != tl)

        @pl.when(first)
        def _():
            all_rows_done(xbuf, gsem, slot)

            @pl.when(tl >= 2)
            def _():
                all_rows_done(ybuf, ssem, slot)

        changed = jnp.logical_or(jc == 0, exp_ref[jc] != exp_ref[jnp.maximum(jc - 1, 0)])

        @pl.when(changed)
        def _():
            wgb[...] = wg_ref[0].astype(BF16)
            wub[...] = wu_ref[0].astype(BF16)
            wdb[...] = wd_ref[0].astype(BF16)

        base = slot * rows
        x = _tiles_to_rows(xbuf, rows, base).astype(BF16)
        hmid = _silu(_dot(x, wgb[...])) * _dot(x, wub[...])
        y = _dot(hmid.astype(BF16), wdb[...]) * roww_ref[0]

        @pl.when(first)
        def _():
            _rows_to_tiles(y, ybuf, base)

        @pl.when(jnp.logical_not(first))
        def _():
            r = lax.broadcasted_iota(I32, y.shape, 0)
            mine = jnp.logical_and(r >= lo_ref[jc], r < hi_ref[jc])
            _rows_to_tiles(jnp.where(mine, y, _tiles_to_rows(ybuf, rows, base)), ybuf, base)

        @pl.when(last)
        def _():
            def issue(g, _):
                for u in range(unroll):
                    r = g * unroll + u
                    pltpu.make_async_copy(tile_of(ybuf, slot, r), out_hbm.at[dst_ref[0, 0, r]], ssem.at[slot]).start()
                return 0
            lax.fori_loop(0, rows // unroll, issue, 0)

        @pl.when(jc == n - 1)
        def _():
            all_rows_done(ybuf, ssem, slot)

            @pl.when(tl >= 1)
            def _():
                all_rows_done(ybuf, ssem, 1 - slot)


def _experts(plan, t3, w_gate, w_up, w_down):
    item_tile, item_e, item_lo, item_hi, n_items, row_tok, row_dst, row_w = plan
    n_tiles, _, tile = row_tok.shape
    e, d, ed = w_gate.shape
    n_steps = item_tile.shape[0]
    smem_blk = lambda fn: pl.BlockSpec((1, 1, tile), fn, memory_space=pltpu.SMEM)
    cur = lambda s, tl, ex, lo, hi, n: (tl[jnp.minimum(s, n_steps - 1)], 0, 0)
    prev = lambda s, tl, ex, lo, hi, n: (tl[jnp.maximum(s - 1, 0)], 0, 0)
    wmap = lambda s, tl, ex, lo, hi, n: (ex[jnp.maximum(s - 1, 0)], 0, 0)
    return pl.pallas_call(
        _experts_body,
        out_shape=jax.ShapeDtypeStruct((n_tiles * tile, 8, 128), F32),
        grid_spec=pltpu.PrefetchScalarGridSpec(
            num_scalar_prefetch=5,
            grid=(n_steps,),
            in_specs=[smem_blk(cur), smem_blk(prev),
                      pl.BlockSpec((1, tile, 1), prev),
                      pl.BlockSpec(memory_space=pl.ANY),
                      pl.BlockSpec((1, d, ed), wmap),
                      pl.BlockSpec((1, d, ed), wmap),
                      pl.BlockSpec((1, ed, d), wmap)],
            out_specs=pl.BlockSpec(memory_space=pl.ANY),
            scratch_shapes=[pltpu.VMEM((2 * tile * 8, 128), F32), pltpu.VMEM((2 * tile * 8, 128), F32),
                            pltpu.VMEM((d, ed), BF16), pltpu.VMEM((d, ed), BF16), pltpu.VMEM((ed, d), BF16),
                            pltpu.SemaphoreType.DMA((2,)), pltpu.SemaphoreType.DMA((2,))]),
        compiler_params=_params(("arbitrary",)),
        name="moe_experts",
    )(item_tile, item_e, item_lo, item_hi, n_items, row_tok, row_dst, row_w.reshape(n_tiles, tile, 1),
      t3, w_gate, w_up, w_down)


def _routing_plan(sel, wts, n_experts, tile):
    n_tok, k = sel.shape
    tk = n_tok * k
    assert tk % tile == 0
    n_tiles = tk // tile
    e_sorted, order, w_sorted = lax.sort((sel.reshape(-1), jnp.arange(tk, dtype=I32), wts.reshape(-1)), num_keys=1)
    bounds = jnp.searchsorted(e_sorted, jnp.arange(n_experts + 1, dtype=I32), side='left').astype(I32)
    start, end = bounds[:-1], bounds[1:]
    first_tile = start // tile
    n_items_e = jnp.where(end > start, (end - 1) // tile - first_tile + 1, 0)
    item_end = jnp.cumsum(n_items_e)
    n_items = item_end[-1]
    w_idx = jnp.arange(n_tiles + n_experts + 1, dtype=I32)
    live = w_idx < n_items
    item_e = jnp.minimum(jnp.searchsorted(item_end, w_idx, side='right'), n_experts - 1).astype(I32)
    item_tile = first_tile[item_e] + w_idx - (item_end - n_items_e)[item_e]
    item_lo = jnp.maximum(start[item_e], item_tile * tile) - item_tile * tile
    item_hi = jnp.minimum(end[item_e], (item_tile + 1) * tile) - item_tile * tile
    item_e = jnp.where(live, item_e, e_sorted[-1])
    item_tile = jnp.where(live, item_tile, n_tiles - 1)
    shape = (n_tiles, 1, tile)
    row_tok = (order // k).reshape(shape)
    row_dst = ((order % k) * n_tok + order // k).reshape(shape)
    return (item_tile.astype(I32), item_e.astype(I32), item_lo.astype(I32), item_hi.astype(I32),
            n_items.astype(I32).reshape(1), row_tok, row_dst, w_sorted.reshape(shape))


def _combine_body(x_ref, *refs, final_norm):
    r_refs = refs[:TOP_K]
    shared_ref, gt_ref, gf_ref, o_ref, acc_ref = refs[TOP_K:]
    acc = r_refs[0][...]
    for r_ref in r_refs[1:]:
        acc = acc + r_ref[...]
    acc_ref[...] = acc
    y = x_ref[...] + gt_ref[0] * (_tiles_to_rows(acc_ref, x_ref.shape[0]) + shared_ref[...])
    o_ref[...] = _rms(y, gf_ref[...]) if final_norm else y


def _combine(x2d, routed, shared, gate, mod_of_tile, g_final, *, first_tile, n_tiles, final_norm):
    t, d = x2d.shape
    tm = TOKEN_TILE
    tiles_all = t // tm
    tok = pl.BlockSpec((tm, d), lambda i: (i + first_tile, 0))
    slot = lambda k: pl.BlockSpec((tm * 8, 128), lambda i: (i + first_tile + k * tiles_all, 0))
    return pl.pallas_call(
        functools.partial(_combine_body, final_norm=final_norm),
        out_shape=jax.ShapeDtypeStruct((n_tiles * tm, d), F32),
        grid=(n_tiles,),
        in_specs=[tok] + [slot(k) for k in range(TOP_K)] + [
            tok,
            pl.BlockSpec((1, tm, d), lambda i: (mod_of_tile(i + first_tile), 0, 0)),
            pl.BlockSpec((1, d), lambda i: (0, 0))],
        out_specs=pl.BlockSpec((tm, d), lambda i: (i, 0)),
        scratch_shapes=[pltpu.VMEM((tm * 8, 128), F32)],
        compiler_params=_params(("arbitrary",)),
        name="moe_combine",
    )(x2d, *([routed] * TOP_K), shared, gate, g_final.reshape(1, d))


SELECT_PAGES = 8


def _sample_select_body(pt_ref, *refs, n_sel):
    pages = refs[:SELECT_PAGES]
    qt_ref, sel_ref, gate_ref = refs[SELECT_PAGES:]
    c = pl.program_id(1)
    ppb = MOBA_BLOCK // PAGE_SIZE
    nh = qt_ref.shape[1]
    for n in range(SELECT_PAGES // ppb):
        tot = pages[n * ppb][0]
        for i in range(1, ppb):
            tot = tot + pages[n * ppb + i][0]
        ksum = jnp.sum(tot, axis=2, keepdims=True)
        gate_ref[:, pl.ds(c * (SELECT_PAGES // ppb) + n, 1), :] = jnp.sum(qt_ref[0] * ksum, axis=1, keepdims=True)

    @pl.when(c == pl.num_programs(1) - 1)
    def _():
        nblk = gate_ref.shape[1]
        for h in range(nh):
            gate = gate_ref[h]
            row = lax.broadcasted_iota(I32, gate.shape, 0)
            g = gate
            picks = []
            for _ in range(n_sel):
                m = jnp.max(g, axis=0, keepdims=True)
                first = jnp.min(jnp.where(g == m, row, nblk), axis=0, keepdims=True)
                picks.append(first)
                g = jnp.where(row == first, -jnp.inf, g)
            picks.append(jnp.zeros((8 - n_sel, gate.shape[1]), I32))
            sel_ref[0, h] = jnp.concatenate(picks, axis=0)


def _sample_select(page_table, cache_kt, qt, n_sel):
    bd, n_pages = page_table.shape
    _, nh, dh, page = cache_kt.shape
    tpad = qt.shape[3]
    nblk = n_pages * PAGE_SIZE // MOBA_BLOCK
    page_spec = lambda i: pl.BlockSpec((1, nh, dh, page),
                                       lambda b, c, pt: (pt[b, c * SELECT_PAGES + i], 0, 0, 0))
    return pl.pallas_call(
        functools.partial(_sample_select_body, n_sel=n_sel),
        out_shape=jax.ShapeDtypeStruct((bd, nh, 8, tpad), I32),
        grid_spec=pltpu.PrefetchScalarGridSpec(
            num_scalar_prefetch=1,
            grid=(bd, n_pages // SELECT_PAGES),
            in_specs=[page_spec(i) for i in range(SELECT_PAGES)]
                     + [pl.BlockSpec((1, nh, dh, tpad), lambda b, c, pt: (b, 0, 0, 0))],
            out_specs=pl.BlockSpec((1, nh, 8, tpad), lambda b, c, pt: (b, 0, 0, 0)),
            scratch_shapes=[pltpu.VMEM((nh, nblk, tpad), F32)]),
        compiler_params=_params(("arbitrary", "arbitrary")),
        name="moba_sample_select",
    )(page_table, *([cache_kt] * SELECT_PAGES), qt)


def _sample_attend_body(phys_ref, selb_ref, slope_ref, *refs, n_tok, n_sel, past):
    ppb = MOBA_BLOCK // PAGE_SIZE
    per_tok = n_sel * ppb
    nkv = n_tok * per_tok
    k_refs = refs[:nkv]
    v_refs = refs[nkv:2 * nkv]
    q_ref, kn_ref, vn_ref, o_ref = refs[2 * nkv:]
    b = pl.program_id(0)
    h = pl.program_id(1)
    nh = pl.num_programs(1)
    slope = slope_ref[h]
    q = (q_ref[0, 0] * (HEAD_DIM ** -0.5)).astype(BF16)
    tpad = q.shape[0]
    ncol = per_tok * PAGE_SIZE
    s_new = _dot_nt(q, kn_ref[0, 0].astype(BF16))
    ti = lax.broadcasted_iota(I32, s_new.shape, 0)
    oi = lax.broadcasted_iota(I32, s_new.shape, 1)
    s_new = jnp.where(jnp.logical_and(oi <= ti, oi < n_tok), s_new - slope * (ti - oi).astype(F32), MASKED)
    vn = vn_ref[0, 0].astype(BF16)
    col = lax.broadcasted_iota(I32, (tpad, ncol), 1)
    rowi = lax.broadcasted_iota(I32, (tpad, ncol), 0)
    outs = []
    for t in range(n_tok):
        kt = jnp.concatenate([k_refs[t * per_tok + i][0, 0] for i in range(per_tok)], axis=1).astype(BF16)
        vt = jnp.concatenate([v_refs[t * per_tok + i][0, 0] for i in range(per_tok)], axis=1).astype(BF16)
        s = _dot(q, kt)
        blk_of_col = jnp.zeros((tpad, ncol), I32)
        for r in range(n_sel):
            sb = selb_ref[((b * nh + h) * n_tok + t) * n_sel + r]
            blk_of_col = jnp.where(col // MOBA_BLOCK == r, sb, blk_of_col)
        spos = blk_of_col * MOBA_BLOCK + col % MOBA_BLOCK
        s = jnp.where(rowi == t, s - slope * (past + t - spos).astype(F32), MASKED)
        so = s_new[t:t + 1]
        m = jnp.maximum(jnp.max(jnp.max(s, axis=1, keepdims=True), axis=0, keepdims=True),
                        jnp.max(so, axis=1, keepdims=True))
        p = jnp.exp(s - m)
        po = jnp.exp(so - m)
        l = jnp.sum(jnp.sum(p, axis=1, keepdims=True), axis=0, keepdims=True) + jnp.sum(po, axis=1, keepdims=True)
        o_sel = _dot_nt(p.astype(BF16), vt)[t:t + 1]
        o_new = _dot(jnp.broadcast_to(po, (tpad, tpad)).astype(BF16), vn)[0:1]
        outs.append((o_sel + o_new) / l)
    outs.append(jnp.zeros((tpad - n_tok, HEAD_DIM), F32))
    o_ref[0, 0] = jnp.concatenate(outs, axis=0)


def _sample_attend(phys, selb, slopes, cache_kt, cache_vt, q, k_new, v_new, *, n_tok, n_sel, past):
    bd, nh, tpad, dh = q.shape
    page = cache_kt.shape[3]
    ppb = MOBA_BLOCK // PAGE_SIZE
    nkv = n_tok * n_sel * ppb
    kv_spec = lambda i: pl.BlockSpec((1, 1, dh, page),
                                     lambda b, h, ph, sb, sl: (ph[(b * nh + h) * nkv + i], h, 0, 0))
    tok_spec = pl.BlockSpec((1, 1, tpad, dh), lambda b, h, ph, sb, sl: (b, h, 0, 0))
    return pl.pallas_call(
        functools.partial(_sample_attend_body, n_tok=n_tok, n_sel=n_sel, past=past),
        out_shape=jax.ShapeDtypeStruct((bd, nh, tpad, dh), F32),
        grid_spec=pltpu.PrefetchScalarGridSpec(
            num_scalar_prefetch=3,
            grid=(bd, nh),
            in_specs=[kv_spec(i) for i in range(nkv)] * 2 + [tok_spec] * 3,
            out_specs=tok_spec),
        compiler_params=_params(("arbitrary", "arbitrary")),
        name="moba_sample_attend",
    )(phys, selb, slopes, *([cache_kt] * nkv), *([cache_vt] * nkv), q, k_new, v_new)


def _moba_sample(q, k_new, v_new, cache_kt, cache_vt, page_table, slopes):
    bd, nh, n_tok, dh = q.shape
    n_pages = page_table.shape[1]
    past = n_pages * PAGE_SIZE
    ppb = MOBA_BLOCK // PAGE_SIZE
    assert past % MOBA_BLOCK == 0 and n_pages % SELECT_PAGES == 0 and n_tok <= 8
    n_sel = min(MOBA_TOPK, past // MOBA_BLOCK)
    assert n_sel > 0
    tpad = 8
    padt = lambda u: jnp.pad(u, ((0, 0), (0, 0), (0, tpad - n_tok), (0, 0)))
    qp = padt(q)
    sel = _sample_select(page_table, cache_kt, jnp.swapaxes(qp, -1, -2), n_sel)[:, :, :n_sel, :n_tok]
    sel = jnp.transpose(sel, (0, 1, 3, 2))
    page_idx = sel[..., None] * ppb + jnp.arange(ppb, dtype=I32)
    phys = jnp.take_along_axis(page_table, page_idx.reshape(bd, -1), axis=1)
    out = _sample_attend(phys.reshape(-1), sel.reshape(-1), slopes, cache_kt, cache_vt, qp, padt(k_new), padt(v_new),
                         n_tok=n_tok, n_sel=n_sel, past=past)
    return out[:, :, :n_tok]


def _alibi_slopes(n_heads):
    return jnp.exp2(-8.0 * (jnp.arange(n_heads, dtype=F32) + 1.0) / n_heads)


def kernel(x_prompt, x_sample, cache_k, cache_v, page_table, state_conv, state_ssm, c_prompt, c_sample,
           w_ada, b_ada, g_mix, w_in, conv_w, conv_b, dt_bias, a_log, d_skip, g_ssd, g_att, w_out,
           g_ffn, w_router, router_bias, w_gate, w_up, w_down, ws_gate, ws_up, ws_down, g_final):
    depth = w_ada.shape[0]
    b, s, d = x_prompt.shape
    bd, ts, _ = x_sample.shape
    tm = TOKEN_TILE
    tp, tsn = b * s, bd * ts
    assert s % tm == 0 and tsn % tm == 0 and tm == MOBA_BLOCK
    n_ssd_heads = dt_bias.shape[1]
    ssd_w = n_ssd_heads * SSD_HEAD_DIM
    att_w = d - ssd_w
    nh = att_w // HEAD_DIM
    conv_dim = conv_w.shape[2]
    n_state = state_ssm.shape[-1]
    n_experts = w_router.shape[2]
    qkvz = 3 * att_w + ssd_w + conv_dim
    slopes = _alibi_slopes(nh)
    prompt_tiles, sample_tiles = tp // tm, tsn // tm
    tps = s // tm
    mod_of_tile = lambda i: jnp.where(i < prompt_tiles, i // tps, b + i - prompt_tiles)

    xp = x_prompt.reshape(tp, d)
    xs = x_sample.reshape(tsn, d)
    kp_l, vp_l, ks_l, vs_l, cp_l, cs_l, hp_l, hs_l = [], [], [], [], [], [], [], []
    for l in range(depth):
        c_all = jnp.concatenate([c_prompt, c_sample], axis=0)
        n_c = c_all.shape[0]
        c_all = jnp.pad(c_all, ((0, -n_c % 8), (0, 0)))
        mods = jnp.split(_adaln(c_all, w_ada[l], b_ada[l])[:n_c], 6, axis=-1)
        mod_p = [u[:b, None, :] for u in mods]
        mod_s = [jnp.repeat(u[b:], ts, axis=0).reshape(sample_tiles, tm, d) for u in mods]
        mod_u = [jnp.concatenate([jnp.broadcast_to(p_, (b, tm, d)), s_], axis=0) for p_, s_ in zip(mod_p, mod_s)]

        dt_cols = w_in[l][:, qkvz:]
        w_all = jnp.concatenate([w_in[l][:, :qkvz], jnp.repeat(dt_cols, SSD_HEAD_DIM, axis=1),
                                 jnp.pad(dt_cols, ((0, 0), (0, 128 - n_ssd_heads)))], axis=1).astype(BF16)
        w_out_b = w_out[l].astype(BF16)
        ssd_par = (conv_w[l], conv_b[l], dt_bias[l], a_log[l], d_skip[l], g_ssd[l])
        dims = dict(att_w=att_w, ssd_w=ssd_w, conv_dim=conv_dim)

        qt, kaug, vt, kpg, vpg, km, z_p, xbc_p, dte_p, dtc_p = _inproj(
            xp, mod_p[0], mod_p[1], g_mix[l], w_all, prompt_batch=(b, s), **dims)
        kmean = km.reshape(b, s // MOBA_BLOCK, nh, HEAD_DIM).transpose(0, 2, 1, 3)
        att_p = _attn_prompt(slopes, qt, kaug, vt, kmean)
        xbc_p3 = xbc_p.reshape(b, s, conv_dim)
        y_p, h_p = _ssd(xbc_p3, z_p.reshape(b, s, ssd_w), dte_p.reshape(b, s, ssd_w), dtc_p.reshape(b, s, 128),
                        jnp.zeros((b, 8, conv_dim), F32), jnp.zeros((b, ssd_w, n_state), F32), *ssd_par,
                        length=MOBA_BLOCK)
        x1 = _outproj(att_p, y_p.reshape(tp, ssd_w), xp, mod_p[2], g_att[l], w_out_b,
                      jnp.zeros((tp + tsn, d), F32), first_tile=0, prompt_batch=(b, s))
        kp_l.append(jnp.swapaxes(kpg, -1, -2))
        vp_l.append(jnp.swapaxes(vpg, -1, -2))
        cp_l.append(xbc_p3[:, s - (CONV_WIDTH - 1):])
        hp_l.append(h_p.reshape(b, n_ssd_heads, SSD_HEAD_DIM, n_state))

        q_s, k_s, v_s, z_s, xbc_s, dte_s, dtc_s = _inproj(xs, mod_s[0], mod_s[1], g_mix[l], w_all, **dims)
        heads = lambda u: u.reshape(bd, ts, nh, HEAD_DIM).transpose(0, 2, 1, 3)
        qh, kh, vh = heads(q_s), heads(k_s), heads(v_s)
        att_s = _moba_sample(qh, kh, vh, jnp.swapaxes(cache_k[l], -1, -2), jnp.swapaxes(cache_v[l], -1, -2),
                             page_table, slopes)
        att_s = att_s.transpose(0, 2, 1, 3).reshape(tsn, att_w)
        xbc_s3 = xbc_s.reshape(bd, ts, conv_dim)
        conv_prev = jnp.pad(state_conv[l], ((0, 0), (8 - (CONV_WIDTH - 1), 0), (0, 0)))
        y_s, h_s = _ssd(xbc_s3, z_s.reshape(bd, ts, ssd_w), dte_s.reshape(bd, ts, ssd_w), dtc_s.reshape(bd, ts, 128),
                        conv_prev, state_ssm[l].reshape(bd, ssd_w, n_state), *ssd_par, length=128)
        x1 = _outproj(att_s, y_s.reshape(tsn, ssd_w), xs, mod_s[2], g_att[l], w_out_b, x1, first_tile=prompt_tiles)
        ks_l.append(kh)
        vs_l.append(vh)
        cs_l.append(jnp.concatenate([state_conv[l], xbc_s3], axis=1)[:, ts:])
        hs_l.append(h_s.reshape(bd, n_ssd_heads, SSD_HEAD_DIM, n_state))

        t_all, shared, sel_t, wts_t = _moe_pre(x1, mod_u[3], mod_u[4], mod_of_tile, g_ffn[l], w_router[l].astype(BF16),
                                               router_bias[l], ws_gate[l].astype(BF16), ws_up[l].astype(BF16),
                                               ws_down[l].astype(BF16))
        n_all = tp + tsn
        sel = sel_t.transpose(0, 2, 1).reshape(n_all, TOP_K)
        wts = wts_t.transpose(0, 2, 1).reshape(n_all, TOP_K)
        routed = _experts(_routing_plan(sel, wts, n_experts, EXPERT_TILE), t_all.reshape(n_all, 8, 128),
                          w_gate[l], w_up[l], w_down[l]).reshape(TOP_K * n_all * 8, 128)
        last = l == depth - 1
        fin = functools.partial(_combine, x1, routed, shared, mod_u[5], mod_of_tile, g_final, final_norm=last)
        xp = fin(first_tile=0, n_tiles=prompt_tiles)
        xs = fin(first_tile=prompt_tiles, n_tiles=sample_tiles)

    return (xp.reshape(b, s, d), xs.reshape(bd, ts, d), jnp.stack(kp_l), jnp.stack(vp_l), jnp.stack(ks_l),
            jnp.stack(vs_l), jnp.stack(cp_l), jnp.stack(cs_l), jnp.stack(hp_l), jnp.stack(hs_l))
```

```python
import functools

import jax
import jax.numpy as jnp
from jax import lax
from jax.experimental import pallas as pl
from jax.experimental.pallas import tpu as pltpu

F32 = jnp.float32
BF16 = jnp.bfloat16
I32 = jnp.int32

NORM_EPS = 1e-6
HEAD_DIM = 64
MOBA_BLOCK = 256
MOBA_TOPK = 3
PAGE_SIZE = 128
SSD_HEAD_DIM = 64
SSD_GROUPS = 2
SSD_STATE = 128
CONV_WIDTH = 4
N_EXPERT_GROUPS = 8
TOPK_GROUPS = 4
TOP_K = 8
ROUTED_SCALE = 2.5
MASKED = -1e30
TOKEN_TILE = 256
EXPERT_TILE = 256
PAST_UNROLL = 4
V_PAD = 16
VMEM_LIMIT = 56 * 1024 * 1024


def _params(sem, vmem=VMEM_LIMIT):
    return pltpu.CompilerParams(dimension_semantics=sem, vmem_limit_bytes=vmem)


def _silu(x):
    return x * jax.nn.sigmoid(x)


def _softplus(x):
    return jnp.maximum(x, 0.0) + jnp.log1p(jnp.exp(-jnp.abs(x)))


def _rms(x, g):
    return x * lax.rsqrt(jnp.mean(x * x, axis=-1, keepdims=True) + NORM_EPS) * g


def _dot(a, b):
    return jnp.dot(a, b, preferred_element_type=F32)


def _dot_nt(a, b):
    return lax.dot_general(a, b, (((1,), (1,)), ((), ())), preferred_element_type=F32)


def _dot_exact(a, b):
    return jnp.dot(a, b, preferred_element_type=F32, precision=lax.Precision.HIGHEST)


def _adaln_body(c_ref, w_ref, b_ref, o_ref):
    a = _silu(c_ref[...]).astype(BF16)
    o_ref[...] = _dot(a, w_ref[...].astype(BF16)) + b_ref[...]


def _adaln(c, w_ada, b_ada):
    r, d = c.shape
    n = w_ada.shape[1]
    tn = 512
    return pl.pallas_call(
        _adaln_body,
        out_shape=jax.ShapeDtypeStruct((r, n), F32),
        grid=(n // tn,),
        in_specs=[pl.BlockSpec((r, d), lambda j: (0, 0)),
                  pl.BlockSpec((d, tn), lambda j: (0, j)),
                  pl.BlockSpec((1, tn), lambda j: (0, j))],
        out_specs=pl.BlockSpec((r, tn), lambda j: (0, j)),
        compiler_params=_params(("arbitrary",)),
        name="adaln",
    )(c, w_ada, b_ada.reshape(1, n))


def _inproj_body(x_ref, sh_ref, sc_ref, g_ref, w_ref, *outs, att_w, ssd_w, conv_dim, prompt):
    x = x_ref[...]
    h = (_rms(x, g_ref[...]) * (1.0 + sc_ref[0]) + sh_ref[0]).astype(BF16)
    tm = x.shape[0]
    a = att_w
    c0 = 3 * a
    c1 = c0 + ssd_w
    c2 = c1 + conv_dim
    c3 = c2 + ssd_w

    def proj(lo, hi):
        return _dot(h, w_ref[:, lo:hi])

    q, k, v = proj(0, a), proj(a, 2 * a), proj(2 * a, c0)
    if prompt:
        (qt_ref, kaug_ref, vt_ref, kp_ref, vp_ref, km_ref, z_ref, xbc_ref, dte_ref, dtc_ref) = outs
        nh = a // HEAD_DIM
        qt_ref[0] = q.T.reshape(nh, HEAD_DIM, tm)
        kt = k.T.reshape(nh, HEAD_DIM, tm)
        vt = v.T.reshape(nh, HEAD_DIM, tm)
        pad_row = lax.broadcasted_iota(I32, (nh, V_PAD, tm), 1)
        vt_ref[0] = jnp.concatenate([vt, jnp.where(pad_row == 0, 1.0, 0.0)], axis=1).astype(BF16)
        for pg in range(tm // PAGE_SIZE):
            kp_ref[0, pg] = kt[:, :, pg * PAGE_SIZE:(pg + 1) * PAGE_SIZE]
            vp_ref[0, pg] = vt[:, :, pg * PAGE_SIZE:(pg + 1) * PAGE_SIZE]
        km_ref[0] = jnp.mean(k, axis=0, keepdims=True)
        lane = lax.broadcasted_iota(I32, (tm, 2 * HEAD_DIM), 1)
        row = lax.broadcasted_iota(I32, (tm, 2 * HEAD_DIM), 0)
        pos = jnp.where(lane == HEAD_DIM, (row % MOBA_BLOCK).astype(F32), 0.0)
        low = lane < HEAD_DIM
        for hp in range(nh // 2):
            kc = k[:, hp * 128:(hp + 1) * 128]
            kaug_ref[0, 2 * hp] = jnp.where(low, kc, pos).astype(BF16)
            kaug_ref[0, 2 * hp + 1] = jnp.where(low, pltpu.roll(kc, HEAD_DIM, 1), pos).astype(BF16)
    else:
        (q_ref, k_ref, v_ref, z_ref, xbc_ref, dte_ref, dtc_ref) = outs
        q_ref[...] = q
        k_ref[...] = k
        v_ref[...] = v
    z_ref[...] = proj(c0, c1)
    xbc_ref[...] = proj(c1, c2)
    dte_ref[...] = proj(c2, c3)
    dtc_ref[...] = proj(c3, c3 + 128)


def _inproj(x2d, shift, scale, g, w_all, *, att_w, ssd_w, conv_dim, prompt_batch=None):
    t, d = x2d.shape
    tm = TOKEN_TILE
    nw = w_all.shape[1]
    nt = t // tm
    prompt = prompt_batch is not None
    tok = lambda width: pl.BlockSpec((tm, width), lambda i: (i, 0))
    common_shapes = [jax.ShapeDtypeStruct((t, ssd_w), F32), jax.ShapeDtypeStruct((t, conv_dim), F32),
                     jax.ShapeDtypeStruct((t, ssd_w), F32), jax.ShapeDtypeStruct((t, 128), F32)]
    common_specs = [tok(ssd_w), tok(conv_dim), tok(ssd_w), tok(128)]
    if prompt:
        b, s = prompt_batch
        tps = s // tm
        nh = att_w // HEAD_DIM
        ppt = tm // PAGE_SIZE
        mod_map = lambda i: (i // tps, 0, 0)
        out_shape = [jax.ShapeDtypeStruct((b, nh, HEAD_DIM, s), F32),
                     jax.ShapeDtypeStruct((b, nh, s, 2 * HEAD_DIM), BF16),
                     jax.ShapeDtypeStruct((b, nh, HEAD_DIM + V_PAD, s), BF16),
                     jax.ShapeDtypeStruct((b, s // PAGE_SIZE, nh, HEAD_DIM, PAGE_SIZE), F32),
                     jax.ShapeDtypeStruct((b, s // PAGE_SIZE, nh, HEAD_DIM, PAGE_SIZE), F32),
                     jax.ShapeDtypeStruct((nt, 1, att_w), F32)] + common_shapes
        out_specs = [pl.BlockSpec((1, nh, HEAD_DIM, tm), lambda i: (i // tps, 0, 0, i % tps)),
                     pl.BlockSpec((1, nh, tm, 2 * HEAD_DIM), lambda i: (i // tps, 0, i % tps, 0)),
                     pl.BlockSpec((1, nh, HEAD_DIM + V_PAD, tm), lambda i: (i // tps, 0, 0, i % tps)),
                     pl.BlockSpec((1, ppt, nh, HEAD_DIM, PAGE_SIZE), lambda i: (i // tps, i % tps, 0, 0, 0)),
                     pl.BlockSpec((1, ppt, nh, HEAD_DIM, PAGE_SIZE), lambda i: (i // tps, i % tps, 0, 0, 0)),
                     pl.BlockSpec((1, 1, att_w), lambda i: (i, 0, 0))] + common_specs
        mod_block = (1, 1, d)
    else:
        mod_map = lambda i: (i, 0, 0)
        out_shape = [jax.ShapeDtypeStruct((t, att_w), F32)] * 3 + common_shapes
        out_specs = [tok(att_w)] * 3 + common_specs
        mod_block = (1, tm, d)
    body = functools.partial(_inproj_body, att_w=att_w, ssd_w=ssd_w, conv_dim=conv_dim, prompt=prompt)
    return pl.pallas_call(
        body,
        out_shape=out_shape,
        grid=(nt,),
        in_specs=[tok(d),
                  pl.BlockSpec(mod_block, mod_map),
                  pl.BlockSpec(mod_block, mod_map),
                  pl.BlockSpec((1, d), lambda i: (0, 0)),
                  pl.BlockSpec((d, nw), lambda i: (0, 0))],
        out_specs=out_specs,
        compiler_params=_params(("arbitrary",)),
        name="inproj_prompt" if prompt else "inproj_sample",
    )(x2d, shift, scale, g.reshape(1, d), w_all)


def _select_topk_rows(gate, n_valid, k):
    nrow = gate.shape[0]
    row = lax.broadcasted_iota(I32, gate.shape, 0)
    g = jnp.where(row < n_valid, gate, -jnp.inf)
    chosen = jnp.zeros(gate.shape, I32)
    for _ in range(k):
        m = jnp.max(g, axis=0, keepdims=True)
        cand = jnp.logical_and(g == m, g > -jnp.inf)
        first = jnp.min(jnp.where(cand, row, nrow), axis=0, keepdims=True)
        pick = row == first
        chosen = jnp.where(pick, 1, chosen)
        g = jnp.where(pick, -jnp.inf, g)
    return chosen


def _attn_prompt_body(slope_ref, qt_ref, k_ref, vt_ref, km_ref, o_ref, bias_ref, s0_ref, s1_ref):
    h = pl.program_id(1)
    t = pl.program_id(2)
    slope = slope_ref[h]
    blk = MOBA_BLOCK
    qt = qt_ref[0, 0]
    gate = _dot_exact(km_ref[0, 0], qt)
    chosen = _select_topk_rows(gate, t, MOBA_TOPK)
    jrow = lax.broadcasted_iota(I32, gate.shape, 0)
    iq = lax.broadcasted_iota(I32, gate.shape, 1)
    dist0 = ((t - jrow) * blk + iq).astype(F32)
    bias_ref[...] = jnp.where(chosen > 0, -slope * dist0, MASKED)

    r2 = lax.broadcasted_iota(I32, qt.shape, 0)
    extra = jnp.where(r2 == 0, slope, 0.0)
    qaug = jnp.concatenate([qt * (HEAD_DIM ** -0.5), extra], axis=0).astype(BF16)

    start = pl.multiple_of(t * blk, blk)
    s = _dot(k_ref[0, 0, pl.ds(start, blk), :], qaug)
    ik = lax.broadcasted_iota(I32, s.shape, 0)
    iqq = lax.broadcasted_iota(I32, s.shape, 1)
    s = jnp.where(ik <= iqq, s - slope * iqq.astype(F32), MASKED)
    m = jnp.max(s, axis=0, keepdims=True)
    p = jnp.exp(s - m)
    acc = _dot(vt_ref[0, 0, :, pl.ds(start, blk)], p.astype(BF16))

    nb = bias_ref.shape[0]

    def block_start(trip, u):
        j = jnp.minimum(trip * PAST_UNROLL + u, nb - 1)
        return j, pl.multiple_of(j * blk, blk)

    def scores(trip, dst):
        for u in range(PAST_UNROLL):
            j, st = block_start(trip, u)
            dst[u] = _dot(k_ref[0, 0, pl.ds(st, blk), :], qaug) + bias_ref[pl.ds(j, 1), :]

    def consume(trip, src, carry):
        m, acc = carry
        ss = [src[u] for u in range(PAST_UNROLL)]
        m_new = m
        for s_u in ss:
            m_new = jnp.maximum(m_new, jnp.max(s_u, axis=0, keepdims=True))
        acc = jnp.exp(m - m_new) * acc
        for u in range(PAST_UNROLL):
            _, st = block_start(trip, u)
            p = jnp.exp(ss[u] - m_new)
            acc = acc + _dot(vt_ref[0, 0, :, pl.ds(st, blk)], p.astype(BF16))
        return m_new, acc

    def body(i, carry):
        scores(2 * i + 1, s1_ref)
        carry = consume(2 * i, s0_ref, carry)
        scores(2 * i + 2, s0_ref)
        return consume(2 * i + 1, s1_ref, carry)

    scores(0, s0_ref)
    n_trips = (t + PAST_UNROLL - 1) // PAST_UNROLL
    m, acc = lax.fori_loop(0, (n_trips + 1) // 2, body, (m, acc))
    o_ref[0, 0] = acc[:HEAD_DIM] / acc[HEAD_DIM:HEAD_DIM + 1]


def _attn_prompt(slopes, qt, kaug, vt, kmean):
    b, nh, dh, s = qt.shape
    vrows = vt.shape[2]
    blk = MOBA_BLOCK
    nb = s // blk
    trip_scores = pltpu.VMEM((PAST_UNROLL, blk, blk), F32)
    return pl.pallas_call(
        _attn_prompt_body,
        out_shape=jax.ShapeDtypeStruct((b, nh, dh, s), F32),
        grid_spec=pltpu.PrefetchScalarGridSpec(
            num_scalar_prefetch=1,
            grid=(b, nh, nb),
            in_specs=[pl.BlockSpec((1, 1, dh, blk), lambda bi, hi, ti, sl: (bi, hi, 0, ti)),
                      pl.BlockSpec((1, 1, s, 2 * dh), lambda bi, hi, ti, sl: (bi, hi, 0, 0)),
                      pl.BlockSpec((1, 1, vrows, s), lambda bi, hi, ti, sl: (bi, hi, 0, 0)),
                      pl.BlockSpec((1, 1, nb, dh), lambda bi, hi, ti, sl: (bi, hi, 0, 0))],
            out_specs=pl.BlockSpec((1, 1, dh, blk), lambda bi, hi, ti, sl: (bi, hi, 0, ti)),
            scratch_shapes=[pltpu.VMEM((nb, blk), F32), trip_scores, trip_scores]),
        compiler_params=_params(("arbitrary", "arbitrary", "arbitrary")),
        name="moba_prompt",
    )(slopes, qt, kaug, vt, kmean)


def _ssd_body(xbc_ref, z_ref, dte_ref, dtc_ref, cprev_ref, h0_ref, cw_ref, cb_ref, dtbe_ref, dtbc_ref,
              ale_ref, alc_ref, dsk_ref, g_ref, y_ref, hout_ref, buf_ref, st_ref, *, rows, length):
    c = pl.program_id(1)
    ng = SSD_GROUPS
    n = SSD_STATE
    width = z_ref.shape[-1]
    gw = width // ng

    @pl.when(c == 0)
    def _():
        if rows < length:
            buf_ref[...] = jnp.zeros(buf_ref.shape, F32)
        buf_ref[0:8, :] = cprev_ref[0]
        for g in range(ng):
            st_ref[g] = h0_ref[0, g * gw:(g + 1) * gw, :].T

    buf_ref[8:8 + rows, :] = xbc_ref[0]
    acc = cb_ref[...]
    for i in range(CONV_WIDTH):
        off = 8 - (CONV_WIDTH - 1) + i
        acc = acc + buf_ref[off:off + length, :] * cw_ref[i:i + 1, :]
    if rows == length:
        buf_ref[0:8, :] = buf_ref[length:length + 8, :]
    xc = _silu(acc)
    xs = xc[:, :width]
    bm = xc[:, width:width + ng * n]
    cm = xc[:, width + ng * n:]

    if rows < length:
        dte_full = jnp.concatenate([dte_ref[0], jnp.zeros((length - rows, width), F32)], axis=0)
        dtc_full = jnp.concatenate([dtc_ref[0], jnp.zeros((length - rows, 128), F32)], axis=0)
        live_e = lax.broadcasted_iota(I32, (length, width), 0) < rows
        live_c = lax.broadcasted_iota(I32, (length, 128), 0) < rows
        dt = jnp.where(live_e, _softplus(dte_full + dtbe_ref[...]), 0.0)
        dtc = jnp.where(live_c, _softplus(dtc_full + dtbc_ref[...]), 0.0)
    else:
        dt = _softplus(dte_ref[0] + dtbe_ref[...])
        dtc = _softplus(dtc_ref[0] + dtbc_ref[...])
    da = dt * (-jnp.exp(ale_ref[...]))
    dac = dtc * (-jnp.exp(alc_ref[...]))
    xdt = xs * dt

    ti = lax.broadcasted_iota(I32, (length, length), 0)
    si = lax.broadcasted_iota(I32, (length, length), 1)
    causal = ti >= si
    tri = jnp.where(causal, 1.0, 0.0)
    acs = _dot_exact(tri, da)
    acs_t = _dot_exact(tri, dac).T
    acs_last = acs[length - 1:length, :]
    e_acs = jnp.exp(acs)
    w_tail = jnp.exp(acs_last - acs)
    lane = lax.broadcasted_iota(I32, (length, 128), 1)

    ys = []
    for g in range(ng):
        bg = bm[:, g * n:(g + 1) * n]
        cg = cm[:, g * n:(g + 1) * n].astype(BF16)
        cb = _dot_nt(cg, bg.astype(BF16))
        st = st_ref[g]
        yg = _dot(cg, st.astype(BF16)) * e_acs[:, g * gw:(g + 1) * gw]
        pieces = []
        for pr in range(gw // 128):
            x_pair = xdt[:, g * gw + pr * 128: g * gw + (pr + 1) * 128].astype(BF16)
            halves = []
            for hh in range(2):
                head = (g * gw + pr * 128) // SSD_HEAD_DIM + hh
                col = acs[:, head * SSD_HEAD_DIM: head * SSD_HEAD_DIM + 1]
                rowv = acs_t[head:head + 1, :]
                decay = jnp.where(causal, jnp.exp(col - rowv), 0.0)
                halves.append(_dot((cb * decay).astype(BF16), x_pair))
            pieces.append(jnp.where(lane < SSD_HEAD_DIM, halves[0], halves[1]))
        ys.append(yg + jnp.concatenate(pieces, axis=1))
        xw = (xdt[:, g * gw:(g + 1) * gw] * w_tail[:, g * gw:(g + 1) * gw]).astype(BF16)
        st_ref[g] = st * jnp.exp(acs_last[:, g * gw:(g + 1) * gw]) + _dot(bg.T.astype(BF16), xw)
    y = jnp.concatenate(ys, axis=1)
    y = y + dsk_ref[...] * xs
    y = y[:rows] * _silu(z_ref[0])
    outs = []
    for g in range(ng):
        outs.append(_rms(y[:, g * gw:(g + 1) * gw], g_ref[:, g * gw:(g + 1) * gw]))
    y_ref[0] = jnp.concatenate(outs, axis=1)

    @pl.when(c == pl.num_programs(1) - 1)
    def _():
        for g in range(ng):
            hout_ref[0, g * gw:(g + 1) * gw, :] = st_ref[g].T


def _ssd(xbc, z, dte, dtc, conv_prev, h0, conv_w, conv_b, dt_bias, a_log, d_skip, g_ssd, *, length):
    bn, lt, cdim = xbc.shape
    width = z.shape[-1]
    n = h0.shape[-1]
    rows = min(lt, length)
    nc = lt // rows
    rep = width // dt_bias.shape[0]
    expand = lambda u: jnp.repeat(u, rep).reshape(1, width)
    lane_pad = lambda u: jnp.pad(u, (0, 128 - u.shape[0])).reshape(1, 128)
    seq = lambda w_: pl.BlockSpec((1, rows, w_), lambda b, c: (b, c, 0))
    const = lambda shape: pl.BlockSpec(shape, lambda b, c: (0,) * len(shape))
    body = functools.partial(_ssd_body, rows=rows, length=length)
    return pl.pallas_call(
        body,
        out_shape=[jax.ShapeDtypeStruct((bn, lt, width), F32), jax.ShapeDtypeStruct((bn, width, n), F32)],
        grid=(bn, nc),
        in_specs=[seq(cdim), seq(width), seq(width), seq(128),
                  pl.BlockSpec((1, 8, cdim), lambda b, c: (b, 0, 0)),
                  pl.BlockSpec((1, width, n), lambda b, c: (b, 0, 0)),
                  const((CONV_WIDTH, cdim)), const((1, cdim)), const((1, width)), const((1, 128)),
                  const((1, width)), const((1, 128)), const((1, width)), const((1, width))],
        out_specs=[seq(width), pl.BlockSpec((1, width, n), lambda b, c: (b, 0, 0))],
        scratch_shapes=[pltpu.VMEM((length + 8, cdim), F32),
                        pltpu.VMEM((SSD_GROUPS, n, width // SSD_GROUPS), F32)],
        compiler_params=_params(("arbitrary", "arbitrary")),
        name="ssd_scan",
    )(xbc, z, dte, dtc, conv_prev, h0, conv_w, conv_b.reshape(1, cdim), expand(dt_bias), lane_pad(dt_bias),
      expand(a_log), lane_pad(a_log), expand(d_skip), g_ssd.reshape(1, width))


def _outproj_body(att_ref, ssd_ref, x_ref, gt_ref, ga_ref, w_ref, *rest, transposed):
    o_ref = rest[-1]
    if transposed:
        a = att_ref[0].reshape(-1, att_ref.shape[-1]).T
    else:
        a = att_ref[...]
    mix = jnp.concatenate([_rms(a, ga_ref[...]), ssd_ref[...]], axis=-1).astype(BF16)
    o_ref[...] = x_ref[...] + gt_ref[0] * _dot(mix, w_ref[...])


def _outproj(att, ssd, x2d, gate, g_att, w_out, unified, *, first_tile, prompt_batch=None):
    t, d = x2d.shape
    tm = TOKEN_TILE
    tok = lambda width: pl.BlockSpec((tm, width), lambda i: (i, 0))
    sw = ssd.shape[-1]
    if prompt_batch is not None:
        b, s = prompt_batch
        tps = s // tm
        nh, dh = att.shape[1], att.shape[2]
        aw = nh * dh
        att_spec = pl.BlockSpec((1, nh, dh, tm), lambda i: (i // tps, 0, 0, i % tps))
        gate_spec = pl.BlockSpec((1, 1, d), lambda i: (i // tps, 0, 0))
    else:
        aw = att.shape[-1]
        att_spec = tok(aw)
        gate_spec = pl.BlockSpec((1, tm, d), lambda i: (i, 0, 0))
    return pl.pallas_call(
        functools.partial(_outproj_body, transposed=prompt_batch is not None),
        out_shape=jax.ShapeDtypeStruct(unified.shape, F32),
        grid=(t // tm,),
        in_specs=[att_spec, tok(sw), tok(d), gate_spec,
                  pl.BlockSpec((1, aw), lambda i: (0, 0)),
                  pl.BlockSpec((aw + sw, d), lambda i: (0, 0)),
                  pl.BlockSpec(memory_space=pl.ANY)],
        out_specs=pl.BlockSpec((tm, d), lambda i: (i + first_tile, 0)),
        input_output_aliases={6: 0},
        compiler_params=_params(("arbitrary",)),
        name="outproj",
    )(att, ssd, x2d, gate, g_att.reshape(1, aw), w_out, unified)


def _route_t(st, sbt):
    e, tm = st.shape
    per = e // N_EXPERT_GROUPS
    row = lax.broadcasted_iota(I32, (e, tm), 0)
    rg = lax.broadcasted_iota(I32, (per, tm), 0)
    gs = []
    for g in range(N_EXPERT_GROUPS):
        blk = sbt[g * per:(g + 1) * per]
        m1 = jnp.max(blk, axis=0, keepdims=True)
        first = jnp.min(jnp.where(blk == m1, rg, per), axis=0, keepdims=True)
        m2 = jnp.max(jnp.where(rg == first, -jnp.inf, blk), axis=0, keepdims=True)
        gs.append(m1 + m2)
    gscore = jnp.concatenate(gs, axis=0)
    gch = _select_topk_rows(gscore, N_EXPERT_GROUPS, TOPK_GROUPS)
    emask = jnp.concatenate([jnp.broadcast_to(gch[g:g + 1], (per, tm)) for g in range(N_EXPERT_GROUPS)], axis=0)
    cur = jnp.where(emask > 0, sbt, -jnp.inf)
    sel, wts = [], []
    for _ in range(TOP_K):
        m = jnp.max(cur, axis=0, keepdims=True)
        first = jnp.min(jnp.where(cur == m, row, e), axis=0, keepdims=True)
        pick = row == first
        sel.append(first)
        wts.append(jnp.sum(jnp.where(pick, st, 0.0), axis=0, keepdims=True))
        cur = jnp.where(pick, -jnp.inf, cur)
    sel = jnp.concatenate(sel, axis=0)
    w = jnp.concatenate(wts, axis=0)
    return sel, w / jnp.sum(w, axis=0, keepdims=True) * ROUTED_SCALE


def _rows_to_tiles(rows, ref, first=0):
    n = rows.shape[0]
    for s in range(8):
        ref[pl.ds(first * 8 + s, n, stride=8), :] = rows[:, s * 128:(s + 1) * 128]


def _tiles_to_rows(ref, n, first=0):
    return jnp.concatenate([ref[pl.ds(first * 8 + s, n, stride=8), :] for s in range(8)], axis=1)


def _moe_pre_body(x_ref, sh_ref, sc_ref, g_ref, wr_ref, rb_ref, wsg_ref, wsu_ref, wsd_ref,
                  t_ref, shared_ref, sel_ref, wt_ref):
    t = _rms(x_ref[...], g_ref[...]) * (1.0 + sc_ref[0]) + sh_ref[0]
    _rows_to_tiles(t, t_ref)
    tb = t.astype(BF16)
    st = jax.nn.sigmoid(_dot(tb, wr_ref[...])).T
    sel, w = _route_t(st, st + rb_ref[...])
    sel_ref[0] = sel
    wt_ref[0] = w
    hs = _silu(_dot(tb, wsg_ref[...])) * _dot(tb, wsu_ref[...])
    shared_ref[...] = _dot(hs.astype(BF16), wsd_ref[...])


def _moe_pre(x2d, shift, scale, mod_of_tile, g, w_router, router_bias, ws_gate, ws_up, ws_down):
    t, d = x2d.shape
    assert d == 8 * 128
    tm = TOKEN_TILE
    nt = t // tm
    e = w_router.shape[1]
    sd = ws_gate.shape[1]
    tok = lambda width: pl.BlockSpec((tm, width), lambda i: (i, 0))
    mod = pl.BlockSpec((1, tm, d), lambda i: (mod_of_tile(i), 0, 0))
    full = lambda shape: pl.BlockSpec(shape, lambda i: (0,) * len(shape))
    return pl.pallas_call(
        _moe_pre_body,
        out_shape=[jax.ShapeDtypeStruct((t * 8, 128), F32), jax.ShapeDtypeStruct((t, d), F32),
                   jax.ShapeDtypeStruct((nt, TOP_K, tm), I32), jax.ShapeDtypeStruct((nt, TOP_K, tm), F32)],
        grid=(nt,),
        in_specs=[tok(d), mod, mod, full((1, d)), full((d, e)), full((e, 1)),
                  full((d, sd)), full((d, sd)), full((sd, d))],
        out_specs=[pl.BlockSpec((tm * 8, 128), lambda i: (i, 0)), tok(d),
                   pl.BlockSpec((1, TOP_K, tm), lambda i: (i, 0, 0)),
                   pl.BlockSpec((1, TOP_K, tm), lambda i: (i, 0, 0))],
        compiler_params=_params(("arbitrary",)),
        name="moe_pre",
    )(x2d, shift, scale, g.reshape(1, d), w_router, router_bias.reshape(e, 1), ws_gate, ws_up, ws_down)


def _experts_body(tile_ref, exp_ref, lo_ref, hi_ref, n_ref, tok_ref, dst_ref, roww_ref, t_hbm, wg_ref, wu_ref, wd_ref,
                  out_hbm, xbuf, ybuf, wgb, wub, wdb, gsem, ssem):
    s = pl.program_id(0)
    n = n_ref[0]
    rows = xbuf.shape[0] // 16
    unroll = 8

    def tile_of(buf, slot, r):
        return buf.at[pl.ds(pl.multiple_of((slot * rows + r) * 8, 8), 8)]

    def all_rows_done(buf, sem, slot):
        whole = buf.at[pl.ds(pl.multiple_of(slot * rows * 8, 8), rows * 8)]
        pltpu.make_async_copy(whole, whole, sem.at[slot]).wait()

    opens_tile = jnp.logical_or(s == 0, tile_ref[s] != tile_ref[jnp.maximum(s - 1, 0)])

    @pl.when(jnp.logical_and(s < n, opens_tile))
    def _():
        slot = tile_ref[s] % 2

        def issue(g, _):
            for u in range(unroll):
                r = g * unroll + u
                pltpu.make_async_copy(t_hbm.at[tok_ref[0, 0, r]], tile_of(xbuf, slot, r), gsem.at[slot]).start()
            return 0
        lax.fori_loop(0, rows // unroll, issue, 0)

    j = s - 1

    @pl.when(jnp.logical_and(j >= 0, j < n))
    def _():
        jc = jnp.maximum(j, 0)
        tl = tile_ref[jc]
        slot = tl % 2
        first = jnp.logical_or(jc == 0, tile_ref[jnp.maximum(jc - 1, 0)] != tl)
        last = jnp.logical_or(jc == n - 1, tile_ref[jc + 1] != tl)

        @pl.when(first)
        def _():
            all_rows_done(xbuf, gsem, slot)

            @pl.when(tl >= 2)
            def _():
                all_rows_done(ybuf, ssem, slot)

        changed = jnp.logical_or(jc == 0, exp_ref[jc] != exp_ref[jnp.maximum(jc - 1, 0)])

        @pl.when(changed)
        def _():
            wgb[...] = wg_ref[0].astype(BF16)
            wub[...] = wu_ref[0].astype(BF16)
            wdb[...] = wd_ref[0].astype(BF16)

        base = slot * rows
        x = _tiles_to_rows(xbuf, rows, base).astype(BF16)
        hmid = _silu(_dot(x, wgb[...])) * _dot(x, wub[...])
        y = _dot(hmid.astype(BF16), wdb[...]) * roww_ref[0]

        @pl.when(first)
        def _():
            _rows_to_tiles(y, ybuf, base)

        @pl.when(jnp.logical_not(first))
        def _():
            r = lax.broadcasted_iota(I32, y.shape, 0)
            mine = jnp.logical_and(r >= lo_ref[jc], r < hi_ref[jc])
            _rows_to_tiles(jnp.where(mine, y, _tiles_to_rows(ybuf, rows, base)), ybuf, base)

        @pl.when(last)
        def _():
            def issue(g, _):
                for u in range(unroll):
                    r = g * unroll + u
                    pltpu.make_async_copy(tile_of(ybuf, slot, r), out_hbm.at[dst_ref[0, 0, r]], ssem.at[slot]).start()
                return 0
            lax.fori_loop(0, rows // unroll, issue, 0)

        @pl.when(jc == n - 1)
        def _():
            all_rows_done(ybuf, ssem, slot)

            @pl.when(tl >= 1)
            def _():
                all_rows_done(ybuf, ssem, 1 - slot)


def _experts(plan, t3, w_gate, w_up, w_down):
    item_tile, item_e, item_lo, item_hi, n_items, row_tok, row_dst, row_w = plan
    n_tiles, _, tile = row_tok.shape
    e, d, ed = w_gate.shape
    n_steps = item_tile.shape[0]
    smem_blk = lambda fn: pl.BlockSpec((1, 1, tile), fn, memory_space=pltpu.SMEM)
    cur = lambda s, tl, ex, lo, hi, n: (tl[jnp.minimum(s, n_steps - 1)], 0, 0)
    prev = lambda s, tl, ex, lo, hi, n: (tl[jnp.maximum(s - 1, 0)], 0, 0)
    wmap = lambda s, tl, ex, lo, hi, n: (ex[jnp.maximum(s - 1, 0)], 0, 0)
    return pl.pallas_call(
        _experts_body,
        out_shape=jax.ShapeDtypeStruct((n_tiles * tile, 8, 128), F32),
        grid_spec=pltpu.PrefetchScalarGridSpec(
            num_scalar_prefetch=5,
            grid=(n_steps,),
            in_specs=[smem_blk(cur), smem_blk(prev),
                      pl.BlockSpec((1, tile, 1), prev),
                      pl.BlockSpec(memory_space=pl.ANY),
                      pl.BlockSpec((1, d, ed), wmap),
                      pl.BlockSpec((1, d, ed), wmap),
                      pl.BlockSpec((1, ed, d), wmap)],
            out_specs=pl.BlockSpec(memory_space=pl.ANY),
            scratch_shapes=[pltpu.VMEM((2 * tile * 8, 128), F32), pltpu.VMEM((2 * tile * 8, 128), F32),
                            pltpu.VMEM((d, ed), BF16), pltpu.VMEM((d, ed), BF16), pltpu.VMEM((ed, d), BF16),
                            pltpu.SemaphoreType.DMA((2,)), pltpu.SemaphoreType.DMA((2,))]),
        compiler_params=_params(("arbitrary",)),
        name="moe_experts",
    )(item_tile, item_e, item_lo, item_hi, n_items, row_tok, row_dst, row_w.reshape(n_tiles, tile, 1),
      t3, w_gate, w_up, w_down)


def _routing_plan(sel, wts, n_experts, tile):
    n_tok, k = sel.shape
    tk = n_tok * k
    assert tk % tile == 0
    n_tiles = tk // tile
    e_sorted, order, w_sorted = lax.sort((sel.reshape(-1), jnp.arange(tk, dtype=I32), wts.reshape(-1)), num_keys=1)
    bounds = jnp.searchsorted(e_sorted, jnp.arange(n_experts + 1, dtype=I32), side='left').astype(I32)
    start, end = bounds[:-1], bounds[1:]
    first_tile = start // tile
    n_items_e = jnp.where(end > start, (end - 1) // tile - first_tile + 1, 0)
    item_end = jnp.cumsum(n_items_e)
    n_items = item_end[-1]
    w_idx = jnp.arange(n_tiles + n_experts + 1, dtype=I32)
    live = w_idx < n_items
    item_e = jnp.minimum(jnp.searchsorted(item_end, w_idx, side='right', method='compare_all'),
                         n_experts - 1).astype(I32)
    item_tile = first_tile[item_e] + w_idx - (item_end - n_items_e)[item_e]
    item_lo = jnp.maximum(start[item_e], item_tile * tile) - item_tile * tile
    item_hi = jnp.minimum(end[item_e], (item_tile + 1) * tile) - item_tile * tile
    item_e = jnp.where(live, item_e, e_sorted[-1])
    item_tile = jnp.where(live, item_tile, n_tiles - 1)
    shape = (n_tiles, 1, tile)
    row_tok = (order // k).reshape(shape)
    row_dst = ((order % k) * n_tok + order // k).reshape(shape)
    return (item_tile.astype(I32), item_e.astype(I32), item_lo.astype(I32), item_hi.astype(I32),
            n_items.astype(I32).reshape(1), row_tok, row_dst, w_sorted.reshape(shape))


def _combine_body(x_ref, *refs, final_norm):
    r_refs = refs[:TOP_K]
    shared_ref, gt_ref, gf_ref, o_ref, acc_ref = refs[TOP_K:]
    acc = r_refs[0][...]
    for r_ref in r_refs[1:]:
        acc = acc + r_ref[...]
    acc_ref[...] = acc
    y = x_ref[...] + gt_ref[0] * (_tiles_to_rows(acc_ref, x_ref.shape[0]) + shared_ref[...])
    o_ref[...] = _rms(y, gf_ref[...]) if final_norm else y


def _combine(x2d, routed, shared, gate, mod_of_tile, g_final, *, first_tile, n_tiles, final_norm):
    t, d = x2d.shape
    tm = TOKEN_TILE
    tiles_all = t // tm
    tok = pl.BlockSpec((tm, d), lambda i: (i + first_tile, 0))
    slot = lambda k: pl.BlockSpec((tm * 8, 128), lambda i: (i + first_tile + k * tiles_all, 0))
    return pl.pallas_call(
        functools.partial(_combine_body, final_norm=final_norm),
        out_shape=jax.ShapeDtypeStruct((n_tiles * tm, d), F32),
        grid=(n_tiles,),
        in_specs=[tok] + [slot(k) for k in range(TOP_K)] + [
            tok,
            pl.BlockSpec((1, tm, d), lambda i: (mod_of_tile(i + first_tile), 0, 0)),
            pl.BlockSpec((1, d), lambda i: (0, 0))],
        out_specs=pl.BlockSpec((tm, d), lambda i: (i, 0)),
        scratch_shapes=[pltpu.VMEM((tm * 8, 128), F32)],
        compiler_params=_params(("arbitrary",)),
        name="moe_combine",
    )(x2d, *([routed] * TOP_K), shared, gate, g_final.reshape(1, d))


SELECT_PAGES = 8


def _sample_select_body(pt_ref, *refs, n_sel):
    pages = refs[:SELECT_PAGES]
    qt_ref, sel_ref, gate_ref = refs[SELECT_PAGES:]
    c = pl.program_id(1)
    ppb = MOBA_BLOCK // PAGE_SIZE
    nh = qt_ref.shape[1]
    for n in range(SELECT_PAGES // ppb):
        tot = pages[n * ppb][0]
        for i in range(1, ppb):
            tot = tot + pages[n * ppb + i][0]
        ksum = jnp.sum(tot, axis=2, keepdims=True)
        gate_ref[:, pl.ds(c * (SELECT_PAGES // ppb) + n, 1), :] = jnp.sum(qt_ref[0] * ksum, axis=1, keepdims=True)

    @pl.when(c == pl.num_programs(1) - 1)
    def _():
        nblk = gate_ref.shape[1]
        for h in range(nh):
            gate = gate_ref[h]
            row = lax.broadcasted_iota(I32, gate.shape, 0)
            g = gate
            picks = []
            for _ in range(n_sel):
                m = jnp.max(g, axis=0, keepdims=True)
                first = jnp.min(jnp.where(g == m, row, nblk), axis=0, keepdims=True)
                picks.append(first)
                g = jnp.where(row == first, -jnp.inf, g)
            picks.append(jnp.zeros((8 - n_sel, gate.shape[1]), I32))
            sel_ref[0, h] = jnp.concatenate(picks, axis=0)


def _sample_select(page_table, cache_kt, qt, n_sel):
    bd, n_pages = page_table.shape
    _, nh, dh, page = cache_kt.shape
    tpad = qt.shape[3]
    nblk = n_pages * PAGE_SIZE // MOBA_BLOCK
    page_spec = lambda i: pl.BlockSpec((1, nh, dh, page),
                                       lambda b, c, pt: (pt[b, c * SELECT_PAGES + i], 0, 0, 0))
    return pl.pallas_call(
        functools.partial(_sample_select_body, n_sel=n_sel),
        out_shape=jax.ShapeDtypeStruct((bd, nh, 8, tpad), I32),
        grid_spec=pltpu.PrefetchScalarGridSpec(
            num_scalar_prefetch=1,
            grid=(bd, n_pages // SELECT_PAGES),
            in_specs=[page_spec(i) for i in range(SELECT_PAGES)]
                     + [pl.BlockSpec((1, nh, dh, tpad), lambda b, c, pt: (b, 0, 0, 0))],
            out_specs=pl.BlockSpec((1, nh, 8, tpad), lambda b, c, pt: (b, 0, 0, 0)),
            scratch_shapes=[pltpu.VMEM((nh, nblk, tpad), F32)]),
        compiler_params=_params(("arbitrary", "arbitrary")),
        name="moba_sample_select",
    )(page_table, *([cache_kt] * SELECT_PAGES), qt)


def _sample_attend_body(pt_ref, selb_ref, slope_ref, kt_hbm, vt_hbm, q_ref, kn_ref, vn_ref, o_ref,
                        kbuf, vbuf, ksem, vsem, *, n_tok, n_sel, past):
    ppb = MOBA_BLOCK // PAGE_SIZE
    per_tok = n_sel * ppb
    nkv = n_tok * per_tok
    b = pl.program_id(0)
    h = pl.program_id(1)
    nh = q_ref.shape[1]
    step = b * nh + h
    n_steps = pl.num_programs(0) * nh

    def fetch(stp, slot):
        hh = stp % nh
        bb = stp // nh
        for i in range(nkv):
            logical_block = selb_ref[stp * (nkv // ppb) + i // ppb]
            pg = pt_ref[bb, logical_block * ppb + i % ppb]
            pltpu.make_async_copy(kt_hbm.at[pg, hh], kbuf.at[slot, i], ksem.at[slot]).start()
            pltpu.make_async_copy(vt_hbm.at[pg, hh], vbuf.at[slot, i], vsem.at[slot]).start()

    @pl.when(step == 0)
    def _():
        fetch(0, 0)

    @pl.when(step + 1 < n_steps)
    def _():
        fetch(step + 1, (step + 1) % 2)

    slot = step % 2
    pltpu.make_async_copy(kbuf.at[slot], kbuf.at[slot], ksem.at[slot]).wait()
    pltpu.make_async_copy(vbuf.at[slot], vbuf.at[slot], vsem.at[slot]).wait()

    slope = slope_ref[h]
    q = (q_ref[0, h] * (HEAD_DIM ** -0.5)).astype(BF16)
    tpad = q.shape[0]
    ncol = per_tok * PAGE_SIZE
    s_new = _dot_nt(q, kn_ref[0, h].astype(BF16))
    ti = lax.broadcasted_iota(I32, s_new.shape, 0)
    oi = lax.broadcasted_iota(I32, s_new.shape, 1)
    s_new = jnp.where(jnp.logical_and(oi <= ti, oi < n_tok), s_new - slope * (ti - oi).astype(F32), MASKED)
    vn = vn_ref[0, h].astype(BF16)
    col = lax.broadcasted_iota(I32, (tpad, ncol), 1)
    rowi = lax.broadcasted_iota(I32, (tpad, ncol), 0)
    outs = []
    for t in range(n_tok):
        kt = jnp.concatenate([kbuf[slot, t * per_tok + i] for i in range(per_tok)], axis=1).astype(BF16)
        vt = jnp.concatenate([vbuf[slot, t * per_tok + i] for i in range(per_tok)], axis=1).astype(BF16)
        s = _dot(q, kt)
        blk_of_col = jnp.zeros((tpad, ncol), I32)
        for r in range(n_sel):
            sb = selb_ref[(step * n_tok + t) * n_sel + r]
            blk_of_col = jnp.where(col // MOBA_BLOCK == r, sb, blk_of_col)
        spos = blk_of_col * MOBA_BLOCK + col % MOBA_BLOCK
        s = jnp.where(rowi == t, s - slope * (past + t - spos).astype(F32), MASKED)
        so = s_new[t:t + 1]
        m = jnp.maximum(jnp.max(jnp.max(s, axis=1, keepdims=True), axis=0, keepdims=True),
                        jnp.max(so, axis=1, keepdims=True))
        p = jnp.exp(s - m)
        po = jnp.exp(so - m)
        l = jnp.sum(jnp.sum(p, axis=1, keepdims=True), axis=0, keepdims=True) + jnp.sum(po, axis=1, keepdims=True)
        o_sel = _dot_nt(p.astype(BF16), vt)[t:t + 1]
        o_new = _dot(jnp.broadcast_to(po, (tpad, tpad)).astype(BF16), vn)[0:1]
        outs.append((o_sel + o_new) / l)
    outs.append(jnp.zeros((tpad - n_tok, HEAD_DIM), F32))
    o_ref[0, h] = jnp.concatenate(outs, axis=0)


def _sample_attend(page_table, selb, slopes, cache_kt, cache_vt, q, k_new, v_new, *, n_tok, n_sel, past):
    bd, nh, tpad, dh = q.shape
    page = cache_kt.shape[3]
    ppb = MOBA_BLOCK // PAGE_SIZE
    nkv = n_tok * n_sel * ppb
    pool = pl.BlockSpec(memory_space=pl.ANY)
    seq_spec = pl.BlockSpec((1, nh, tpad, dh), lambda b, h, ph, sb, sl: (b, 0, 0, 0))
    slices = pltpu.VMEM((2, nkv, dh, page), F32)
    return pl.pallas_call(
        functools.partial(_sample_attend_body, n_tok=n_tok, n_sel=n_sel, past=past),
        out_shape=jax.ShapeDtypeStruct((bd, nh, tpad, dh), F32),
        grid_spec=pltpu.PrefetchScalarGridSpec(
            num_scalar_prefetch=3,
            grid=(bd, nh),
            in_specs=[pool, pool, seq_spec, seq_spec, seq_spec],
            out_specs=seq_spec,
            scratch_shapes=[slices, slices, pltpu.SemaphoreType.DMA((2,)), pltpu.SemaphoreType.DMA((2,))]),
        compiler_params=_params(("arbitrary", "arbitrary")),
        name="moba_sample_attend",
    )(page_table, selb, slopes, cache_kt, cache_vt, q, k_new, v_new)


def _moba_sample(q, k_new, v_new, cache_kt, cache_vt, page_table, slopes):
    bd, nh, n_tok, dh = q.shape
    n_pages = page_table.shape[1]
    past = n_pages * PAGE_SIZE
    assert past % MOBA_BLOCK == 0 and n_pages % SELECT_PAGES == 0 and n_tok <= 8
    n_sel = min(MOBA_TOPK, past // MOBA_BLOCK)
    assert n_sel > 0
    tpad = 8
    padt = lambda u: jnp.pad(u, ((0, 0), (0, 0), (0, tpad - n_tok), (0, 0)))
    qp = padt(q)
    sel = _sample_select(page_table, cache_kt, jnp.swapaxes(qp, -1, -2), n_sel)[:, :, :n_sel, :n_tok]
    sel = jnp.transpose(sel, (0, 1, 3, 2))
    out = _sample_attend(page_table, sel.reshape(-1), slopes, cache_kt, cache_vt, qp, padt(k_new), padt(v_new),
                         n_tok=n_tok, n_sel=n_sel, past=past)
    return out[:, :, :n_tok]


def _alibi_slopes(n_heads):
    return jnp.exp2(-8.0 * (jnp.arange(n_heads, dtype=F32) + 1.0) / n_heads)


def kernel(x_prompt, x_sample, cache_k, cache_v, page_table, state_conv, state_ssm, c_prompt, c_sample,
           w_ada, b_ada, g_mix, w_in, conv_w, conv_b, dt_bias, a_log, d_skip, g_ssd, g_att, w_out,
           g_ffn, w_router, router_bias, w_gate, w_up, w_down, ws_gate, ws_up, ws_down, g_final):
    depth = w_ada.shape[0]
    b, s, d = x_prompt.shape
    bd, ts, _ = x_sample.shape
    tm = TOKEN_TILE
    tp, tsn = b * s, bd * ts
    assert s % tm == 0 and tsn % tm == 0 and tm == MOBA_BLOCK
    n_ssd_heads = dt_bias.shape[1]
    ssd_w = n_ssd_heads * SSD_HEAD_DIM
    att_w = d - ssd_w
    nh = att_w // HEAD_DIM
    conv_dim = conv_w.shape[2]
    n_state = state_ssm.shape[-1]
    n_experts = w_router.shape[2]
    qkvz = 3 * att_w + ssd_w + conv_dim
    slopes = _alibi_slopes(nh)
    prompt_tiles, sample_tiles = tp // tm, tsn // tm
    tps = s // tm
    mod_of_tile = lambda i: jnp.where(i < prompt_tiles, i // tps, b + i - prompt_tiles)

    xp = x_prompt.reshape(tp, d)
    xs = x_sample.reshape(tsn, d)
    kp_l, vp_l, ks_l, vs_l, cp_l, cs_l, hp_l, hs_l = [], [], [], [], [], [], [], []
    for l in range(depth):
        c_all = jnp.concatenate([c_prompt, c_sample], axis=0)
        n_c = c_all.shape[0]
        c_all = jnp.pad(c_all, ((0, -n_c % 8), (0, 0)))
        mods = jnp.split(_adaln(c_all, w_ada[l], b_ada[l])[:n_c], 6, axis=-1)
        mod_p = [u[:b, None, :] for u in mods]
        mod_s = [jnp.repeat(u[b:], ts, axis=0).reshape(sample_tiles, tm, d) for u in mods]
        mod_u = [jnp.concatenate([jnp.broadcast_to(p_, (b, tm, d)), s_], axis=0) for p_, s_ in zip(mod_p, mod_s)]

        dt_cols = w_in[l][:, qkvz:]
        w_all = jnp.concatenate([w_in[l][:, :qkvz], jnp.repeat(dt_cols, SSD_HEAD_DIM, axis=1),
                                 jnp.pad(dt_cols, ((0, 0), (0, 128 - n_ssd_heads)))], axis=1).astype(BF16)
        w_out_b = w_out[l].astype(BF16)
        ssd_par = (conv_w[l], conv_b[l], dt_bias[l], a_log[l], d_skip[l], g_ssd[l])
        dims = dict(att_w=att_w, ssd_w=ssd_w, conv_dim=conv_dim)

        qt, kaug, vt, kpg, vpg, km, z_p, xbc_p, dte_p, dtc_p = _inproj(
            xp, mod_p[0], mod_p[1], g_mix[l], w_all, prompt_batch=(b, s), **dims)
        kmean = km.reshape(b, s // MOBA_BLOCK, nh, HEAD_DIM).transpose(0, 2, 1, 3)
        att_p = _attn_prompt(slopes, qt, kaug, vt, kmean)
        xbc_p3 = xbc_p.reshape(b, s, conv_dim)
        y_p, h_p = _ssd(xbc_p3, z_p.reshape(b, s, ssd_w), dte_p.reshape(b, s, ssd_w), dtc_p.reshape(b, s, 128),
                        jnp.zeros((b, 8, conv_dim), F32), jnp.zeros((b, ssd_w, n_state), F32), *ssd_par,
                        length=MOBA_BLOCK)
        x1 = _outproj(att_p, y_p.reshape(tp, ssd_w), xp, mod_p[2], g_att[l], w_out_b,
                      jnp.zeros((tp + tsn, d), F32), first_tile=0, prompt_batch=(b, s))
        kp_l.append(jnp.swapaxes(kpg, -1, -2))
        vp_l.append(jnp.swapaxes(vpg, -1, -2))
        cp_l.append(xbc_p3[:, s - (CONV_WIDTH - 1):])
        hp_l.append(h_p.reshape(b, n_ssd_heads, SSD_HEAD_DIM, n_state))

        q_s, k_s, v_s, z_s, xbc_s, dte_s, dtc_s = _inproj(xs, mod_s[0], mod_s[1], g_mix[l], w_all, **dims)
        heads = lambda u: u.reshape(bd, ts, nh, HEAD_DIM).transpose(0, 2, 1, 3)
        qh, kh, vh = heads(q_s), heads(k_s), heads(v_s)
        att_s = _moba_sample(qh, kh, vh, jnp.swapaxes(cache_k[l], -1, -2), jnp.swapaxes(cache_v[l], -1, -2),
                             page_table, slopes)
        att_s = att_s.transpose(0, 2, 1, 3).reshape(tsn, att_w)
        xbc_s3 = xbc_s.reshape(bd, ts, conv_dim)
        conv_prev = jnp.pad(state_conv[l], ((0, 0), (8 - (CONV_WIDTH - 1), 0), (0, 0)))
        y_s, h_s = _ssd(xbc_s3, z_s.reshape(bd, ts, ssd_w), dte_s.reshape(bd, ts, ssd_w), dtc_s.reshape(bd, ts, 128),
                        conv_prev, state_ssm[l].reshape(bd, ssd_w, n_state), *ssd_par, length=128)
        x1 = _outproj(att_s, y_s.reshape(tsn, ssd_w), xs, mod_s[2], g_att[l], w_out_b, x1, first_tile=prompt_tiles)
        ks_l.append(kh)
        vs_l.append(vh)
        cs_l.append(jnp.concatenate([state_conv[l], xbc_s3], axis=1)[:, ts:])
        hs_l.append(h_s.reshape(bd, n_ssd_heads, SSD_HEAD_DIM, n_state))

        t_all, shared, sel_t, wts_t = _moe_pre(x1, mod_u[3], mod_u[4], mod_of_tile, g_ffn[l], w_router[l].astype(BF16),
                                               router_bias[l], ws_gate[l].astype(BF16), ws_up[l].astype(BF16),
                                               ws_down[l].astype(BF16))
        n_all = tp + tsn
        sel = sel_t.transpose(0, 2, 1).reshape(n_all, TOP_K)
        wts = wts_t.transpose(0, 2, 1).reshape(n_all, TOP_K)
        routed = _experts(_routing_plan(sel, wts, n_experts, EXPERT_TILE), t_all.reshape(n_all, 8, 128),
                          w_gate[l], w_up[l], w_down[l]).reshape(TOP_K * n_all * 8, 128)
        last = l == depth - 1
        fin = functools.partial(_combine, x1, routed, shared, mod_u[5], mod_of_tile, g_final, final_norm=last)
        xp = fin(first_tile=0, n_tiles=prompt_tiles)
        xs = fin(first_tile=prompt_tiles, n_tiles=sample_tiles)

    return (xp.reshape(b, s, d), xs.reshape(bd, ts, d), jnp.stack(kp_l), jnp.stack(vp_l), jnp.stack(ks_l),
            jnp.stack(vs_l), jnp.stack(cp_l), jnp.stack(cs_l), jnp.stack(hp_l), jnp.stack(hs_l))
```

```python
import functools

import jax
import jax.numpy as jnp
from jax import lax
from jax.experimental import pallas as pl
from jax.experimental.pallas import tpu as pltpu

F32 = jnp.float32
BF16 = jnp.bfloat16
I32 = jnp.int32

NORM_EPS = 1e-6
HEAD_DIM = 64
MOBA_BLOCK = 256
MOBA_TOPK = 3
PAGE_SIZE = 128
SSD_HEAD_DIM = 64
SSD_GROUPS = 2
SSD_STATE = 128
CONV_WIDTH = 4
N_EXPERT_GROUPS = 8
TOPK_GROUPS = 4
TOP_K = 8
ROUTED_SCALE = 2.5
MASKED = -1e30
TOKEN_TILE = 256
EXPERT_TILE = 256
PAST_UNROLL = 4
V_PAD = 16
VMEM_LIMIT = 56 * 1024 * 1024


def _params(sem, vmem=VMEM_LIMIT):
    return pltpu.CompilerParams(dimension_semantics=sem, vmem_limit_bytes=vmem)


def _silu(x):
    return x * jax.nn.sigmoid(x)


def _softplus(x):
    return jnp.maximum(x, 0.0) + jnp.log1p(jnp.exp(-jnp.abs(x)))


def _rms(x, g):
    return x * lax.rsqrt(jnp.mean(x * x, axis=-1, keepdims=True) + NORM_EPS) * g


def _dot(a, b):
    return jnp.dot(a, b, preferred_element_type=F32)


def _dot_nt(a, b):
    return lax.dot_general(a, b, (((1,), (1,)), ((), ())), preferred_element_type=F32)


def _dot_exact(a, b):
    return jnp.dot(a, b, preferred_element_type=F32, precision=lax.Precision.HIGHEST)


def _adaln_body(c_ref, w_ref, b_ref, o_ref):
    a = _silu(c_ref[...]).astype(BF16)
    o_ref[...] = _dot(a, w_ref[...].astype(BF16)) + b_ref[...]


def _adaln(c, w_ada, b_ada):
    r, d = c.shape
    n = w_ada.shape[1]
    tn = 512
    return pl.pallas_call(
        _adaln_body,
        out_shape=jax.ShapeDtypeStruct((r, n), F32),
        grid=(n // tn,),
        in_specs=[pl.BlockSpec((r, d), lambda j: (0, 0)),
                  pl.BlockSpec((d, tn), lambda j: (0, j)),
                  pl.BlockSpec((1, tn), lambda j: (0, j))],
        out_specs=pl.BlockSpec((r, tn), lambda j: (0, j)),
        compiler_params=_params(("arbitrary",)),
        name="adaln",
    )(c, w_ada, b_ada.reshape(1, n))


def _inproj_body(x_ref, sh_ref, sc_ref, g_ref, w_ref, *outs, att_w, ssd_w, conv_dim, prompt):
    x = x_ref[...]
    h = (_rms(x, g_ref[...]) * (1.0 + sc_ref[0]) + sh_ref[0]).astype(BF16)
    tm = x.shape[0]
    a = att_w
    c0 = 3 * a
    c1 = c0 + ssd_w
    c2 = c1 + conv_dim
    c3 = c2 + ssd_w

    def proj(lo, hi):
        return _dot(h, w_ref[:, lo:hi])

    q, k, v = proj(0, a), proj(a, 2 * a), proj(2 * a, c0)
    if prompt:
        (qt_ref, kaug_ref, vt_ref, kp_ref, vp_ref, km_ref, z_ref, xbc_ref, dte_ref, dtc_ref) = outs
        nh = a // HEAD_DIM
        qt_ref[0] = q.T.reshape(nh, HEAD_DIM, tm)
        kt = k.T.reshape(nh, HEAD_DIM, tm)
        vt = v.T.reshape(nh, HEAD_DIM, tm)
        pad_row = lax.broadcasted_iota(I32, (nh, V_PAD, tm), 1)
        vt_ref[0] = jnp.concatenate([vt, jnp.where(pad_row == 0, 1.0, 0.0)], axis=1).astype(BF16)
        for pg in range(tm // PAGE_SIZE):
            kp_ref[0, pg] = kt[:, :, pg * PAGE_SIZE:(pg + 1) * PAGE_SIZE]
            vp_ref[0, pg] = vt[:, :, pg * PAGE_SIZE:(pg + 1) * PAGE_SIZE]
        km_ref[0] = jnp.mean(k, axis=0, keepdims=True)
        lane = lax.broadcasted_iota(I32, (tm, 2 * HEAD_DIM), 1)
        row = lax.broadcasted_iota(I32, (tm, 2 * HEAD_DIM), 0)
        pos = jnp.where(lane == HEAD_DIM, (row % MOBA_BLOCK).astype(F32), 0.0)
        low = lane < HEAD_DIM
        for hp in range(nh // 2):
            kc = k[:, hp * 128:(hp + 1) * 128]
            kaug_ref[0, 2 * hp] = jnp.where(low, kc, pos).astype(BF16)
            kaug_ref[0, 2 * hp + 1] = jnp.where(low, pltpu.roll(kc, HEAD_DIM, 1), pos).astype(BF16)
    else:
        (q_ref, k_ref, v_ref, z_ref, xbc_ref, dte_ref, dtc_ref) = outs
        q_ref[...] = q
        k_ref[...] = k
        v_ref[...] = v
    z_ref[...] = proj(c0, c1)
    xbc_ref[...] = proj(c1, c2)
    dte_ref[...] = proj(c2, c3)
    dtc_ref[...] = proj(c3, c3 + 128)


def _inproj(x2d, shift, scale, g, w_all, *, att_w, ssd_w, conv_dim, prompt_batch=None):
    t, d = x2d.shape
    tm = TOKEN_TILE
    nw = w_all.shape[1]
    nt = t // tm
    prompt = prompt_batch is not None
    tok = lambda width: pl.BlockSpec((tm, width), lambda i: (i, 0))
    common_shapes = [jax.ShapeDtypeStruct((t, ssd_w), F32), jax.ShapeDtypeStruct((t, conv_dim), F32),
                     jax.ShapeDtypeStruct((t, ssd_w), F32), jax.ShapeDtypeStruct((t, 128), F32)]
    common_specs = [tok(ssd_w), tok(conv_dim), tok(ssd_w), tok(128)]
    if prompt:
        b, s = prompt_batch
        tps = s // tm
        nh = att_w // HEAD_DIM
        ppt = tm // PAGE_SIZE
        mod_map = lambda i: (i // tps, 0, 0)
        out_shape = [jax.ShapeDtypeStruct((b, nh, HEAD_DIM, s), F32),
                     jax.ShapeDtypeStruct((b, nh, s, 2 * HEAD_DIM), BF16),
                     jax.ShapeDtypeStruct((b, nh, HEAD_DIM + V_PAD, s), BF16),
                     jax.ShapeDtypeStruct((b, s // PAGE_SIZE, nh, HEAD_DIM, PAGE_SIZE), F32),
                     jax.ShapeDtypeStruct((b, s // PAGE_SIZE, nh, HEAD_DIM, PAGE_SIZE), F32),
                     jax.ShapeDtypeStruct((nt, 1, att_w), F32)] + common_shapes
        out_specs = [pl.BlockSpec((1, nh, HEAD_DIM, tm), lambda i: (i // tps, 0, 0, i % tps)),
                     pl.BlockSpec((1, nh, tm, 2 * HEAD_DIM), lambda i: (i // tps, 0, i % tps, 0)),
                     pl.BlockSpec((1, nh, HEAD_DIM + V_PAD, tm), lambda i: (i // tps, 0, 0, i % tps)),
                     pl.BlockSpec((1, ppt, nh, HEAD_DIM, PAGE_SIZE), lambda i: (i // tps, i % tps, 0, 0, 0)),
                     pl.BlockSpec((1, ppt, nh, HEAD_DIM, PAGE_SIZE), lambda i: (i // tps, i % tps, 0, 0, 0)),
                     pl.BlockSpec((1, 1, att_w), lambda i: (i, 0, 0))] + common_specs
        mod_block = (1, 1, d)
    else:
        mod_map = lambda i: (i, 0, 0)
        out_shape = [jax.ShapeDtypeStruct((t, att_w), F32)] * 3 + common_shapes
        out_specs = [tok(att_w)] * 3 + common_specs
        mod_block = (1, tm, d)
    body = functools.partial(_inproj_body, att_w=att_w, ssd_w=ssd_w, conv_dim=conv_dim, prompt=prompt)
    return pl.pallas_call(
        body,
        out_shape=out_shape,
        grid=(nt,),
        in_specs=[tok(d),
                  pl.BlockSpec(mod_block, mod_map),
                  pl.BlockSpec(mod_block, mod_map),
                  pl.BlockSpec((1, d), lambda i: (0, 0)),
                  pl.BlockSpec((d, nw), lambda i: (0, 0))],
        out_specs=out_specs,
        compiler_params=_params(("arbitrary",)),
        name="inproj_prompt" if prompt else "inproj_sample",
    )(x2d, shift, scale, g.reshape(1, d), w_all)


def _select_topk_rows(gate, n_valid, k):
    nrow = gate.shape[0]
    row = lax.broadcasted_iota(I32, gate.shape, 0)
    g = jnp.where(row < n_valid, gate, -jnp.inf)
    chosen = jnp.zeros(gate.shape, I32)
    for _ in range(k):
        m = jnp.max(g, axis=0, keepdims=True)
        cand = jnp.logical_and(g == m, g > -jnp.inf)
        first = jnp.min(jnp.where(cand, row, nrow), axis=0, keepdims=True)
        pick = row == first
        chosen = jnp.where(pick, 1, chosen)
        g = jnp.where(pick, -jnp.inf, g)
    return chosen


def _attn_prompt_body(slope_ref, qt_ref, k_ref, vt_ref, km_ref, o_ref, bias_ref, s0_ref, s1_ref):
    h = pl.program_id(1)
    t = pl.program_id(2)
    slope = slope_ref[h]
    blk = MOBA_BLOCK
    qt = qt_ref[0, 0]
    gate = _dot_exact(km_ref[0, 0], qt)
    chosen = _select_topk_rows(gate, t, MOBA_TOPK)
    jrow = lax.broadcasted_iota(I32, gate.shape, 0)
    iq = lax.broadcasted_iota(I32, gate.shape, 1)
    dist0 = ((t - jrow) * blk + iq).astype(F32)
    bias_ref[...] = jnp.where(chosen > 0, -slope * dist0, MASKED)

    r2 = lax.broadcasted_iota(I32, qt.shape, 0)
    extra = jnp.where(r2 == 0, slope, 0.0)
    qaug = jnp.concatenate([qt * (HEAD_DIM ** -0.5), extra], axis=0).astype(BF16)

    start = pl.multiple_of(t * blk, blk)
    s = _dot(k_ref[0, 0, pl.ds(start, blk), :], qaug)
    ik = lax.broadcasted_iota(I32, s.shape, 0)
    iqq = lax.broadcasted_iota(I32, s.shape, 1)
    s = jnp.where(ik <= iqq, s - slope * iqq.astype(F32), MASKED)
    m = jnp.max(s, axis=0, keepdims=True)
    p = jnp.exp(s - m)
    acc = _dot(vt_ref[0, 0, :, pl.ds(start, blk)], p.astype(BF16))

    nb = bias_ref.shape[0]

    def block_start(trip, u):
        j = jnp.minimum(trip * PAST_UNROLL + u, nb - 1)
        return j, pl.multiple_of(j * blk, blk)

    def scores(trip, dst):
        for u in range(PAST_UNROLL):
            j, st = block_start(trip, u)
            dst[u] = _dot(k_ref[0, 0, pl.ds(st, blk), :], qaug) + bias_ref[pl.ds(j, 1), :]

    def consume(trip, src, carry):
        m, acc = carry
        ss = [src[u] for u in range(PAST_UNROLL)]
        m_new = m
        for s_u in ss:
            m_new = jnp.maximum(m_new, jnp.max(s_u, axis=0, keepdims=True))
        acc = jnp.exp(m - m_new) * acc
        for u in range(PAST_UNROLL):
            _, st = block_start(trip, u)
            p = jnp.exp(ss[u] - m_new)
            acc = acc + _dot(vt_ref[0, 0, :, pl.ds(st, blk)], p.astype(BF16))
        return m_new, acc

    def body(i, carry):
        scores(2 * i + 1, s1_ref)
        carry = consume(2 * i, s0_ref, carry)
        scores(2 * i + 2, s0_ref)
        return consume(2 * i + 1, s1_ref, carry)

    scores(0, s0_ref)
    n_trips = (t + PAST_UNROLL - 1) // PAST_UNROLL
    m, acc = lax.fori_loop(0, (n_trips + 1) // 2, body, (m, acc))
    o_ref[0, 0] = acc[:HEAD_DIM] / acc[HEAD_DIM:HEAD_DIM + 1]


def _attn_prompt(slopes, qt, kaug, vt, kmean):
    b, nh, dh, s = qt.shape
    vrows = vt.shape[2]
    blk = MOBA_BLOCK
    nb = s // blk
    trip_scores = pltpu.VMEM((PAST_UNROLL, blk, blk), F32)
    return pl.pallas_call(
        _attn_prompt_body,
        out_shape=jax.ShapeDtypeStruct((b, nh, dh, s), F32),
        grid_spec=pltpu.PrefetchScalarGridSpec(
            num_scalar_prefetch=1,
            grid=(b, nh, nb),
            in_specs=[pl.BlockSpec((1, 1, dh, blk), lambda bi, hi, ti, sl: (bi, hi, 0, ti)),
                      pl.BlockSpec((1, 1, s, 2 * dh), lambda bi, hi, ti, sl: (bi, hi, 0, 0)),
                      pl.BlockSpec((1, 1, vrows, s), lambda bi, hi, ti, sl: (bi, hi, 0, 0)),
                      pl.BlockSpec((1, 1, nb, dh), lambda bi, hi, ti, sl: (bi, hi, 0, 0))],
            out_specs=pl.BlockSpec((1, 1, dh, blk), lambda bi, hi, ti, sl: (bi, hi, 0, ti)),
            scratch_shapes=[pltpu.VMEM((nb, blk), F32), trip_scores, trip_scores]),
        compiler_params=_params(("arbitrary", "arbitrary", "arbitrary")),
        name="moba_prompt",
    )(slopes, qt, kaug, vt, kmean)


def _ssd_body(xbc_ref, z_ref, dte_ref, dtc_ref, cprev_ref, h0_ref, cw_ref, cb_ref, dtbe_ref, dtbc_ref,
              ale_ref, alc_ref, dsk_ref, g_ref, y_ref, hout_ref, buf_ref, st_ref, *, rows, length):
    c = pl.program_id(1)
    ng = SSD_GROUPS
    n = SSD_STATE
    width = z_ref.shape[-1]
    gw = width // ng

    @pl.when(c == 0)
    def _():
        if rows < length:
            buf_ref[...] = jnp.zeros(buf_ref.shape, F32)
        buf_ref[0:8, :] = cprev_ref[0]
        for g in range(ng):
            st_ref[g] = h0_ref[0, g * gw:(g + 1) * gw, :].T

    buf_ref[8:8 + rows, :] = xbc_ref[0]
    acc = cb_ref[...]
    for i in range(CONV_WIDTH):
        off = 8 - (CONV_WIDTH - 1) + i
        acc = acc + buf_ref[off:off + length, :] * cw_ref[i:i + 1, :]
    if rows == length:
        buf_ref[0:8, :] = buf_ref[length:length + 8, :]
    xc = _silu(acc)
    xs = xc[:, :width]
    bm = xc[:, width:width + ng * n]
    cm = xc[:, width + ng * n:]

    if rows < length:
        dte_full = jnp.concatenate([dte_ref[0], jnp.zeros((length - rows, width), F32)], axis=0)
        dtc_full = jnp.concatenate([dtc_ref[0], jnp.zeros((length - rows, 128), F32)], axis=0)
        live_e = lax.broadcasted_iota(I32, (length, width), 0) < rows
        live_c = lax.broadcasted_iota(I32, (length, 128), 0) < rows
        dt = jnp.where(live_e, _softplus(dte_full + dtbe_ref[...]), 0.0)
        dtc = jnp.where(live_c, _softplus(dtc_full + dtbc_ref[...]), 0.0)
    else:
        dt = _softplus(dte_ref[0] + dtbe_ref[...])
        dtc = _softplus(dtc_ref[0] + dtbc_ref[...])
    da = dt * (-jnp.exp(ale_ref[...]))
    dac = dtc * (-jnp.exp(alc_ref[...]))
    xdt = xs * dt

    ti = lax.broadcasted_iota(I32, (length, length), 0)
    si = lax.broadcasted_iota(I32, (length, length), 1)
    causal = ti >= si
    tri = jnp.where(causal, 1.0, 0.0)
    acs = _dot_exact(tri, da)
    acs_t = _dot_exact(tri, dac).T
    acs_last = acs[length - 1:length, :]
    e_acs = jnp.exp(acs)
    w_tail = jnp.exp(acs_last - acs)
    lane = lax.broadcasted_iota(I32, (length, 128), 1)

    ys = []
    for g in range(ng):
        bg = bm[:, g * n:(g + 1) * n]
        cg = cm[:, g * n:(g + 1) * n].astype(BF16)
        cb = _dot_nt(cg, bg.astype(BF16))
        st = st_ref[g]
        yg = _dot(cg, st.astype(BF16)) * e_acs[:, g * gw:(g + 1) * gw]
        pieces = []
        for pr in range(gw // 128):
            x_pair = xdt[:, g * gw + pr * 128: g * gw + (pr + 1) * 128].astype(BF16)
            halves = []
            for hh in range(2):
                head = (g * gw + pr * 128) // SSD_HEAD_DIM + hh
                col = acs[:, head * SSD_HEAD_DIM: head * SSD_HEAD_DIM + 1]
                rowv = acs_t[head:head + 1, :]
                decay = jnp.where(causal, jnp.exp(col - rowv), 0.0)
                halves.append(_dot((cb * decay).astype(BF16), x_pair))
            pieces.append(jnp.where(lane < SSD_HEAD_DIM, halves[0], halves[1]))
        ys.append(yg + jnp.concatenate(pieces, axis=1))
        xw = (xdt[:, g * gw:(g + 1) * gw] * w_tail[:, g * gw:(g + 1) * gw]).astype(BF16)
        st_ref[g] = st * jnp.exp(acs_last[:, g * gw:(g + 1) * gw]) + _dot(bg.T.astype(BF16), xw)
    y = jnp.concatenate(ys, axis=1)
    y = y + dsk_ref[...] * xs
    y = y[:rows] * _silu(z_ref[0])
    outs = []
    for g in range(ng):
        outs.append(_rms(y[:, g * gw:(g + 1) * gw], g_ref[:, g * gw:(g + 1) * gw]))
    y_ref[0] = jnp.concatenate(outs, axis=1)

    @pl.when(c == pl.num_programs(1) - 1)
    def _():
        for g in range(ng):
            hout_ref[0, g * gw:(g + 1) * gw, :] = st_ref[g].T


def _ssd(xbc, z, dte, dtc, conv_prev, h0, conv_w, conv_b, dt_bias, a_log, d_skip, g_ssd, *, length):
    bn, lt, cdim = xbc.shape
    width = z.shape[-1]
    n = h0.shape[-1]
    rows = min(lt, length)
    nc = lt // rows
    rep = width // dt_bias.shape[0]
    expand = lambda u: jnp.repeat(u, rep).reshape(1, width)
    lane_pad = lambda u: jnp.pad(u, (0, 128 - u.shape[0])).reshape(1, 128)
    seq = lambda w_: pl.BlockSpec((1, rows, w_), lambda b, c: (b, c, 0))
    const = lambda shape: pl.BlockSpec(shape, lambda b, c: (0,) * len(shape))
    body = functools.partial(_ssd_body, rows=rows, length=length)
    return pl.pallas_call(
        body,
        out_shape=[jax.ShapeDtypeStruct((bn, lt, width), F32), jax.ShapeDtypeStruct((bn, width, n), F32)],
        grid=(bn, nc),
        in_specs=[seq(cdim), seq(width), seq(width), seq(128),
                  pl.BlockSpec((1, 8, cdim), lambda b, c: (b, 0, 0)),
                  pl.BlockSpec((1, width, n), lambda b, c: (b, 0, 0)),
                  const((CONV_WIDTH, cdim)), const((1, cdim)), const((1, width)), const((1, 128)),
                  const((1, width)), const((1, 128)), const((1, width)), const((1, width))],
        out_specs=[seq(width), pl.BlockSpec((1, width, n), lambda b, c: (b, 0, 0))],
        scratch_shapes=[pltpu.VMEM((length + 8, cdim), F32),
                        pltpu.VMEM((SSD_GROUPS, n, width // SSD_GROUPS), F32)],
        compiler_params=_params(("arbitrary", "arbitrary")),
        name="ssd_scan",
    )(xbc, z, dte, dtc, conv_prev, h0, conv_w, conv_b.reshape(1, cdim), expand(dt_bias), lane_pad(dt_bias),
      expand(a_log), lane_pad(a_log), expand(d_skip), g_ssd.reshape(1, width))


def _outproj_body(att_ref, ssd_ref, x_ref, gt_ref, ga_ref, w_ref, *rest, transposed):
    o_ref = rest[-1]
    if transposed:
        a = att_ref[0].reshape(-1, att_ref.shape[-1]).T
    else:
        a = att_ref[...]
    mix = jnp.concatenate([_rms(a, ga_ref[...]), ssd_ref[...]], axis=-1).astype(BF16)
    o_ref[...] = x_ref[...] + gt_ref[0] * _dot(mix, w_ref[...])


def _outproj(att, ssd, x2d, gate, g_att, w_out, unified, *, first_tile, prompt_batch=None):
    t, d = x2d.shape
    tm = TOKEN_TILE
    tok = lambda width: pl.BlockSpec((tm, width), lambda i: (i, 0))
    sw = ssd.shape[-1]
    if prompt_batch is not None:
        b, s = prompt_batch
        tps = s // tm
        nh, dh = att.shape[1], att.shape[2]
        aw = nh * dh
        att_spec = pl.BlockSpec((1, nh, dh, tm), lambda i: (i // tps, 0, 0, i % tps))
        gate_spec = pl.BlockSpec((1, 1, d), lambda i: (i // tps, 0, 0))
    else:
        aw = att.shape[-1]
        att_spec = tok(aw)
        gate_spec = pl.BlockSpec((1, tm, d), lambda i: (i, 0, 0))
    return pl.pallas_call(
        functools.partial(_outproj_body, transposed=prompt_batch is not None),
        out_shape=jax.ShapeDtypeStruct(unified.shape, F32),
        grid=(t // tm,),
        in_specs=[att_spec, tok(sw), tok(d), gate_spec,
                  pl.BlockSpec((1, aw), lambda i: (0, 0)),
                  pl.BlockSpec((aw + sw, d), lambda i: (0, 0)),
                  pl.BlockSpec(memory_space=pl.ANY)],
        out_specs=pl.BlockSpec((tm, d), lambda i: (i + first_tile, 0)),
        input_output_aliases={6: 0},
        compiler_params=_params(("arbitrary",)),
        name="outproj",
    )(att, ssd, x2d, gate, g_att.reshape(1, aw), w_out, unified)


def _route_t(st, sbt):
    e, tm = st.shape
    per = e // N_EXPERT_GROUPS
    row = lax.broadcasted_iota(I32, (e, tm), 0)
    rg = lax.broadcasted_iota(I32, (per, tm), 0)
    gs = []
    for g in range(N_EXPERT_GROUPS):
        blk = sbt[g * per:(g + 1) * per]
        m1 = jnp.max(blk, axis=0, keepdims=True)
        first = jnp.min(jnp.where(blk == m1, rg, per), axis=0, keepdims=True)
        m2 = jnp.max(jnp.where(rg == first, -jnp.inf, blk), axis=0, keepdims=True)
        gs.append(m1 + m2)
    gscore = jnp.concatenate(gs, axis=0)
    gch = _select_topk_rows(gscore, N_EXPERT_GROUPS, TOPK_GROUPS)
    emask = jnp.concatenate([jnp.broadcast_to(gch[g:g + 1], (per, tm)) for g in range(N_EXPERT_GROUPS)], axis=0)
    cur = jnp.where(emask > 0, sbt, -jnp.inf)
    sel, wts = [], []
    for _ in range(TOP_K):
        m = jnp.max(cur, axis=0, keepdims=True)
        first = jnp.min(jnp.where(cur == m, row, e), axis=0, keepdims=True)
        pick = row == first
        sel.append(first)
        wts.append(jnp.sum(jnp.where(pick, st, 0.0), axis=0, keepdims=True))
        cur = jnp.where(pick, -jnp.inf, cur)
    sel = jnp.concatenate(sel, axis=0)
    w = jnp.concatenate(wts, axis=0)
    return sel, w / jnp.sum(w, axis=0, keepdims=True) * ROUTED_SCALE


def _rows_to_tiles(rows, ref, first=0):
    n = rows.shape[0]
    for s in range(8):
        ref[pl.ds(first * 8 + s, n, stride=8), :] = rows[:, s * 128:(s + 1) * 128]


def _tiles_to_rows(ref, n, first=0):
    return jnp.concatenate([ref[pl.ds(first * 8 + s, n, stride=8), :] for s in range(8)], axis=1)


def _moe_pre_body(x_ref, sh_ref, sc_ref, g_ref, wr_ref, rb_ref, wsg_ref, wsu_ref, wsd_ref,
                  t_ref, shared_ref, sel_ref, wt_ref):
    t = _rms(x_ref[...], g_ref[...]) * (1.0 + sc_ref[0]) + sh_ref[0]
    _rows_to_tiles(t, t_ref)
    tb = t.astype(BF16)
    st = jax.nn.sigmoid(_dot(tb, wr_ref[...])).T
    sel, w = _route_t(st, st + rb_ref[...])
    sel_ref[0] = sel
    wt_ref[0] = w
    hs = _silu(_dot(tb, wsg_ref[...])) * _dot(tb, wsu_ref[...])
    shared_ref[...] = _dot(hs.astype(BF16), wsd_ref[...])


def _moe_pre(x2d, shift, scale, mod_of_tile, g, w_router, router_bias, ws_gate, ws_up, ws_down):
    t, d = x2d.shape
    assert d == 8 * 128
    tm = TOKEN_TILE
    nt = t // tm
    e = w_router.shape[1]
    sd = ws_gate.shape[1]
    tok = lambda width: pl.BlockSpec((tm, width), lambda i: (i, 0))
    mod = pl.BlockSpec((1, tm, d), lambda i: (mod_of_tile(i), 0, 0))
    full = lambda shape: pl.BlockSpec(shape, lambda i: (0,) * len(shape))
    return pl.pallas_call(
        _moe_pre_body,
        out_shape=[jax.ShapeDtypeStruct((t * 8, 128), F32), jax.ShapeDtypeStruct((t, d), F32),
                   jax.ShapeDtypeStruct((nt, TOP_K, tm), I32), jax.ShapeDtypeStruct((nt, TOP_K, tm), F32)],
        grid=(nt,),
        in_specs=[tok(d), mod, mod, full((1, d)), full((d, e)), full((e, 1)),
                  full((d, sd)), full((d, sd)), full((sd, d))],
        out_specs=[pl.BlockSpec((tm * 8, 128), lambda i: (i, 0)), tok(d),
                   pl.BlockSpec((1, TOP_K, tm), lambda i: (i, 0, 0)),
                   pl.BlockSpec((1, TOP_K, tm), lambda i: (i, 0, 0))],
        compiler_params=_params(("arbitrary",)),
        name="moe_pre",
    )(x2d, shift, scale, g.reshape(1, d), w_router, router_bias.reshape(e, 1), ws_gate, ws_up, ws_down)


SPARE_SLOT = 2
DMA_PHASES = 8


def _experts_body(tile_ref, exp_ref, lo_ref, hi_ref, gslot_ref, qslot_ref, n_ref,
                  tok0_ref, tokn_ref, dstq_ref, dstl_ref, roww_ref, t_hbm, wg_ref, wu_ref, wd_ref,
                  out_hbm, xbuf, ybuf, wgb, wub, wdb, gsem, ssem):
    s = pl.program_id(0)
    n = n_ref[0]
    rows = xbuf.shape[0] // 24

    def tile_of(buf, base, r):
        return buf.at[pl.ds(pl.multiple_of(base + r * 8, 8), 8)]

    def slot_done(buf, sem):
        whole = buf.at[pl.ds(0, rows * 8)]
        pltpu.make_async_copy(whole, whole, sem).wait()

    def gather_start(idx_ref, base, r):
        pltpu.make_async_copy(t_hbm.at[idx_ref[0, 0, r]], tile_of(xbuf, base, r), gsem).start()

    def scatter_start(idx_ref, base, r):
        pltpu.make_async_copy(tile_of(ybuf, base, r), out_hbm.at[idx_ref[0, 0, r]], ssem).start()

    @pl.when(s == 0)
    def _():
        ybuf[...] = jnp.zeros(ybuf.shape, F32)
        for r in range(rows):
            gather_start(tok0_ref, 0, r)
        slot_done(xbuf, gsem)

    @pl.when(s < n)
    def _():
        changed = jnp.logical_or(s == 0, exp_ref[s] != exp_ref[jnp.maximum(s - 1, 0)])

        @pl.when(changed)
        def _():
            wgb[...] = wg_ref[0].astype(BF16)
            wub[...] = wu_ref[0].astype(BF16)
            wdb[...] = wd_ref[0].astype(BF16)

        gbase = pl.multiple_of(gslot_ref[s] * (rows * 8), 8)
        qbase = pl.multiple_of(qslot_ref[s] * (rows * 8), 8)
        per = rows // DMA_PHASES

        def issue(phase):
            for r in range(phase * per, (phase + 1) * per):
                gather_start(tokn_ref, gbase, r)
                scatter_start(dstq_ref, qbase, r)

        base = (tile_ref[s] % 2) * rows
        issue(0)
        x = _tiles_to_rows(xbuf, rows, base).astype(BF16)
        issue(1)
        hg = _dot(x, wgb[...])
        issue(2)
        hu = _dot(x, wub[...])
        issue(3)
        hmid = (_silu(hg) * hu).astype(BF16)
        r_i = lax.broadcasted_iota(I32, (rows, 128), 0)
        mine = jnp.logical_and(r_i >= lo_ref[s], r_i < hi_ref[s])
        w_row = roww_ref[0]
        n_chunks = DMA_PHASES - 4
        cols = wdb.shape[1] // n_chunks
        for c in range(n_chunks):
            issue(4 + c)
            yc = _dot(hmid, wdb[:, c * cols:(c + 1) * cols]) * w_row
            for k in range(cols // 128):
                idx = pl.ds(base * 8 + c * (cols // 128) + k, rows, stride=8)
                ybuf[idx, :] = jnp.where(mine, yc[:, k * 128:(k + 1) * 128], ybuf[idx, :])
        slot_done(xbuf, gsem)
        slot_done(ybuf, ssem)

        @pl.when(s == n - 1)
        def _():
            for r in range(rows):
                scatter_start(dstl_ref, base * 8, r)
            slot_done(ybuf, ssem)


def _experts(plan, t3, w_gate, w_up, w_down):
    item_tile, item_e, item_lo, item_hi, gslot, qslot, n_items, row_tok, row_dst, step_dst, row_w = plan
    n_tiles, _, tile = row_tok.shape
    e, d, ed = w_gate.shape
    assert tile % DMA_PHASES == 0 and d % ((DMA_PHASES - 4) * 128) == 0
    n_steps = item_tile.shape[0]
    on = lambda f: (lambda s, tl, ex, lo, hi, gs, qs, n: f(s, tl, ex))
    smem_blk = lambda f: pl.BlockSpec((1, 1, tile), on(f), memory_space=pltpu.SMEM)
    own_tile = lambda s, tl, ex: (tl[s], 0, 0)
    own_expert = on(lambda s, tl, ex: (ex[s], 0, 0))
    return pl.pallas_call(
        _experts_body,
        out_shape=jax.ShapeDtypeStruct((n_tiles * tile + tile, 8, 128), F32),
        grid_spec=pltpu.PrefetchScalarGridSpec(
            num_scalar_prefetch=7,
            grid=(n_steps,),
            in_specs=[smem_blk(lambda s, tl, ex: (tl[0], 0, 0)),
                      smem_blk(lambda s, tl, ex: (tl[jnp.minimum(s + 1, n_steps - 1)], 0, 0)),
                      smem_blk(lambda s, tl, ex: (s, 0, 0)),
                      smem_blk(own_tile),
                      pl.BlockSpec((1, tile, 1), on(own_tile)),
                      pl.BlockSpec(memory_space=pl.ANY),
                      pl.BlockSpec((1, d, ed), own_expert),
                      pl.BlockSpec((1, d, ed), own_expert),
                      pl.BlockSpec((1, ed, d), own_expert)],
            out_specs=pl.BlockSpec(memory_space=pl.ANY),
            scratch_shapes=[pltpu.VMEM((3 * tile * 8, 128), F32), pltpu.VMEM((3 * tile * 8, 128), F32),
                            pltpu.VMEM((d, ed), BF16), pltpu.VMEM((d, ed), BF16), pltpu.VMEM((ed, d), BF16),
                            pltpu.SemaphoreType.DMA(()), pltpu.SemaphoreType.DMA(())]),
        compiler_params=_params(("arbitrary",)),
        name="moe_experts",
    )(item_tile, item_e, item_lo, item_hi, gslot, qslot, n_items, row_tok, row_tok, step_dst, row_dst,
      row_w.reshape(n_tiles, tile, 1), t3, w_gate, w_up, w_down)


def _routing_plan(sel, wts, n_experts, tile):
    n_tok, k = sel.shape
    tk = n_tok * k
    assert tk % tile == 0
    n_tiles = tk // tile
    e_sorted, order, w_sorted = lax.sort((sel.reshape(-1), jnp.arange(tk, dtype=I32), wts.reshape(-1)), num_keys=1)
    bounds = jnp.searchsorted(e_sorted, jnp.arange(n_experts + 1, dtype=I32), side='left').astype(I32)
    start, end = bounds[:-1], bounds[1:]
    first_tile = start // tile
    n_items_e = jnp.where(end > start, (end - 1) // tile - first_tile + 1, 0)
    item_end = jnp.cumsum(n_items_e)
    n_items = item_end[-1]
    w_idx = jnp.arange(n_tiles + n_experts + 1, dtype=I32)
    live = w_idx < n_items
    item_e = jnp.minimum(jnp.searchsorted(item_end, w_idx, side='right', method='compare_all'),
                         n_experts - 1).astype(I32)
    item_tile = first_tile[item_e] + w_idx - (item_end - n_items_e)[item_e]
    item_lo = jnp.maximum(start[item_e], item_tile * tile) - item_tile * tile
    item_hi = jnp.minimum(end[item_e], (item_tile + 1) * tile) - item_tile * tile
    item_e = jnp.where(live, item_e, e_sorted[-1]).astype(I32)
    item_tile = jnp.where(live, item_tile, n_tiles - 1).astype(I32)
    shape = (n_tiles, 1, tile)
    row_tok = (order // k).reshape(shape)
    row_dst = ((order % k) * n_tok + order // k).reshape(shape)
    prev_tile = jnp.concatenate([item_tile[:1], item_tile[:-1]])
    next_tile = jnp.concatenate([item_tile[1:], item_tile[-1:]])
    opens = jnp.logical_or(w_idx == 0, item_tile != prev_tile)
    closes = jnp.logical_or(w_idx == n_items - 1, next_tile != item_tile)
    opens_next = jnp.concatenate([opens[1:], opens[-1:]])
    closes_prev = jnp.concatenate([closes[:1], closes[:-1]])
    gslot = jnp.where(jnp.logical_and(w_idx + 1 < n_items, opens_next), next_tile % 2, SPARE_SLOT).astype(I32)
    scatters = jnp.logical_and(w_idx >= 1, closes_prev)
    qslot = jnp.where(scatters, prev_tile % 2, SPARE_SLOT).astype(I32)
    spare_rows = tk + jnp.arange(tile, dtype=I32)
    step_dst = jnp.where(scatters[:, None], row_dst[:, 0, :][prev_tile], spare_rows[None, :])[:, None, :]
    return (item_tile, item_e, item_lo.astype(I32), item_hi.astype(I32), gslot, qslot,
            n_items.astype(I32).reshape(1), row_tok, row_dst, step_dst.astype(I32), w_sorted.reshape(shape))


def _combine_body(x_ref, *refs, final_norm):
    r_refs = refs[:TOP_K]
    shared_ref, gt_ref, gf_ref, o_ref, acc_ref = refs[TOP_K:]
    acc = r_refs[0][...]
    for r_ref in r_refs[1:]:
        acc = acc + r_ref[...]
    acc_ref[...] = acc
    y = x_ref[...] + gt_ref[0] * (_tiles_to_rows(acc_ref, x_ref.shape[0]) + shared_ref[...])
    o_ref[...] = _rms(y, gf_ref[...]) if final_norm else y


def _combine(x2d, routed, shared, gate, mod_of_tile, g_final, *, first_tile, n_tiles, final_norm):
    t, d = x2d.shape
    tm = TOKEN_TILE
    tiles_all = t // tm
    tok = pl.BlockSpec((tm, d), lambda i: (i + first_tile, 0))
    slot = lambda k: pl.BlockSpec((tm * 8, 128), lambda i: (i + first_tile + k * tiles_all, 0))
    return pl.pallas_call(
        functools.partial(_combine_body, final_norm=final_norm),
        out_shape=jax.ShapeDtypeStruct((n_tiles * tm, d), F32),
        grid=(n_tiles,),
        in_specs=[tok] + [slot(k) for k in range(TOP_K)] + [
            tok,
            pl.BlockSpec((1, tm, d), lambda i: (mod_of_tile(i + first_tile), 0, 0)),
            pl.BlockSpec((1, d), lambda i: (0, 0))],
        out_specs=pl.BlockSpec((tm, d), lambda i: (i, 0)),
        scratch_shapes=[pltpu.VMEM((tm * 8, 128), F32)],
        compiler_params=_params(("arbitrary",)),
        name="moe_combine",
    )(x2d, *([routed] * TOP_K), shared, gate, g_final.reshape(1, d))


SELECT_PAGES = 16


def _sample_select_body(pt_ref, *refs, n_sel):
    pages = refs[:SELECT_PAGES]
    qt_ref, sel_ref, gate_ref = refs[SELECT_PAGES:]
    c = pl.program_id(1)
    ppb = MOBA_BLOCK // PAGE_SIZE
    nh = qt_ref.shape[1]
    for n in range(SELECT_PAGES // ppb):
        tot = pages[n * ppb][0]
        for i in range(1, ppb):
            tot = tot + pages[n * ppb + i][0]
        ksum = jnp.sum(tot, axis=2, keepdims=True)
        gate_ref[:, pl.ds(c * (SELECT_PAGES // ppb) + n, 1), :] = jnp.sum(qt_ref[0] * ksum, axis=1, keepdims=True)

    @pl.when(c == pl.num_programs(1) - 1)
    def _():
        nblk = gate_ref.shape[1]
        for h in range(nh):
            gate = gate_ref[h]
            row = lax.broadcasted_iota(I32, gate.shape, 0)
            g = gate
            picks = []
            for _ in range(n_sel):
                m = jnp.max(g, axis=0, keepdims=True)
                first = jnp.min(jnp.where(g == m, row, nblk), axis=0, keepdims=True)
                picks.append(first)
                g = jnp.where(row == first, -jnp.inf, g)
            picks.append(jnp.zeros((8 - n_sel, gate.shape[1]), I32))
            sel_ref[0, h] = jnp.concatenate(picks, axis=0)


def _sample_select(page_table, cache_kt, qt, n_sel):
    bd, n_pages = page_table.shape
    _, nh, dh, page = cache_kt.shape
    tpad = qt.shape[3]
    nblk = n_pages * PAGE_SIZE // MOBA_BLOCK
    page_spec = lambda i: pl.BlockSpec((1, nh, dh, page),
                                       lambda b, c, pt: (pt[b, c * SELECT_PAGES + i], 0, 0, 0))
    return pl.pallas_call(
        functools.partial(_sample_select_body, n_sel=n_sel),
        out_shape=jax.ShapeDtypeStruct((bd, nh, 8, tpad), I32),
        grid_spec=pltpu.PrefetchScalarGridSpec(
            num_scalar_prefetch=1,
            grid=(bd, n_pages // SELECT_PAGES),
            in_specs=[page_spec(i) for i in range(SELECT_PAGES)]
                     + [pl.BlockSpec((1, nh, dh, tpad), lambda b, c, pt: (b, 0, 0, 0))],
            out_specs=pl.BlockSpec((1, nh, 8, tpad), lambda b, c, pt: (b, 0, 0, 0)),
            scratch_shapes=[pltpu.VMEM((nh, nblk, tpad), F32)]),
        compiler_params=_params(("arbitrary", "arbitrary")),
        name="moba_sample_select",
    )(page_table, *([cache_kt] * SELECT_PAGES), qt)


def _sample_attend_body(pt_ref, selb_ref, slope_ref, kt_hbm, vt_hbm, q_ref, kn_ref, vn_ref, o_ref,
                        kbuf, vbuf, ksem, vsem, *, n_tok, n_sel, past):
    ppb = MOBA_BLOCK // PAGE_SIZE
    per_tok = n_sel * ppb
    nkv = n_tok * per_tok
    b = pl.program_id(0)
    h = pl.program_id(1)
    nh = q_ref.shape[1]
    step = b * nh + h
    n_steps = pl.num_programs(0) * nh

    def fetch(stp, slot):
        hh = stp % nh
        bb = stp // nh
        for i in range(nkv):
            logical_block = selb_ref[stp * (nkv // ppb) + i // ppb]
            pg = pt_ref[bb, logical_block * ppb + i % ppb]
            pltpu.make_async_copy(kt_hbm.at[pg, hh], kbuf.at[slot, i], ksem.at[slot]).start()
            pltpu.make_async_copy(vt_hbm.at[pg, hh], vbuf.at[slot, i], vsem.at[slot]).start()

    @pl.when(step == 0)
    def _():
        fetch(0, 0)

    @pl.when(step + 1 < n_steps)
    def _():
        fetch(step + 1, (step + 1) % 2)

    slot = step % 2
    pltpu.make_async_copy(kbuf.at[slot], kbuf.at[slot], ksem.at[slot]).wait()
    pltpu.make_async_copy(vbuf.at[slot], vbuf.at[slot], vsem.at[slot]).wait()

    slope = slope_ref[h]
    q = (q_ref[0, h] * (HEAD_DIM ** -0.5)).astype(BF16)
    tpad = q.shape[0]
    ncol = per_tok * PAGE_SIZE
    s_new = _dot_nt(q, kn_ref[0, h].astype(BF16))
    ti = lax.broadcasted_iota(I32, s_new.shape, 0)
    oi = lax.broadcasted_iota(I32, s_new.shape, 1)
    s_new = jnp.where(jnp.logical_and(oi <= ti, oi < n_tok), s_new - slope * (ti - oi).astype(F32), MASKED)
    vn = vn_ref[0, h].astype(BF16)
    col = lax.broadcasted_iota(I32, (tpad, ncol), 1)
    rowi = lax.broadcasted_iota(I32, (tpad, ncol), 0)
    outs = []
    for t in range(n_tok):
        kt = jnp.concatenate([kbuf[slot, t * per_tok + i] for i in range(per_tok)], axis=1).astype(BF16)
        vt = jnp.concatenate([vbuf[slot, t * per_tok + i] for i in range(per_tok)], axis=1).astype(BF16)
        s = _dot(q, kt)
        blk_of_col = jnp.zeros((tpad, ncol), I32)
        for r in range(n_sel):
            sb = selb_ref[(step * n_tok + t) * n_sel + r]
            blk_of_col = jnp.where(col // MOBA_BLOCK == r, sb, blk_of_col)
        spos = blk_of_col * MOBA_BLOCK + col % MOBA_BLOCK
        s = jnp.where(rowi == t, s - slope * (past + t - spos).astype(F32), MASKED)
        so = s_new[t:t + 1]
        m = jnp.maximum(jnp.max(jnp.max(s, axis=1, keepdims=True), axis=0, keepdims=True),
                        jnp.max(so, axis=1, keepdims=True))
        p = jnp.exp(s - m)
        po = jnp.exp(so - m)
        l = jnp.sum(jnp.sum(p, axis=1, keepdims=True), axis=0, keepdims=True) + jnp.sum(po, axis=1, keepdims=True)
        o_sel = _dot_nt(p.astype(BF16), vt)[t:t + 1]
        o_new = _dot(jnp.broadcast_to(po, (tpad, tpad)).astype(BF16), vn)[0:1]
        outs.append((o_sel + o_new) / l)
    outs.append(jnp.zeros((tpad - n_tok, HEAD_DIM), F32))
    o_ref[0, h] = jnp.concatenate(outs, axis=0)


def _sample_attend(page_table, selb, slopes, cache_kt, cache_vt, q, k_new, v_new, *, n_tok, n_sel, past):
    bd, nh, tpad, dh = q.shape
    page = cache_kt.shape[3]
    ppb = MOBA_BLOCK // PAGE_SIZE
    nkv = n_tok * n_sel * ppb
    pool = pl.BlockSpec(memory_space=pl.ANY)
    seq_spec = pl.BlockSpec((1, nh, tpad, dh), lambda b, h, ph, sb, sl: (b, 0, 0, 0))
    slices = pltpu.VMEM((2, nkv, dh, page), F32)
    return pl.pallas_call(
        functools.partial(_sample_attend_body, n_tok=n_tok, n_sel=n_sel, past=past),
        out_shape=jax.ShapeDtypeStruct((bd, nh, tpad, dh), F32),
        grid_spec=pltpu.PrefetchScalarGridSpec(
            num_scalar_prefetch=3,
            grid=(bd, nh),
            in_specs=[pool, pool, seq_spec, seq_spec, seq_spec],
            out_specs=seq_spec,
            scratch_shapes=[slices, slices, pltpu.SemaphoreType.DMA((2,)), pltpu.SemaphoreType.DMA((2,))]),
        compiler_params=_params(("arbitrary", "arbitrary")),
        name="moba_sample_attend",
    )(page_table, selb, slopes, cache_kt, cache_vt, q, k_new, v_new)


def _moba_sample(q, k_new, v_new, cache_kt, cache_vt, page_table, slopes):
    bd, nh, n_tok, dh = q.shape
    n_pages = page_table.shape[1]
    past = n_pages * PAGE_SIZE
    assert past % MOBA_BLOCK == 0 and n_pages % SELECT_PAGES == 0 and n_tok <= 8
    n_sel = min(MOBA_TOPK, past // MOBA_BLOCK)
    assert n_sel > 0
    tpad = 8
    padt = lambda u: jnp.pad(u, ((0, 0), (0, 0), (0, tpad - n_tok), (0, 0)))
    qp = padt(q)
    sel = _sample_select(page_table, cache_kt, jnp.swapaxes(qp, -1, -2), n_sel)[:, :, :n_sel, :n_tok]
    sel = jnp.transpose(sel, (0, 1, 3, 2))
    out = _sample_attend(page_table, sel.reshape(-1), slopes, cache_kt, cache_vt, qp, padt(k_new), padt(v_new),
                         n_tok=n_tok, n_sel=n_sel, past=past)
    return out[:, :, :n_tok]


def _alibi_slopes(n_heads):
    return jnp.exp2(-8.0 * (jnp.arange(n_heads, dtype=F32) + 1.0) / n_heads)


def kernel(x_prompt, x_sample, cache_k, cache_v, page_table, state_conv, state_ssm, c_prompt, c_sample,
           w_ada, b_ada, g_mix, w_in, conv_w, conv_b, dt_bias, a_log, d_skip, g_ssd, g_att, w_out,
           g_ffn, w_router, router_bias, w_gate, w_up, w_down, ws_gate, ws_up, ws_down, g_final):
    depth = w_ada.shape[0]
    b, s, d = x_prompt.shape
    bd, ts, _ = x_sample.shape
    tm = TOKEN_TILE
    tp, tsn = b * s, bd * ts
    assert s % tm == 0 and tsn % tm == 0 and tm == MOBA_BLOCK
    n_ssd_heads = dt_bias.shape[1]
    ssd_w = n_ssd_heads * SSD_HEAD_DIM
    att_w = d - ssd_w
    nh = att_w // HEAD_DIM
    conv_dim = conv_w.shape[2]
    n_state = state_ssm.shape[-1]
    n_experts = w_router.shape[2]
    qkvz = 3 * att_w + ssd_w + conv_dim
    slopes = _alibi_slopes(nh)
    prompt_tiles, sample_tiles = tp // tm, tsn // tm
    tps = s // tm
    mod_of_tile = lambda i: jnp.where(i < prompt_tiles, i // tps, b + i - prompt_tiles)

    xp = x_prompt.reshape(tp, d)
    xs = x_sample.reshape(tsn, d)
    kp_l, vp_l, ks_l, vs_l, cp_l, cs_l, hp_l, hs_l = [], [], [], [], [], [], [], []
    for l in range(depth):
        c_all = jnp.concatenate([c_prompt, c_sample], axis=0)
        n_c = c_all.shape[0]
        c_all = jnp.pad(c_all, ((0, -n_c % 8), (0, 0)))
        mods = jnp.split(_adaln(c_all, w_ada[l], b_ada[l])[:n_c], 6, axis=-1)
        mod_p = [u[:b, None, :] for u in mods]
        mod_s = [jnp.repeat(u[b:], ts, axis=0).reshape(sample_tiles, tm, d) for u in mods]
        mod_u = [jnp.concatenate([jnp.broadcast_to(p_, (b, tm, d)), s_], axis=0) for p_, s_ in zip(mod_p, mod_s)]

        dt_cols = w_in[l][:, qkvz:]
        w_all = jnp.concatenate([w_in[l][:, :qkvz], jnp.repeat(dt_cols, SSD_HEAD_DIM, axis=1),
                                 jnp.pad(dt_cols, ((0, 0), (0, 128 - n_ssd_heads)))], axis=1).astype(BF16)
        w_out_b = w_out[l].astype(BF16)
        ssd_par = (conv_w[l], conv_b[l], dt_bias[l], a_log[l], d_skip[l], g_ssd[l])
        dims = dict(att_w=att_w, ssd_w=ssd_w, conv_dim=conv_dim)

        qt, kaug, vt, kpg, vpg, km, z_p, xbc_p, dte_p, dtc_p = _inproj(
            xp, mod_p[0], mod_p[1], g_mix[l], w_all, prompt_batch=(b, s), **dims)
        kmean = km.reshape(b, s // MOBA_BLOCK, nh, HEAD_DIM).transpose(0, 2, 1, 3)
        att_p = _attn_prompt(slopes, qt, kaug, vt, kmean)
        xbc_p3 = xbc_p.reshape(b, s, conv_dim)
        y_p, h_p = _ssd(xbc_p3, z_p.reshape(b, s, ssd_w), dte_p.reshape(b, s, ssd_w), dtc_p.reshape(b, s, 128),
                        jnp.zeros((b, 8, conv_dim), F32), jnp.zeros((b, ssd_w, n_state), F32), *ssd_par,
                        length=MOBA_BLOCK)
        x1 = _outproj(att_p, y_p.reshape(tp, ssd_w), xp, mod_p[2], g_att[l], w_out_b,
                      jnp.zeros((tp + tsn, d), F32), first_tile=0, prompt_batch=(b, s))
        kp_l.append(jnp.swapaxes(kpg, -1, -2))
        vp_l.append(jnp.swapaxes(vpg, -1, -2))
        cp_l.append(xbc_p3[:, s - (CONV_WIDTH - 1):])
        hp_l.append(h_p.reshape(b, n_ssd_heads, SSD_HEAD_DIM, n_state))

        q_s, k_s, v_s, z_s, xbc_s, dte_s, dtc_s = _inproj(xs, mod_s[0], mod_s[1], g_mix[l], w_all, **dims)
        heads = lambda u: u.reshape(bd, ts, nh, HEAD_DIM).transpose(0, 2, 1, 3)
        qh, kh, vh = heads(q_s), heads(k_s), heads(v_s)
        att_s = _moba_sample(qh, kh, vh, jnp.swapaxes(cache_k[l], -1, -2), jnp.swapaxes(cache_v[l], -1, -2),
                             page_table, slopes)
        att_s = att_s.transpose(0, 2, 1, 3).reshape(tsn, att_w)
        xbc_s3 = xbc_s.reshape(bd, ts, conv_dim)
        conv_prev = jnp.pad(state_conv[l], ((0, 0), (8 - (CONV_WIDTH - 1), 0), (0, 0)))
        y_s, h_s = _ssd(xbc_s3, z_s.reshape(bd, ts, ssd_w), dte_s.reshape(bd, ts, ssd_w), dtc_s.reshape(bd, ts, 128),
                        conv_prev, state_ssm[l].reshape(bd, ssd_w, n_state), *ssd_par, length=128)
        x1 = _outproj(att_s, y_s.reshape(tsn, ssd_w), xs, mod_s[2], g_att[l], w_out_b, x1, first_tile=prompt_tiles)
        ks_l.append(kh)
        vs_l.append(vh)
        cs_l.append(jnp.concatenate([state_conv[l], xbc_s3], axis=1)[:, ts:])
        hs_l.append(h_s.reshape(bd, n_ssd_heads, SSD_HEAD_DIM, n_state))

        t_all, shared, sel_t, wts_t = _moe_pre(x1, mod_u[3], mod_u[4], mod_of_tile, g_ffn[l], w_router[l].astype(BF16),
                                               router_bias[l], ws_gate[l].astype(BF16), ws_up[l].astype(BF16),
                                               ws_down[l].astype(BF16))
        n_all = tp + tsn
        sel = sel_t.transpose(0, 2, 1).reshape(n_all, TOP_K)
        wts = wts_t.transpose(0, 2, 1).reshape(n_all, TOP_K)
        routed = _experts(_routing_plan(sel, wts, n_experts, EXPERT_TILE), t_all.reshape(n_all, 8, 128),
                          w_gate[l], w_up[l], w_down[l]).reshape(-1, 128)
        last = l == depth - 1
        fin = functools.partial(_combine, x1, routed, shared, mod_u[5], mod_of_tile, g_final, final_norm=last)
        xp = fin(first_tile=0, n_tiles=prompt_tiles)
        xs = fin(first_tile=prompt_tiles, n_tiles=sample_tiles)

    return (xp.reshape(b, s, d), xs.reshape(bd, ts, d), jnp.stack(kp_l), jnp.stack(vp_l), jnp.stack(ks_l),
            jnp.stack(vs_l), jnp.stack(cp_l), jnp.stack(cs_l), jnp.stack(hp_l), jnp.stack(hs_l))
```

```python
import functools

import jax
import jax.numpy as jnp
from jax import lax
from jax.experimental import pallas as pl
from jax.experimental.pallas import tpu as pltpu

F32 = jnp.float32
BF16 = jnp.bfloat16
I32 = jnp.int32

NORM_EPS = 1e-6
HEAD_DIM = 64
MOBA_BLOCK = 256
MOBA_TOPK = 3
PAGE_SIZE = 128
SSD_HEAD_DIM = 64
SSD_GROUPS = 2
SSD_STATE = 128
CONV_WIDTH = 4
N_EXPERT_GROUPS = 8
TOPK_GROUPS = 4
TOP_K = 8
ROUTED_SCALE = 2.5
MASKED = -1e30
TOKEN_TILE = 256
EXPERT_TILE = 256
PAST_UNROLL = 4
V_PAD = 16
VMEM_LIMIT = 56 * 1024 * 1024


def _params(sem, vmem=VMEM_LIMIT):
    return pltpu.CompilerParams(dimension_semantics=sem, vmem_limit_bytes=vmem)


def _silu(x):
    return x * jax.nn.sigmoid(x)


def _softplus(x):
    return jnp.maximum(x, 0.0) + jnp.log1p(jnp.exp(-jnp.abs(x)))


def _rms(x, g):
    return x * lax.rsqrt(jnp.mean(x * x, axis=-1, keepdims=True) + NORM_EPS) * g


def _dot(a, b):
    return jnp.dot(a, b, preferred_element_type=F32)


def _dot_nt(a, b):
    return lax.dot_general(a, b, (((1,), (1,)), ((), ())), preferred_element_type=F32)


def _dot_exact(a, b):
    return jnp.dot(a, b, preferred_element_type=F32, precision=lax.Precision.HIGHEST)


def _adaln_body(c_ref, w_ref, b_ref, o_ref):
    a = _silu(c_ref[...]).astype(BF16)
    o_ref[...] = _dot(a, w_ref[...].astype(BF16)) + b_ref[...]


def _adaln(c, w_ada, b_ada):
    r, d = c.shape
    n = w_ada.shape[1]
    tn = 512
    return pl.pallas_call(
        _adaln_body,
        out_shape=jax.ShapeDtypeStruct((r, n), F32),
        grid=(n // tn,),
        in_specs=[pl.BlockSpec((r, d), lambda j: (0, 0)),
                  pl.BlockSpec((d, tn), lambda j: (0, j)),
                  pl.BlockSpec((1, tn), lambda j: (0, j))],
        out_specs=pl.BlockSpec((r, tn), lambda j: (0, j)),
        compiler_params=_params(("arbitrary",)),
        name="adaln",
    )(c, w_ada, b_ada.reshape(1, n))


def _inproj_body(x_ref, sh_ref, sc_ref, g_ref, w_ref, *outs, att_w, ssd_w, conv_dim, prompt):
    x = x_ref[...]
    h = (_rms(x, g_ref[...]) * (1.0 + sc_ref[0]) + sh_ref[0]).astype(BF16)
    tm = x.shape[0]
    a = att_w
    c0 = 3 * a
    c1 = c0 + ssd_w
    c2 = c1 + conv_dim
    c3 = c2 + ssd_w

    def proj(lo, hi):
        return _dot(h, w_ref[:, lo:hi])

    q, k, v = proj(0, a), proj(a, 2 * a), proj(2 * a, c0)
    if prompt:
        (qt_ref, kaug_ref, vt_ref, kp_ref, vp_ref, km_ref, z_ref, xbc_ref, dte_ref, dtc_ref) = outs
        nh = a // HEAD_DIM
        qt_ref[0] = q.T.reshape(nh, HEAD_DIM, tm)
        kt = k.T.reshape(nh, HEAD_DIM, tm)
        vt = v.T.reshape(nh, HEAD_DIM, tm)
        pad_row = lax.broadcasted_iota(I32, (nh, V_PAD, tm), 1)
        vt_ref[0] = jnp.concatenate([vt, jnp.where(pad_row == 0, 1.0, 0.0)], axis=1).astype(BF16)
        for pg in range(tm // PAGE_SIZE):
            kp_ref[0, pg] = kt[:, :, pg * PAGE_SIZE:(pg + 1) * PAGE_SIZE]
            vp_ref[0, pg] = vt[:, :, pg * PAGE_SIZE:(pg + 1) * PAGE_SIZE]
        km_ref[0] = jnp.mean(k, axis=0, keepdims=True)
        lane = lax.broadcasted_iota(I32, (tm, 2 * HEAD_DIM), 1)
        row = lax.broadcasted_iota(I32, (tm, 2 * HEAD_DIM), 0)
        pos = jnp.where(lane == HEAD_DIM, (row % MOBA_BLOCK).astype(F32), 0.0)
        low = lane < HEAD_DIM
        for hp in range(nh // 2):
            kc = k[:, hp * 128:(hp + 1) * 128]
            kaug_ref[0, 2 * hp] = jnp.where(low, kc, pos).astype(BF16)
            kaug_ref[0, 2 * hp + 1] = jnp.where(low, pltpu.roll(kc, HEAD_DIM, 1), pos).astype(BF16)
    else:
        (q_ref, k_ref, v_ref, z_ref, xbc_ref, dte_ref, dtc_ref) = outs
        q_ref[...] = q
        k_ref[...] = k
        v_ref[...] = v
    z_ref[...] = proj(c0, c1)
    xbc_ref[...] = proj(c1, c2)
    dte_ref[...] = proj(c2, c3)
    dtc_ref[...] = proj(c3, c3 + 128)


def _inproj(x2d, shift, scale, g, w_all, *, att_w, ssd_w, conv_dim, prompt_batch=None):
    t, d = x2d.shape
    tm = TOKEN_TILE
    nw = w_all.shape[1]
    nt = t // tm
    prompt = prompt_batch is not None
    tok = lambda width: pl.BlockSpec((tm, width), lambda i: (i, 0))
    common_shapes = [jax.ShapeDtypeStruct((t, ssd_w), F32), jax.ShapeDtypeStruct((t, conv_dim), F32),
                     jax.ShapeDtypeStruct((t, ssd_w), F32), jax.ShapeDtypeStruct((t, 128), F32)]
    common_specs = [tok(ssd_w), tok(conv_dim), tok(ssd_w), tok(128)]
    if prompt:
        b, s = prompt_batch
        tps = s // tm
        nh = att_w // HEAD_DIM
        ppt = tm // PAGE_SIZE
        mod_map = lambda i: (i // tps, 0, 0)
        out_shape = [jax.ShapeDtypeStruct((b, nh, HEAD_DIM, s), F32),
                     jax.ShapeDtypeStruct((b, nh, s, 2 * HEAD_DIM), BF16),
                     jax.ShapeDtypeStruct((b, nh, HEAD_DIM + V_PAD, s), BF16),
                     jax.ShapeDtypeStruct((b, s // PAGE_SIZE, nh, HEAD_DIM, PAGE_SIZE), F32),
                     jax.ShapeDtypeStruct((b, s // PAGE_SIZE, nh, HEAD_DIM, PAGE_SIZE), F32),
                     jax.ShapeDtypeStruct((nt, 1, att_w), F32)] + common_shapes
        out_specs = [pl.BlockSpec((1, nh, HEAD_DIM, tm), lambda i: (i // tps, 0, 0, i % tps)),
                     pl.BlockSpec((1, nh, tm, 2 * HEAD_DIM), lambda i: (i // tps, 0, i % tps, 0)),
                     pl.BlockSpec((1, nh, HEAD_DIM + V_PAD, tm), lambda i: (i // tps, 0, 0, i % tps)),
                     pl.BlockSpec((1, ppt, nh, HEAD_DIM, PAGE_SIZE), lambda i: (i // tps, i % tps, 0, 0, 0)),
                     pl.BlockSpec((1, ppt, nh, HEAD_DIM, PAGE_SIZE), lambda i: (i // tps, i % tps, 0, 0, 0)),
                     pl.BlockSpec((1, 1, att_w), lambda i: (i, 0, 0))] + common_specs
        mod_block = (1, 1, d)
    else:
        mod_map = lambda i: (i, 0, 0)
        out_shape = [jax.ShapeDtypeStruct((t, att_w), F32)] * 3 + common_shapes
        out_specs = [tok(att_w)] * 3 + common_specs
        mod_block = (1, tm, d)
    body = functools.partial(_inproj_body, att_w=att_w, ssd_w=ssd_w, conv_dim=conv_dim, prompt=prompt)
    return pl.pallas_call(
        body,
        out_shape=out_shape,
        grid=(nt,),
        in_specs=[tok(d),
                  pl.BlockSpec(mod_block, mod_map),
                  pl.BlockSpec(mod_block, mod_map),
                  pl.BlockSpec((1, d), lambda i: (0, 0)),
                  pl.BlockSpec((d, nw), lambda i: (0, 0))],
        out_specs=out_specs,
        compiler_params=_params(("arbitrary",)),
        name="inproj_prompt" if prompt else "inproj_sample",
    )(x2d, shift, scale, g.reshape(1, d), w_all)


def _select_topk_rows(gate, n_valid, k):
    nrow = gate.shape[0]
    row = lax.broadcasted_iota(I32, gate.shape, 0)
    g = jnp.where(row < n_valid, gate, -jnp.inf)
    chosen = jnp.zeros(gate.shape, I32)
    for _ in range(k):
        m = jnp.max(g, axis=0, keepdims=True)
        cand = jnp.logical_and(g == m, g > -jnp.inf)
        first = jnp.min(jnp.where(cand, row, nrow), axis=0, keepdims=True)
        pick = row == first
        chosen = jnp.where(pick, 1, chosen)
        g = jnp.where(pick, -jnp.inf, g)
    return chosen


def _attn_prompt_body(slope_ref, qt_ref, k_ref, vt_ref, km_ref, o_ref, bias_ref, s0_ref, s1_ref):
    h = pl.program_id(1)
    t = pl.program_id(2)
    slope = slope_ref[h]
    blk = MOBA_BLOCK
    qt = qt_ref[0, 0]
    gate = _dot_exact(km_ref[0, 0], qt)
    chosen = _select_topk_rows(gate, t, MOBA_TOPK)
    jrow = lax.broadcasted_iota(I32, gate.shape, 0)
    iq = lax.broadcasted_iota(I32, gate.shape, 1)
    dist0 = ((t - jrow) * blk + iq).astype(F32)
    bias_ref[...] = jnp.where(chosen > 0, -slope * dist0, MASKED)

    r2 = lax.broadcasted_iota(I32, qt.shape, 0)
    extra = jnp.where(r2 == 0, slope, 0.0)
    qaug = jnp.concatenate([qt * (HEAD_DIM ** -0.5), extra], axis=0).astype(BF16)

    start = pl.multiple_of(t * blk, blk)
    s = _dot(k_ref[0, 0, pl.ds(start, blk), :], qaug)
    ik = lax.broadcasted_iota(I32, s.shape, 0)
    iqq = lax.broadcasted_iota(I32, s.shape, 1)
    s = jnp.where(ik <= iqq, s - slope * iqq.astype(F32), MASKED)
    m = jnp.max(s, axis=0, keepdims=True)
    p = jnp.exp(s - m)
    acc = _dot(vt_ref[0, 0, :, pl.ds(start, blk)], p.astype(BF16))

    nb = bias_ref.shape[0]

    def block_start(trip, u):
        j = jnp.minimum(trip * PAST_UNROLL + u, nb - 1)
        return j, pl.multiple_of(j * blk, blk)

    def scores(trip, dst):
        for u in range(PAST_UNROLL):
            j, st = block_start(trip, u)
            dst[u] = _dot(k_ref[0, 0, pl.ds(st, blk), :], qaug) + bias_ref[pl.ds(j, 1), :]

    def consume(trip, src, carry):
        m, acc = carry
        ss = [src[u] for u in range(PAST_UNROLL)]
        m_new = m
        for s_u in ss:
            m_new = jnp.maximum(m_new, jnp.max(s_u, axis=0, keepdims=True))
        acc = jnp.exp(m - m_new) * acc
        for u in range(PAST_UNROLL):
            _, st = block_start(trip, u)
            p = jnp.exp(ss[u] - m_new)
            acc = acc + _dot(vt_ref[0, 0, :, pl.ds(st, blk)], p.astype(BF16))
        return m_new, acc

    def body(i, carry):
        scores(2 * i + 1, s1_ref)
        carry = consume(2 * i, s0_ref, carry)
        scores(2 * i + 2, s0_ref)
        return consume(2 * i + 1, s1_ref, carry)

    scores(0, s0_ref)
    n_trips = (t + PAST_UNROLL - 1) // PAST_UNROLL
    m, acc = lax.fori_loop(0, (n_trips + 1) // 2, body, (m, acc))
    o_ref[0, 0] = acc[:HEAD_DIM] / acc[HEAD_DIM:HEAD_DIM + 1]


def _attn_prompt(slopes, qt, kaug, vt, kmean):
    b, nh, dh, s = qt.shape
    vrows = vt.shape[2]
    blk = MOBA_BLOCK
    nb = s // blk
    trip_scores = pltpu.VMEM((PAST_UNROLL, blk, blk), F32)
    return pl.pallas_call(
        _attn_prompt_body,
        out_shape=jax.ShapeDtypeStruct((b, nh, dh, s), F32),
        grid_spec=pltpu.PrefetchScalarGridSpec(
            num_scalar_prefetch=1,
            grid=(b, nh, nb),
            in_specs=[pl.BlockSpec((1, 1, dh, blk), lambda bi, hi, ti, sl: (bi, hi, 0, ti)),
                      pl.BlockSpec((1, 1, s, 2 * dh), lambda bi, hi, ti, sl: (bi, hi, 0, 0)),
                      pl.BlockSpec((1, 1, vrows, s), lambda bi, hi, ti, sl: (bi, hi, 0, 0)),
                      pl.BlockSpec((1, 1, nb, dh), lambda bi, hi, ti, sl: (bi, hi, 0, 0))],
            out_specs=pl.BlockSpec((1, 1, dh, blk), lambda bi, hi, ti, sl: (bi, hi, 0, ti)),
            scratch_shapes=[pltpu.VMEM((nb, blk), F32), trip_scores, trip_scores]),
        compiler_params=_params(("arbitrary", "arbitrary", "arbitrary")),
        name="moba_prompt",
    )(slopes, qt, kaug, vt, kmean)


def _ssd_body(xbc_ref, z_ref, dte_ref, dtc_ref, cprev_ref, h0_ref, cw_ref, cb_ref, dtbe_ref, dtbc_ref,
              ale_ref, alc_ref, dsk_ref, g_ref, y_ref, hout_ref, buf_ref, st_ref, *, rows, length):
    c = pl.program_id(1)
    ng = SSD_GROUPS
    n = SSD_STATE
    width = z_ref.shape[-1]
    gw = width // ng

    @pl.when(c == 0)
    def _():
        if rows < length:
            buf_ref[...] = jnp.zeros(buf_ref.shape, F32)
        buf_ref[0:8, :] = cprev_ref[0]
        for g in range(ng):
            st_ref[g] = h0_ref[0, g * gw:(g + 1) * gw, :].T

    buf_ref[8:8 + rows, :] = xbc_ref[0]
    acc = cb_ref[...]
    for i in range(CONV_WIDTH):
        off = 8 - (CONV_WIDTH - 1) + i
        acc = acc + buf_ref[off:off + length, :] * cw_ref[i:i + 1, :]
    if rows == length:
        buf_ref[0:8, :] = buf_ref[length:length + 8, :]
    xc = _silu(acc)
    xs = xc[:, :width]
    bm = xc[:, width:width + ng * n]
    cm = xc[:, width + ng * n:]

    if rows < length:
        dte_full = jnp.concatenate([dte_ref[0], jnp.zeros((length - rows, width), F32)], axis=0)
        dtc_full = jnp.concatenate([dtc_ref[0], jnp.zeros((length - rows, 128), F32)], axis=0)
        live_e = lax.broadcasted_iota(I32, (length, width), 0) < rows
        live_c = lax.broadcasted_iota(I32, (length, 128), 0) < rows
        dt = jnp.where(live_e, _softplus(dte_full + dtbe_ref[...]), 0.0)
        dtc = jnp.where(live_c, _softplus(dtc_full + dtbc_ref[...]), 0.0)
    else:
        dt = _softplus(dte_ref[0] + dtbe_ref[...])
        dtc = _softplus(dtc_ref[0] + dtbc_ref[...])
    da = dt * (-jnp.exp(ale_ref[...]))
    dac = dtc * (-jnp.exp(alc_ref[...]))
    xdt = xs * dt

    ti = lax.broadcasted_iota(I32, (length, length), 0)
    si = lax.broadcasted_iota(I32, (length, length), 1)
    causal = ti >= si
    tri = jnp.where(causal, 1.0, 0.0)
    acs = _dot_exact(tri, da)
    acs_t = _dot_exact(tri, dac).T
    acs_last = acs[length - 1:length, :]
    e_acs = jnp.exp(acs)
    w_tail = jnp.exp(acs_last - acs)
    lane = lax.broadcasted_iota(I32, (length, 128), 1)

    ys = []
    for g in range(ng):
        bg = bm[:, g * n:(g + 1) * n]
        cg = cm[:, g * n:(g + 1) * n].astype(BF16)
        cb = _dot_nt(cg, bg.astype(BF16))
        st = st_ref[g]
        yg = _dot(cg, st.astype(BF16)) * e_acs[:, g * gw:(g + 1) * gw]
        pieces = []
        for pr in range(gw // 128):
            x_pair = xdt[:, g * gw + pr * 128: g * gw + (pr + 1) * 128].astype(BF16)
            halves = []
            for hh in range(2):
                head = (g * gw + pr * 128) // SSD_HEAD_DIM + hh
                col = acs[:, head * SSD_HEAD_DIM: head * SSD_HEAD_DIM + 1]
                rowv = acs_t[head:head + 1, :]
                decay = jnp.where(causal, jnp.exp(col - rowv), 0.0)
                halves.append(_dot((cb * decay).astype(BF16), x_pair))
            pieces.append(jnp.where(lane < SSD_HEAD_DIM, halves[0], halves[1]))
        ys.append(yg + jnp.concatenate(pieces, axis=1))
        xw = (xdt[:, g * gw:(g + 1) * gw] * w_tail[:, g * gw:(g + 1) * gw]).astype(BF16)
        st_ref[g] = st * jnp.exp(acs_last[:, g * gw:(g + 1) * gw]) + _dot(bg.T.astype(BF16), xw)
    y = jnp.concatenate(ys, axis=1)
    y = y + dsk_ref[...] * xs
    y = y[:rows] * _silu(z_ref[0])
    outs = []
    for g in range(ng):
        outs.append(_rms(y[:, g * gw:(g + 1) * gw], g_ref[:, g * gw:(g + 1) * gw]))
    y_ref[0] = jnp.concatenate(outs, axis=1)

    @pl.when(c == pl.num_programs(1) - 1)
    def _():
        for g in range(ng):
            hout_ref[0, g * gw:(g + 1) * gw, :] = st_ref[g].T


def _ssd(xbc, z, dte, dtc, conv_prev, h0, conv_w, conv_b, dt_bias, a_log, d_skip, g_ssd, *, length):
    bn, lt, cdim = xbc.shape
    width = z.shape[-1]
    n = h0.shape[-1]
    rows = min(lt, length)
    nc = lt // rows
    rep = width // dt_bias.shape[0]
    expand = lambda u: jnp.repeat(u, rep).reshape(1, width)
    lane_pad = lambda u: jnp.pad(u, (0, 128 - u.shape[0])).reshape(1, 128)
    seq = lambda w_: pl.BlockSpec((1, rows, w_), lambda b, c: (b, c, 0))
    const = lambda shape: pl.BlockSpec(shape, lambda b, c: (0,) * len(shape))
    body = functools.partial(_ssd_body, rows=rows, length=length)
    return pl.pallas_call(
        body,
        out_shape=[jax.ShapeDtypeStruct((bn, lt, width), F32), jax.ShapeDtypeStruct((bn, width, n), F32)],
        grid=(bn, nc),
        in_specs=[seq(cdim), seq(width), seq(width), seq(128),
                  pl.BlockSpec((1, 8, cdim), lambda b, c: (b, 0, 0)),
                  pl.BlockSpec((1, width, n), lambda b, c: (b, 0, 0)),
                  const((CONV_WIDTH, cdim)), const((1, cdim)), const((1, width)), const((1, 128)),
                  const((1, width)), const((1, 128)), const((1, width)), const((1, width))],
        out_specs=[seq(width), pl.BlockSpec((1, width, n), lambda b, c: (b, 0, 0))],
        scratch_shapes=[pltpu.VMEM((length + 8, cdim), F32),
                        pltpu.VMEM((SSD_GROUPS, n, width // SSD_GROUPS), F32)],
        compiler_params=_params(("arbitrary", "arbitrary")),
        name="ssd_scan",
    )(xbc, z, dte, dtc, conv_prev, h0, conv_w, conv_b.reshape(1, cdim), expand(dt_bias), lane_pad(dt_bias),
      expand(a_log), lane_pad(a_log), expand(d_skip), g_ssd.reshape(1, width))


def _outproj_body(att_ref, ssd_ref, x_ref, gt_ref, ga_ref, w_ref, *rest, transposed):
    o_ref = rest[-1]
    if transposed:
        a = att_ref[0].reshape(-1, att_ref.shape[-1]).T
    else:
        a = att_ref[...]
    mix = jnp.concatenate([_rms(a, ga_ref[...]), ssd_ref[...]], axis=-1).astype(BF16)
    o_ref[...] = x_ref[...] + gt_ref[0] * _dot(mix, w_ref[...])


def _outproj(att, ssd, x2d, gate, g_att, w_out, unified, *, first_tile, prompt_batch=None):
    t, d = x2d.shape
    tm = TOKEN_TILE
    tok = lambda width: pl.BlockSpec((tm, width), lambda i: (i, 0))
    sw = ssd.shape[-1]
    if prompt_batch is not None:
        b, s = prompt_batch
        tps = s // tm
        nh, dh = att.shape[1], att.shape[2]
        aw = nh * dh
        att_spec = pl.BlockSpec((1, nh, dh, tm), lambda i: (i // tps, 0, 0, i % tps))
        gate_spec = pl.BlockSpec((1, 1, d), lambda i: (i // tps, 0, 0))
    else:
        aw = att.shape[-1]
        att_spec = tok(aw)
        gate_spec = pl.BlockSpec((1, tm, d), lambda i: (i, 0, 0))
    return pl.pallas_call(
        functools.partial(_outproj_body, transposed=prompt_batch is not None),
        out_shape=jax.ShapeDtypeStruct(unified.shape, F32),
        grid=(t // tm,),
        in_specs=[att_spec, tok(sw), tok(d), gate_spec,
                  pl.BlockSpec((1, aw), lambda i: (0, 0)),
                  pl.BlockSpec((aw + sw, d), lambda i: (0, 0)),
                  pl.BlockSpec(memory_space=pl.ANY)],
        out_specs=pl.BlockSpec((tm, d), lambda i: (i + first_tile, 0)),
        input_output_aliases={6: 0},
        compiler_params=_params(("arbitrary",)),
        name="outproj",
    )(att, ssd, x2d, gate, g_att.reshape(1, aw), w_out, unified)


def _route_t(st, sbt):
    e, tm = st.shape
    per = e // N_EXPERT_GROUPS
    row = lax.broadcasted_iota(I32, (e, tm), 0)
    rg = lax.broadcasted_iota(I32, (per, tm), 0)
    gs = []
    for g in range(N_EXPERT_GROUPS):
        blk = sbt[g * per:(g + 1) * per]
        m1 = jnp.max(blk, axis=0, keepdims=True)
        first = jnp.min(jnp.where(blk == m1, rg, per), axis=0, keepdims=True)
        m2 = jnp.max(jnp.where(rg == first, -jnp.inf, blk), axis=0, keepdims=True)
        gs.append(m1 + m2)
    gscore = jnp.concatenate(gs, axis=0)
    gch = _select_topk_rows(gscore, N_EXPERT_GROUPS, TOPK_GROUPS)
    emask = jnp.concatenate([jnp.broadcast_to(gch[g:g + 1], (per, tm)) for g in range(N_EXPERT_GROUPS)], axis=0)
    cur = jnp.where(emask > 0, sbt, -jnp.inf)
    sel, wts = [], []
    for _ in range(TOP_K):
        m = jnp.max(cur, axis=0, keepdims=True)
        first = jnp.min(jnp.where(cur == m, row, e), axis=0, keepdims=True)
        pick = row == first
        sel.append(first)
        wts.append(jnp.sum(jnp.where(pick, st, 0.0), axis=0, keepdims=True))
        cur = jnp.where(pick, -jnp.inf, cur)
    sel = jnp.concatenate(sel, axis=0)
    w = jnp.concatenate(wts, axis=0)
    return sel, w / jnp.sum(w, axis=0, keepdims=True) * ROUTED_SCALE


def _rows_to_tiles(rows, ref, first=0):
    n = rows.shape[0]
    for s in range(8):
        ref[pl.ds(first * 8 + s, n, stride=8), :] = rows[:, s * 128:(s + 1) * 128]


def _tiles_to_rows(ref, n, first=0):
    return jnp.concatenate([ref[pl.ds(first * 8 + s, n, stride=8), :] for s in range(8)], axis=1)


def _moe_pre_body(x_ref, sh_ref, sc_ref, g_ref, wr_ref, rb_ref, wsg_ref, wsu_ref, wsd_ref,
                  t_ref, shared_ref, sel_ref, wt_ref):
    t = _rms(x_ref[...], g_ref[...]) * (1.0 + sc_ref[0]) + sh_ref[0]
    _rows_to_tiles(t, t_ref)
    tb = t.astype(BF16)
    st = jax.nn.sigmoid(_dot(tb, wr_ref[...])).T
    sel, w = _route_t(st, st + rb_ref[...])
    sel_ref[0] = sel
    wt_ref[0] = w
    hs = _silu(_dot(tb, wsg_ref[...])) * _dot(tb, wsu_ref[...])
    shared_ref[...] = _dot(hs.astype(BF16), wsd_ref[...])


def _moe_pre(x2d, shift, scale, mod_of_tile, g, w_router, router_bias, ws_gate, ws_up, ws_down):
    t, d = x2d.shape
    assert d == 8 * 128
    tm = TOKEN_TILE
    nt = t // tm
    e = w_router.shape[1]
    sd = ws_gate.shape[1]
    tok = lambda width: pl.BlockSpec((tm, width), lambda i: (i, 0))
    mod = pl.BlockSpec((1, tm, d), lambda i: (mod_of_tile(i), 0, 0))
    full = lambda shape: pl.BlockSpec(shape, lambda i: (0,) * len(shape))
    return pl.pallas_call(
        _moe_pre_body,
        out_shape=[jax.ShapeDtypeStruct((t * 8, 128), F32), jax.ShapeDtypeStruct((t, d), F32),
                   jax.ShapeDtypeStruct((nt, TOP_K, tm), I32), jax.ShapeDtypeStruct((nt, TOP_K, tm), F32)],
        grid=(nt,),
        in_specs=[tok(d), mod, mod, full((1, d)), full((d, e)), full((e, 1)),
                  full((d, sd)), full((d, sd)), full((sd, d))],
        out_specs=[pl.BlockSpec((tm * 8, 128), lambda i: (i, 0)), tok(d),
                   pl.BlockSpec((1, TOP_K, tm), lambda i: (i, 0, 0)),
                   pl.BlockSpec((1, TOP_K, tm), lambda i: (i, 0, 0))],
        compiler_params=_params(("arbitrary",)),
        name="moe_pre",
    )(x2d, shift, scale, g.reshape(1, d), w_router, router_bias.reshape(e, 1), ws_gate, ws_up, ws_down)


SPARE_SLOT = 2
DMA_PHASES = 8


def _experts_body(tile_ref, exp_ref, lo_ref, hi_ref, gslot_ref, qslot_ref, n_ref,
                  tok0_ref, tokn_ref, dstq_ref, dstl_ref, roww_ref, t_hbm, wg_ref, wu_ref, wd_ref,
                  out_hbm, xbuf, ybuf, wgb, wub, wdb, gsem, ssem):
    s = pl.program_id(0)
    n = n_ref[0]
    rows = xbuf.shape[0] // 24

    def tile_of(buf, base, r):
        return buf.at[pl.ds(pl.multiple_of(base + r * 8, 8), 8)]

    def slot_done(buf, sem):
        whole = buf.at[pl.ds(0, rows * 8)]
        pltpu.make_async_copy(whole, whole, sem).wait()

    def gather_start(idx_ref, base, r):
        pltpu.make_async_copy(t_hbm.at[idx_ref[0, 0, r]], tile_of(xbuf, base, r), gsem).start(priority=r % 2)

    def scatter_start(idx_ref, base, r):
        pltpu.make_async_copy(tile_of(ybuf, base, r), out_hbm.at[idx_ref[0, 0, r]], ssem).start(priority=(r + 1) % 2)

    @pl.when(s == 0)
    def _():
        ybuf[...] = jnp.zeros(ybuf.shape, F32)
        for r in range(rows):
            gather_start(tok0_ref, 0, r)
        slot_done(xbuf, gsem)

    @pl.when(s < n)
    def _():
        changed = jnp.logical_or(s == 0, exp_ref[s] != exp_ref[jnp.maximum(s - 1, 0)])

        @pl.when(changed)
        def _():
            wgb[...] = wg_ref[0].astype(BF16)
            wub[...] = wu_ref[0].astype(BF16)
            wdb[...] = wd_ref[0].astype(BF16)

        gbase = pl.multiple_of(gslot_ref[s] * (rows * 8), 8)
        qbase = pl.multiple_of(qslot_ref[s] * (rows * 8), 8)
        per = rows // DMA_PHASES

        def issue(phase):
            for r in range(phase * per, (phase + 1) * per):
                gather_start(tokn_ref, gbase, r)
                scatter_start(dstq_ref, qbase, r)

        base = (tile_ref[s] % 2) * rows
        issue(0)
        x = _tiles_to_rows(xbuf, rows, base).astype(BF16)
        issue(1)
        hg = _dot(x, wgb[...])
        issue(2)
        hu = _dot(x, wub[...])
        issue(3)
        hmid = (_silu(hg) * hu).astype(BF16)
        r_i = lax.broadcasted_iota(I32, (rows, 128), 0)
        mine = jnp.logical_and(r_i >= lo_ref[s], r_i < hi_ref[s])
        w_row = roww_ref[0]
        n_chunks = DMA_PHASES - 4
        cols = wdb.shape[1] // n_chunks
        for c in range(n_chunks):
            issue(4 + c)
            yc = _dot(hmid, wdb[:, c * cols:(c + 1) * cols]) * w_row
            for k in range(cols // 128):
                idx = pl.ds(base * 8 + c * (cols // 128) + k, rows, stride=8)
                ybuf[idx, :] = jnp.where(mine, yc[:, k * 128:(k + 1) * 128], ybuf[idx, :])
        slot_done(xbuf, gsem)
        slot_done(ybuf, ssem)

        @pl.when(s == n - 1)
        def _():
            for r in range(rows):
                scatter_start(dstl_ref, base * 8, r)
            slot_done(ybuf, ssem)


def _experts(plan, t3, w_gate, w_up, w_down):
    item_tile, item_e, item_lo, item_hi, gslot, qslot, n_items, row_tok, row_dst, step_dst, row_w = plan
    n_tiles, _, tile = row_tok.shape
    e, d, ed = w_gate.shape
    assert tile % DMA_PHASES == 0 and d % ((DMA_PHASES - 4) * 128) == 0
    n_steps = item_tile.shape[0]
    on = lambda f: (lambda s, tl, ex, lo, hi, gs, qs, n: f(s, tl, ex))
    smem_blk = lambda f: pl.BlockSpec((1, 1, tile), on(f), memory_space=pltpu.SMEM)
    own_tile = lambda s, tl, ex: (tl[s], 0, 0)
    own_expert = on(lambda s, tl, ex: (ex[s], 0, 0))
    return pl.pallas_call(
        _experts_body,
        out_shape=jax.ShapeDtypeStruct((n_tiles * tile + tile, 8, 128), F32),
        grid_spec=pltpu.PrefetchScalarGridSpec(
            num_scalar_prefetch=7,
            grid=(n_steps,),
            in_specs=[smem_blk(lambda s, tl, ex: (tl[0], 0, 0)),
                      smem_blk(lambda s, tl, ex: (tl[jnp.minimum(s + 1, n_steps - 1)], 0, 0)),
                      smem_blk(lambda s, tl, ex: (s, 0, 0)),
                      smem_blk(own_tile),
                      pl.BlockSpec((1, tile, 1), on(own_tile)),
                      pl.BlockSpec(memory_space=pl.ANY),
                      pl.BlockSpec((1, d, ed), own_expert),
                      pl.BlockSpec((1, d, ed), own_expert),
                      pl.BlockSpec((1, ed, d), own_expert)],
            out_specs=pl.BlockSpec(memory_space=pl.ANY),
            scratch_shapes=[pltpu.VMEM((3 * tile * 8, 128), F32), pltpu.VMEM((3 * tile * 8, 128), F32),
                            pltpu.VMEM((d, ed), BF16), pltpu.VMEM((d, ed), BF16), pltpu.VMEM((ed, d), BF16),
                            pltpu.SemaphoreType.DMA(()), pltpu.SemaphoreType.DMA(())]),
        compiler_params=_params(("arbitrary",)),
        name="moe_experts",
    )(item_tile, item_e, item_lo, item_hi, gslot, qslot, n_items, row_tok, row_tok, step_dst, row_dst,
      row_w.reshape(n_tiles, tile, 1), t3, w_gate, w_up, w_down)


def _routing_plan(sel, wts, n_experts, tile):
    n_tok, k = sel.shape
    tk = n_tok * k
    assert tk % tile == 0
    n_tiles = tk // tile
    e_sorted, order, w_sorted = lax.sort((sel.reshape(-1), jnp.arange(tk, dtype=I32), wts.reshape(-1)), num_keys=1)
    bounds = jnp.searchsorted(e_sorted, jnp.arange(n_experts + 1, dtype=I32), side='left').astype(I32)
    start, end = bounds[:-1], bounds[1:]
    first_tile = start // tile
    n_items_e = jnp.where(end > start, (end - 1) // tile - first_tile + 1, 0)
    item_end = jnp.cumsum(n_items_e)
    n_items = item_end[-1]
    w_idx = jnp.arange(n_tiles + n_experts + 1, dtype=I32)
    live = w_idx < n_items
    item_e = jnp.minimum(jnp.searchsorted(item_end, w_idx, side='right', method='compare_all'),
                         n_experts - 1).astype(I32)
    item_tile = first_tile[item_e] + w_idx - (item_end - n_items_e)[item_e]
    item_lo = jnp.maximum(start[item_e], item_tile * tile) - item_tile * tile
    item_hi = jnp.minimum(end[item_e], (item_tile + 1) * tile) - item_tile * tile
    item_e = jnp.where(live, item_e, e_sorted[-1]).astype(I32)
    item_tile = jnp.where(live, item_tile, n_tiles - 1).astype(I32)
    shape = (n_tiles, 1, tile)
    row_tok = (order // k).reshape(shape)
    row_dst = ((order % k) * n_tok + order // k).reshape(shape)
    prev_tile = jnp.concatenate([item_tile[:1], item_tile[:-1]])
    next_tile = jnp.concatenate([item_tile[1:], item_tile[-1:]])
    opens = jnp.logical_or(w_idx == 0, item_tile != prev_tile)
    closes = jnp.logical_or(w_idx == n_items - 1, next_tile != item_tile)
    opens_next = jnp.concatenate([opens[1:], opens[-1:]])
    closes_prev = jnp.concatenate([closes[:1], closes[:-1]])
    gslot = jnp.where(jnp.logical_and(w_idx + 1 < n_items, opens_next), next_tile % 2, SPARE_SLOT).astype(I32)
    scatters = jnp.logical_and(w_idx >= 1, closes_prev)
    qslot = jnp.where(scatters, prev_tile % 2, SPARE_SLOT).astype(I32)
    spare_rows = tk + jnp.arange(tile, dtype=I32)
    step_dst = jnp.where(scatters[:, None], row_dst[:, 0, :][prev_tile], spare_rows[None, :])[:, None, :]
    return (item_tile, item_e, item_lo.astype(I32), item_hi.astype(I32), gslot, qslot,
            n_items.astype(I32).reshape(1), row_tok, row_dst, step_dst.astype(I32), w_sorted.reshape(shape))


def _combine_body(x_ref, *refs, final_norm):
    r_refs = refs[:TOP_K]
    shared_ref, gt_ref, gf_ref, o_ref, acc_ref = refs[TOP_K:]
    acc = r_refs[0][...]
    for r_ref in r_refs[1:]:
        acc = acc + r_ref[...]
    acc_ref[...] = acc
    y = x_ref[...] + gt_ref[0] * (_tiles_to_rows(acc_ref, x_ref.shape[0]) + shared_ref[...])
    o_ref[...] = _rms(y, gf_ref[...]) if final_norm else y


def _combine(x2d, routed, shared, gate, mod_of_tile, g_final, *, first_tile, n_tiles, final_norm):
    t, d = x2d.shape
    tm = TOKEN_TILE
    tiles_all = t // tm
    tok = pl.BlockSpec((tm, d), lambda i: (i + first_tile, 0))
    slot = lambda k: pl.BlockSpec((tm * 8, 128), lambda i: (i + first_tile + k * tiles_all, 0))
    return pl.pallas_call(
        functools.partial(_combine_body, final_norm=final_norm),
        out_shape=jax.ShapeDtypeStruct((n_tiles * tm, d), F32),
        grid=(n_tiles,),
        in_specs=[tok] + [slot(k) for k in range(TOP_K)] + [
            tok,
            pl.BlockSpec((1, tm, d), lambda i: (mod_of_tile(i + first_tile), 0, 0)),
            pl.BlockSpec((1, d), lambda i: (0, 0))],
        out_specs=pl.BlockSpec((tm, d), lambda i: (i, 0)),
        scratch_shapes=[pltpu.VMEM((tm * 8, 128), F32)],
        compiler_params=_params(("arbitrary",)),
        name="moe_combine",
    )(x2d, *([routed] * TOP_K), shared, gate, g_final.reshape(1, d))


SELECT_PAGES = 16


def _sample_select_body(pt_ref, *refs, n_sel):
    pages = refs[:SELECT_PAGES]
    qt_ref, sel_ref, gate_ref = refs[SELECT_PAGES:]
    c = pl.program_id(1)
    ppb = MOBA_BLOCK // PAGE_SIZE
    nh = qt_ref.shape[1]
    for n in range(SELECT_PAGES // ppb):
        tot = pages[n * ppb][0]
        for i in range(1, ppb):
            tot = tot + pages[n * ppb + i][0]
        ksum = jnp.sum(tot, axis=2, keepdims=True)
        gate_ref[:, pl.ds(c * (SELECT_PAGES // ppb) + n, 1), :] = jnp.sum(qt_ref[0] * ksum, axis=1, keepdims=True)

    @pl.when(c == pl.num_programs(1) - 1)
    def _():
        nblk = gate_ref.shape[1]
        for h in range(nh):
            gate = gate_ref[h]
            row = lax.broadcasted_iota(I32, gate.shape, 0)
            g = gate
            picks = []
            for _ in range(n_sel):
                m = jnp.max(g, axis=0, keepdims=True)
                first = jnp.min(jnp.where(g == m, row, nblk), axis=0, keepdims=True)
                picks.append(first)
                g = jnp.where(row == first, -jnp.inf, g)
            picks.append(jnp.zeros((8 - n_sel, gate.shape[1]), I32))
            sel_ref[0, h] = jnp.concatenate(picks, axis=0)


def _sample_select(page_table, cache_kt, qt, n_sel):
    bd, n_pages = page_table.shape
    _, nh, dh, page = cache_kt.shape
    tpad = qt.shape[3]
    nblk = n_pages * PAGE_SIZE // MOBA_BLOCK
    page_spec = lambda i: pl.BlockSpec((1, nh, dh, page),
                                       lambda b, c, pt: (pt[b, c * SELECT_PAGES + i], 0, 0, 0))
    return pl.pallas_call(
        functools.partial(_sample_select_body, n_sel=n_sel),
        out_shape=jax.ShapeDtypeStruct((bd, nh, 8, tpad), I32),
        grid_spec=pltpu.PrefetchScalarGridSpec(
            num_scalar_prefetch=1,
            grid=(bd, n_pages // SELECT_PAGES),
            in_specs=[page_spec(i) for i in range(SELECT_PAGES)]
                     + [pl.BlockSpec((1, nh, dh, tpad), lambda b, c, pt: (b, 0, 0, 0))],
            out_specs=pl.BlockSpec((1, nh, 8, tpad), lambda b, c, pt: (b, 0, 0, 0)),
            scratch_shapes=[pltpu.VMEM((nh, nblk, tpad), F32)]),
        compiler_params=_params(("arbitrary", "arbitrary")),
        name="moba_sample_select",
    )(page_table, *([cache_kt] * SELECT_PAGES), qt)


def _sample_attend_body(pt_ref, selb_ref, slope_ref, kt_hbm, vt_hbm, q_ref, kn_ref, vn_ref, o_ref,
                        kbuf, vbuf, ksem, vsem, *, n_tok, n_sel, past):
    ppb = MOBA_BLOCK // PAGE_SIZE
    per_tok = n_sel * ppb
    nkv = n_tok * per_tok
    b = pl.program_id(0)
    h = pl.program_id(1)
    nh = q_ref.shape[1]
    step = b * nh + h
    n_steps = pl.num_programs(0) * nh

    def fetch(stp, slot):
        hh = stp % nh
        bb = stp // nh
        for i in range(nkv):
            logical_block = selb_ref[stp * (nkv // ppb) + i // ppb]
            pg = pt_ref[bb, logical_block * ppb + i % ppb]
            pltpu.make_async_copy(kt_hbm.at[pg, hh], kbuf.at[slot, i], ksem.at[slot]).start(priority=0)
            pltpu.make_async_copy(vt_hbm.at[pg, hh], vbuf.at[slot, i], vsem.at[slot]).start(priority=1)

    @pl.when(step == 0)
    def _():
        fetch(0, 0)

    @pl.when(step + 1 < n_steps)
    def _():
        fetch(step + 1, (step + 1) % 2)

    slot = step % 2
    pltpu.make_async_copy(kbuf.at[slot], kbuf.at[slot], ksem.at[slot]).wait()
    pltpu.make_async_copy(vbuf.at[slot], vbuf.at[slot], vsem.at[slot]).wait()

    slope = slope_ref[h]
    q = (q_ref[0, h] * (HEAD_DIM ** -0.5)).astype(BF16)
    tpad = q.shape[0]
    ncol = per_tok * PAGE_SIZE
    s_new = _dot_nt(q, kn_ref[0, h].astype(BF16))
    ti = lax.broadcasted_iota(I32, s_new.shape, 0)
    oi = lax.broadcasted_iota(I32, s_new.shape, 1)
    s_new = jnp.where(jnp.logical_and(oi <= ti, oi < n_tok), s_new - slope * (ti - oi).astype(F32), MASKED)
    vn = vn_ref[0, h].astype(BF16)
    col = lax.broadcasted_iota(I32, (tpad, ncol), 1)
    rowi = lax.broadcasted_iota(I32, (tpad, ncol), 0)
    outs = []
    for t in range(n_tok):
        kt = jnp.concatenate([kbuf[slot, t * per_tok + i] for i in range(per_tok)], axis=1).astype(BF16)
        vt = jnp.concatenate([vbuf[slot, t * per_tok + i] for i in range(per_tok)], axis=1).astype(BF16)
        s = _dot(q, kt)
        blk_of_col = jnp.zeros((tpad, ncol), I32)
        for r in range(n_sel):
            sb = selb_ref[(step * n_tok + t) * n_sel + r]
            blk_of_col = jnp.where(col // MOBA_BLOCK == r, sb, blk_of_col)
        spos = blk_of_col * MOBA_BLOCK + col % MOBA_BLOCK
        s = jnp.where(rowi == t, s - slope * (past + t - spos).astype(F32), MASKED)
        so = s_new[t:t + 1]
        m = jnp.maximum(jnp.max(jnp.max(s, axis=1, keepdims=True), axis=0, keepdims=True),
                        jnp.max(so, axis=1, keepdims=True))
        p = jnp.exp(s - m)
        po = jnp.exp(so - m)
        l = jnp.sum(jnp.sum(p, axis=1, keepdims=True), axis=0, keepdims=True) + jnp.sum(po, axis=1, keepdims=True)
        o_sel = _dot_nt(p.astype(BF16), vt)[t:t + 1]
        o_new = _dot(jnp.broadcast_to(po, (tpad, tpad)).astype(BF16), vn)[0:1]
        outs.append((o_sel + o_new) / l)
    outs.append(jnp.zeros((tpad - n_tok, HEAD_DIM), F32))
    o_ref[0, h] = jnp.concatenate(outs, axis=0)


def _sample_attend(page_table, selb, slopes, cache_kt, cache_vt, q, k_new, v_new, *, n_tok, n_sel, past):
    bd, nh, tpad, dh = q.shape
    page = cache_kt.shape[3]
    ppb = MOBA_BLOCK // PAGE_SIZE
    nkv = n_tok * n_sel * ppb
    pool = pl.BlockSpec(memory_space=pl.ANY)
    seq_spec = pl.BlockSpec((1, nh, tpad, dh), lambda b, h, ph, sb, sl: (b, 0, 0, 0))
    slices = pltpu.VMEM((2, nkv, dh, page), F32)
    return pl.pallas_call(
        functools.partial(_sample_attend_body, n_tok=n_tok, n_sel=n_sel, past=past),
        out_shape=jax.ShapeDtypeStruct((bd, nh, tpad, dh), F32),
        grid_spec=pltpu.PrefetchScalarGridSpec(
            num_scalar_prefetch=3,
            grid=(bd, nh),
            in_specs=[pool, pool, seq_spec, seq_spec, seq_spec],
            out_specs=seq_spec,
            scratch_shapes=[slices, slices, pltpu.SemaphoreType.DMA((2,)), pltpu.SemaphoreType.DMA((2,))]),
        compiler_params=_params(("arbitrary", "arbitrary")),
        name="moba_sample_attend",
    )(page_table, selb, slopes, cache_kt, cache_vt, q, k_new, v_new)


def _moba_sample(q, k_new, v_new, cache_kt, cache_vt, page_table, slopes):
    bd, nh, n_tok, dh = q.shape
    n_pages = page_table.shape[1]
    past = n_pages * PAGE_SIZE
    assert past % MOBA_BLOCK == 0 and n_pages % SELECT_PAGES == 0 and n_tok <= 8
    n_sel = min(MOBA_TOPK, past // MOBA_BLOCK)
    assert n_sel > 0
    tpad = 8
    padt = lambda u: jnp.pad(u, ((0, 0), (0, 0), (0, tpad - n_tok), (0, 0)))
    qp = padt(q)
    sel = _sample_select(page_table, cache_kt, jnp.swapaxes(qp, -1, -2), n_sel)[:, :, :n_sel, :n_tok]
    sel = jnp.transpose(sel, (0, 1, 3, 2))
    out = _sample_attend(page_table, sel.reshape(-1), slopes, cache_kt, cache_vt, qp, padt(k_new), padt(v_new),
                         n_tok=n_tok, n_sel=n_sel, past=past)
    return out[:, :, :n_tok]


def _alibi_slopes(n_heads):
    return jnp.exp2(-8.0 * (jnp.arange(n_heads, dtype=F32) + 1.0) / n_heads)


def kernel(x_prompt, x_sample, cache_k, cache_v, page_table, state_conv, state_ssm, c_prompt, c_sample,
           w_ada, b_ada, g_mix, w_in, conv_w, conv_b, dt_bias, a_log, d_skip, g_ssd, g_att, w_out,
           g_ffn, w_router, router_bias, w_gate, w_up, w_down, ws_gate, ws_up, ws_down, g_final):
    depth = w_ada.shape[0]
    b, s, d = x_prompt.shape
    bd, ts, _ = x_sample.shape
    tm = TOKEN_TILE
    tp, tsn = b * s, bd * ts
    assert s % tm == 0 and tsn % tm == 0 and tm == MOBA_BLOCK
    n_ssd_heads = dt_bias.shape[1]
    ssd_w = n_ssd_heads * SSD_HEAD_DIM
    att_w = d - ssd_w
    nh = att_w // HEAD_DIM
    conv_dim = conv_w.shape[2]
    n_state = state_ssm.shape[-1]
    n_experts = w_router.shape[2]
    qkvz = 3 * att_w + ssd_w + conv_dim
    slopes = _alibi_slopes(nh)
    prompt_tiles, sample_tiles = tp // tm, tsn // tm
    tps = s // tm
    mod_of_tile = lambda i: jnp.where(i < prompt_tiles, i // tps, b + i - prompt_tiles)

    xp = x_prompt.reshape(tp, d)
    xs = x_sample.reshape(tsn, d)
    kp_l, vp_l, ks_l, vs_l, cp_l, cs_l, hp_l, hs_l = [], [], [], [], [], [], [], []
    for l in range(depth):
        c_all = jnp.concatenate([c_prompt, c_sample], axis=0)
        n_c = c_all.shape[0]
        c_all = jnp.pad(c_all, ((0, -n_c % 8), (0, 0)))
        mods = jnp.split(_adaln(c_all, w_ada[l], b_ada[l])[:n_c], 6, axis=-1)
        mod_p = [u[:b, None, :] for u in mods]
        mod_s = [jnp.repeat(u[b:], ts, axis=0).reshape(sample_tiles, tm, d) for u in mods]
        mod_u = [jnp.concatenate([jnp.broadcast_to(p_, (b, tm, d)), s_], axis=0) for p_, s_ in zip(mod_p, mod_s)]

        dt_cols = w_in[l][:, qkvz:]
        w_all = jnp.concatenate([w_in[l][:, :qkvz], jnp.repeat(dt_cols, SSD_HEAD_DIM, axis=1),
                                 jnp.pad(dt_cols, ((0, 0), (0, 128 - n_ssd_heads)))], axis=1).astype(BF16)
        w_out_b = w_out[l].astype(BF16)
        ssd_par = (conv_w[l], conv_b[l], dt_bias[l], a_log[l], d_skip[l], g_ssd[l])
        dims = dict(att_w=att_w, ssd_w=ssd_w, conv_dim=conv_dim)

        qt, kaug, vt, kpg, vpg, km, z_p, xbc_p, dte_p, dtc_p = _inproj(
            xp, mod_p[0], mod_p[1], g_mix[l], w_all, prompt_batch=(b, s), **dims)
        kmean = km.reshape(b, s // MOBA_BLOCK, nh, HEAD_DIM).transpose(0, 2, 1, 3)
        att_p = _attn_prompt(slopes, qt, kaug, vt, kmean)
        xbc_p3 = xbc_p.reshape(b, s, conv_dim)
        y_p, h_p = _ssd(xbc_p3, z_p.reshape(b, s, ssd_w), dte_p.reshape(b, s, ssd_w), dtc_p.reshape(b, s, 128),
                        jnp.zeros((b, 8, conv_dim), F32), jnp.zeros((b, ssd_w, n_state), F32), *ssd_par,
                        length=MOBA_BLOCK)
        x1 = _outproj(att_p, y_p.reshape(tp, ssd_w), xp, mod_p[2], g_att[l], w_out_b,
                      jnp.zeros((tp + tsn, d), F32), first_tile=0, prompt_batch=(b, s))
        kp_l.append(jnp.swapaxes(kpg, -1, -2))
        vp_l.append(jnp.swapaxes(vpg, -1, -2))
        cp_l.append(xbc_p3[:, s - (CONV_WIDTH - 1):])
        hp_l.append(h_p.reshape(b, n_ssd_heads, SSD_HEAD_DIM, n_state))

        q_s, k_s, v_s, z_s, xbc_s, dte_s, dtc_s = _inproj(xs, mod_s[0], mod_s[1], g_mix[l], w_all, **dims)
        heads = lambda u: u.reshape(bd, ts, nh, HEAD_DIM).transpose(0, 2, 1, 3)
        qh, kh, vh = heads(q_s), heads(k_s), heads(v_s)
        att_s = _moba_sample(qh, kh, vh, jnp.swapaxes(cache_k[l], -1, -2), jnp.swapaxes(cache_v[l], -1, -2),
                             page_table, slopes)
        att_s = att_s.transpose(0, 2, 1, 3).reshape(tsn, att_w)
        xbc_s3 = xbc_s.reshape(bd, ts, conv_dim)
        conv_prev = jnp.pad(state_conv[l], ((0, 0), (8 - (CONV_WIDTH - 1), 0), (0, 0)))
        y_s, h_s = _ssd(xbc_s3, z_s.reshape(bd, ts, ssd_w), dte_s.reshape(bd, ts, ssd_w), dtc_s.reshape(bd, ts, 128),
                        conv_prev, state_ssm[l].reshape(bd, ssd_w, n_state), *ssd_par, length=128)
        x1 = _outproj(att_s, y_s.reshape(tsn, ssd_w), xs, mod_s[2], g_att[l], w_out_b, x1, first_tile=prompt_tiles)
        ks_l.append(kh)
        vs_l.append(vh)
        cs_l.append(jnp.concatenate([state_conv[l], xbc_s3], axis=1)[:, ts:])
        hs_l.append(h_s.reshape(bd, n_ssd_heads, SSD_HEAD_DIM, n_state))

        t_all, shared, sel_t, wts_t = _moe_pre(x1, mod_u[3], mod_u[4], mod_of_tile, g_ffn[l], w_router[l].astype(BF16),
                                               router_bias[l], ws_gate[l].astype(BF16), ws_up[l].astype(BF16),
                                               ws_down[l].astype(BF16))
        n_all = tp + tsn
        sel = sel_t.transpose(0, 2, 1).reshape(n_all, TOP_K)
        wts = wts_t.transpose(0, 2, 1).reshape(n_all, TOP_K)
        routed = _experts(_routing_plan(sel, wts, n_experts, EXPERT_TILE), t_all.reshape(n_all, 8, 128),
                          w_gate[l], w_up[l], w_down[l]).reshape(-1, 128)
        last = l == depth - 1
        fin = functools.partial(_combine, x1, routed, shared, mod_u[5], mod_of_tile, g_final, final_norm=last)
        xp = fin(first_tile=0, n_tiles=prompt_tiles)
        xs = fin(first_tile=prompt_tiles, n_tiles=sample_tiles)

    return (xp.reshape(b, s, d), xs.reshape(bd, ts, d), jnp.stack(kp_l), jnp.stack(vp_l), jnp.stack(ks_l),
            jnp.stack(vs_l), jnp.stack(cp_l), jnp.stack(cs_l), jnp.stack(hp_l), jnp.stack(hs_l))
```

```python
import functools

import jax
import jax.numpy as jnp
from jax import lax
from jax.experimental import pallas as pl
from jax.experimental.pallas import tpu as pltpu

F32 = jnp.float32
BF16 = jnp.bfloat16
I32 = jnp.int32

NORM_EPS = 1e-6
HEAD_DIM = 64
MOBA_BLOCK = 256
MOBA_TOPK = 3
PAGE_SIZE = 128
SSD_HEAD_DIM = 64
SSD_GROUPS = 2
SSD_STATE = 128
CONV_WIDTH = 4
N_EXPERT_GROUPS = 8
TOPK_GROUPS = 4
TOP_K = 8
ROUTED_SCALE = 2.5
MASKED = -1e30
TOKEN_TILE = 256
EXPERT_TILE = 256
PAST_UNROLL = 4
V_PAD = 16
VMEM_LIMIT = 56 * 1024 * 1024


def _params(sem, vmem=VMEM_LIMIT):
    return pltpu.CompilerParams(dimension_semantics=sem, vmem_limit_bytes=vmem)


def _silu(x):
    return x * jax.nn.sigmoid(x)


def _softplus(x):
    return jnp.maximum(x, 0.0) + jnp.log1p(jnp.exp(-jnp.abs(x)))


def _rms(x, g):
    return x * lax.rsqrt(jnp.mean(x * x, axis=-1, keepdims=True) + NORM_EPS) * g


def _dot(a, b):
    return jnp.dot(a, b, preferred_element_type=F32)


def _dot_nt(a, b):
    return lax.dot_general(a, b, (((1,), (1,)), ((), ())), preferred_element_type=F32)


def _dot_exact(a, b):
    return jnp.dot(a, b, preferred_element_type=F32, precision=lax.Precision.HIGHEST)


def _adaln_body(c_ref, w_ref, b_ref, o_ref):
    a = _silu(c_ref[...]).astype(BF16)
    o_ref[...] = _dot(a, w_ref[...].astype(BF16)) + b_ref[...]


def _adaln(c, w_ada, b_ada):
    r, d = c.shape
    n = w_ada.shape[1]
    tn = 512
    return pl.pallas_call(
        _adaln_body,
        out_shape=jax.ShapeDtypeStruct((r, n), F32),
        grid=(n // tn,),
        in_specs=[pl.BlockSpec((r, d), lambda j: (0, 0)),
                  pl.BlockSpec((d, tn), lambda j: (0, j)),
                  pl.BlockSpec((1, tn), lambda j: (0, j))],
        out_specs=pl.BlockSpec((r, tn), lambda j: (0, j)),
        compiler_params=_params(("arbitrary",)),
        name="adaln",
    )(c, w_ada, b_ada.reshape(1, n))


def _inproj_body(x_ref, sh_ref, sc_ref, g_ref, w_ref, *outs, att_w, ssd_w, conv_dim, prompt):
    x = x_ref[...]
    h = (_rms(x, g_ref[...]) * (1.0 + sc_ref[0]) + sh_ref[0]).astype(BF16)
    tm = x.shape[0]
    a = att_w
    c0 = 3 * a
    c1 = c0 + ssd_w
    c2 = c1 + conv_dim
    c3 = c2 + ssd_w

    def proj(lo, hi):
        return _dot(h, w_ref[:, lo:hi])

    q, k, v = proj(0, a), proj(a, 2 * a), proj(2 * a, c0)
    if prompt:
        (qt_ref, kaug_ref, vt_ref, kp_ref, vp_ref, km_ref, z_ref, xbc_ref, dte_ref, dtc_ref) = outs
        nh = a // HEAD_DIM
        qt_ref[0] = q.T.reshape(nh, HEAD_DIM, tm)
        kt = k.T.reshape(nh, HEAD_DIM, tm)
        vt = v.T.reshape(nh, HEAD_DIM, tm)
        pad_row = lax.broadcasted_iota(I32, (nh, V_PAD, tm), 1)
        vt_ref[0] = jnp.concatenate([vt, jnp.where(pad_row == 0, 1.0, 0.0)], axis=1).astype(BF16)
        for pg in range(tm // PAGE_SIZE):
            kp_ref[0, pg] = kt[:, :, pg * PAGE_SIZE:(pg + 1) * PAGE_SIZE]
            vp_ref[0, pg] = vt[:, :, pg * PAGE_SIZE:(pg + 1) * PAGE_SIZE]
        km_ref[0] = jnp.mean(k, axis=0, keepdims=True)
        lane = lax.broadcasted_iota(I32, (tm, 2 * HEAD_DIM), 1)
        row = lax.broadcasted_iota(I32, (tm, 2 * HEAD_DIM), 0)
        pos = jnp.where(lane == HEAD_DIM, (row % MOBA_BLOCK).astype(F32), 0.0)
        low = lane < HEAD_DIM
        for hp in range(nh // 2):
            kc = k[:, hp * 128:(hp + 1) * 128]
            kaug_ref[0, 2 * hp] = jnp.where(low, kc, pos).astype(BF16)
            kaug_ref[0, 2 * hp + 1] = jnp.where(low, pltpu.roll(kc, HEAD_DIM, 1), pos).astype(BF16)
    else:
        (q_ref, k_ref, v_ref, z_ref, xbc_ref, dte_ref, dtc_ref) = outs
        q_ref[...] = q
        k_ref[...] = k
        v_ref[...] = v
    z_ref[...] = proj(c0, c1)
    xbc_ref[...] = proj(c1, c2)
    dte_ref[...] = proj(c2, c3)
    dtc_ref[...] = proj(c3, c3 + 128)


def _inproj(x2d, shift, scale, g, w_all, *, att_w, ssd_w, conv_dim, prompt_batch=None):
    t, d = x2d.shape
    tm = TOKEN_TILE
    nw = w_all.shape[1]
    nt = t // tm
    prompt = prompt_batch is not None
    tok = lambda width: pl.BlockSpec((tm, width), lambda i: (i, 0))
    common_shapes = [jax.ShapeDtypeStruct((t, ssd_w), F32), jax.ShapeDtypeStruct((t, conv_dim), F32),
                     jax.ShapeDtypeStruct((t, ssd_w), F32), jax.ShapeDtypeStruct((t, 128), F32)]
    common_specs = [tok(ssd_w), tok(conv_dim), tok(ssd_w), tok(128)]
    if prompt:
        b, s = prompt_batch
        tps = s // tm
        nh = att_w // HEAD_DIM
        ppt = tm // PAGE_SIZE
        mod_map = lambda i: (i // tps, 0, 0)
        out_shape = [jax.ShapeDtypeStruct((b, nh, HEAD_DIM, s), F32),
                     jax.ShapeDtypeStruct((b, nh, s, 2 * HEAD_DIM), BF16),
                     jax.ShapeDtypeStruct((b, nh, HEAD_DIM + V_PAD, s), BF16),
                     jax.ShapeDtypeStruct((b, s // PAGE_SIZE, nh, HEAD_DIM, PAGE_SIZE), F32),
                     jax.ShapeDtypeStruct((b, s // PAGE_SIZE, nh, HEAD_DIM, PAGE_SIZE), F32),
                     jax.ShapeDtypeStruct((nt, 1, att_w), F32)] + common_shapes
        out_specs = [pl.BlockSpec((1, nh, HEAD_DIM, tm), lambda i: (i // tps, 0, 0, i % tps)),
                     pl.BlockSpec((1, nh, tm, 2 * HEAD_DIM), lambda i: (i // tps, 0, i % tps, 0)),
                     pl.BlockSpec((1, nh, HEAD_DIM + V_PAD, tm), lambda i: (i // tps, 0, 0, i % tps)),
                     pl.BlockSpec((1, ppt, nh, HEAD_DIM, PAGE_SIZE), lambda i: (i // tps, i % tps, 0, 0, 0)),
                     pl.BlockSpec((1, ppt, nh, HEAD_DIM, PAGE_SIZE), lambda i: (i // tps, i % tps, 0, 0, 0)),
                     pl.BlockSpec((1, 1, att_w), lambda i: (i, 0, 0))] + common_specs
        mod_block = (1, 1, d)
    else:
        mod_map = lambda i: (i, 0, 0)
        out_shape = [jax.ShapeDtypeStruct((t, att_w), F32)] * 3 + common_shapes
        out_specs = [tok(att_w)] * 3 + common_specs
        mod_block = (1, tm, d)
    body = functools.partial(_inproj_body, att_w=att_w, ssd_w=ssd_w, conv_dim=conv_dim, prompt=prompt)
    return pl.pallas_call(
        body,
        out_shape=out_shape,
        grid=(nt,),
        in_specs=[tok(d),
                  pl.BlockSpec(mod_block, mod_map),
                  pl.BlockSpec(mod_block, mod_map),
                  pl.BlockSpec((1, d), lambda i: (0, 0)),
                  pl.BlockSpec((d, nw), lambda i: (0, 0))],
        out_specs=out_specs,
        compiler_params=_params(("arbitrary",)),
        name="inproj_prompt" if prompt else "inproj_sample",
    )(x2d, shift, scale, g.reshape(1, d), w_all)


def _select_topk_rows(gate, n_valid, k):
    nrow = gate.shape[0]
    row = lax.broadcasted_iota(I32, gate.shape, 0)
    g = jnp.where(row < n_valid, gate, -jnp.inf)
    chosen = jnp.zeros(gate.shape, I32)
    for _ in range(k):
        m = jnp.max(g, axis=0, keepdims=True)
        cand = jnp.logical_and(g == m, g > -jnp.inf)
        first = jnp.min(jnp.where(cand, row, nrow), axis=0, keepdims=True)
        pick = row == first
        chosen = jnp.where(pick, 1, chosen)
        g = jnp.where(pick, -jnp.inf, g)
    return chosen


def _attn_prompt_body(slope_ref, qt_ref, k_ref, vt_ref, km_ref, o_ref, bias_ref, s0_ref, s1_ref):
    h = pl.program_id(1)
    t = pl.program_id(2)
    slope = slope_ref[h]
    blk = MOBA_BLOCK
    qt = qt_ref[0, 0]
    gate = _dot_exact(km_ref[0, 0], qt)
    chosen = _select_topk_rows(gate, t, MOBA_TOPK)
    jrow = lax.broadcasted_iota(I32, gate.shape, 0)
    iq = lax.broadcasted_iota(I32, gate.shape, 1)
    dist0 = ((t - jrow) * blk + iq).astype(F32)
    bias_ref[...] = jnp.where(chosen > 0, -slope * dist0, MASKED)

    r2 = lax.broadcasted_iota(I32, qt.shape, 0)
    extra = jnp.where(r2 == 0, slope, 0.0)
    qaug = jnp.concatenate([qt * (HEAD_DIM ** -0.5), extra], axis=0).astype(BF16)

    start = pl.multiple_of(t * blk, blk)
    s = _dot(k_ref[0, 0, pl.ds(start, blk), :], qaug)
    ik = lax.broadcasted_iota(I32, s.shape, 0)
    iqq = lax.broadcasted_iota(I32, s.shape, 1)
    s = jnp.where(ik <= iqq, s - slope * iqq.astype(F32), MASKED)
    m = jnp.max(s, axis=0, keepdims=True)
    p = jnp.exp(s - m)
    acc = _dot(vt_ref[0, 0, :, pl.ds(start, blk)], p.astype(BF16))

    nb = bias_ref.shape[0]

    def block_start(trip, u):
        j = jnp.minimum(trip * PAST_UNROLL + u, nb - 1)
        return j, pl.multiple_of(j * blk, blk)

    def scores(trip, dst):
        for u in range(PAST_UNROLL):
            j, st = block_start(trip, u)
            dst[u] = _dot(k_ref[0, 0, pl.ds(st, blk), :], qaug) + bias_ref[pl.ds(j, 1), :]

    def consume(trip, src, carry):
        m, acc = carry
        ss = [src[u] for u in range(PAST_UNROLL)]
        m_new = m
        for s_u in ss:
            m_new = jnp.maximum(m_new, jnp.max(s_u, axis=0, keepdims=True))
        acc = jnp.exp(m - m_new) * acc
        for u in range(PAST_UNROLL):
            _, st = block_start(trip, u)
            p = jnp.exp(ss[u] - m_new)
            acc = acc + _dot(vt_ref[0, 0, :, pl.ds(st, blk)], p.astype(BF16))
        return m_new, acc

    def body(i, carry):
        scores(2 * i + 1, s1_ref)
        carry = consume(2 * i, s0_ref, carry)
        scores(2 * i + 2, s0_ref)
        return consume(2 * i + 1, s1_ref, carry)

    scores(0, s0_ref)
    n_trips = (t + PAST_UNROLL - 1) // PAST_UNROLL
    m, acc = lax.fori_loop(0, (n_trips + 1) // 2, body, (m, acc))
    o_ref[0, 0] = acc[:HEAD_DIM] / acc[HEAD_DIM:HEAD_DIM + 1]


def _attn_prompt(slopes, qt, kaug, vt, kmean):
    b, nh, dh, s = qt.shape
    vrows = vt.shape[2]
    blk = MOBA_BLOCK
    nb = s // blk
    trip_scores = pltpu.VMEM((PAST_UNROLL, blk, blk), F32)
    return pl.pallas_call(
        _attn_prompt_body,
        out_shape=jax.ShapeDtypeStruct((b, nh, dh, s), F32),
        grid_spec=pltpu.PrefetchScalarGridSpec(
            num_scalar_prefetch=1,
            grid=(b, nh, nb),
            in_specs=[pl.BlockSpec((1, 1, dh, blk), lambda bi, hi, ti, sl: (bi, hi, 0, ti)),
                      pl.BlockSpec((1, 1, s, 2 * dh), lambda bi, hi, ti, sl: (bi, hi, 0, 0)),
                      pl.BlockSpec((1, 1, vrows, s), lambda bi, hi, ti, sl: (bi, hi, 0, 0)),
                      pl.BlockSpec((1, 1, nb, dh), lambda bi, hi, ti, sl: (bi, hi, 0, 0))],
            out_specs=pl.BlockSpec((1, 1, dh, blk), lambda bi, hi, ti, sl: (bi, hi, 0, ti)),
            scratch_shapes=[pltpu.VMEM((nb, blk), F32), trip_scores, trip_scores]),
        compiler_params=_params(("arbitrary", "arbitrary", "arbitrary")),
        name="moba_prompt",
    )(slopes, qt, kaug, vt, kmean)


def _ssd_body(xbc_ref, z_ref, dte_ref, dtc_ref, cprev_ref, h0_ref, cw_ref, cb_ref, dtbe_ref, dtbc_ref,
              ale_ref, alc_ref, dsk_ref, g_ref, y_ref, hout_ref, buf_ref, st_ref, *, rows, length):
    c = pl.program_id(1)
    ng = SSD_GROUPS
    n = SSD_STATE
    width = z_ref.shape[-1]
    gw = width // ng

    @pl.when(c == 0)
    def _():
        if rows < length:
            buf_ref[...] = jnp.zeros(buf_ref.shape, F32)
        buf_ref[0:8, :] = cprev_ref[0]
        for g in range(ng):
            st_ref[g] = h0_ref[0, g * gw:(g + 1) * gw, :].T

    buf_ref[8:8 + rows, :] = xbc_ref[0]
    acc = cb_ref[...]
    for i in range(CONV_WIDTH):
        off = 8 - (CONV_WIDTH - 1) + i
        acc = acc + buf_ref[off:off + length, :] * cw_ref[i:i + 1, :]
    if rows == length:
        buf_ref[0:8, :] = buf_ref[length:length + 8, :]
    xc = _silu(acc)
    xs = xc[:, :width]
    bm = xc[:, width:width + ng * n]
    cm = xc[:, width + ng * n:]

    if rows < length:
        dte_full = jnp.concatenate([dte_ref[0], jnp.zeros((length - rows, width), F32)], axis=0)
        dtc_full = jnp.concatenate([dtc_ref[0], jnp.zeros((length - rows, 128), F32)], axis=0)
        live_e = lax.broadcasted_iota(I32, (length, width), 0) < rows
        live_c = lax.broadcasted_iota(I32, (length, 128), 0) < rows
        dt = jnp.where(live_e, _softplus(dte_full + dtbe_ref[...]), 0.0)
        dtc = jnp.where(live_c, _softplus(dtc_full + dtbc_ref[...]), 0.0)
    else:
        dt = _softplus(dte_ref[0] + dtbe_ref[...])
        dtc = _softplus(dtc_ref[0] + dtbc_ref[...])
    da = dt * (-jnp.exp(ale_ref[...]))
    dac = dtc * (-jnp.exp(alc_ref[...]))
    xdt = xs * dt

    ti = lax.broadcasted_iota(I32, (length, length), 0)
    si = lax.broadcasted_iota(I32, (length, length), 1)
    causal = ti >= si
    tri = jnp.where(causal, 1.0, 0.0)
    acs = _dot_exact(tri, da)
    acs_t = _dot_exact(tri, dac).T
    acs_last = acs[length - 1:length, :]
    e_acs = jnp.exp(acs)
    w_tail = jnp.exp(acs_last - acs)
    lane = lax.broadcasted_iota(I32, (length, 128), 1)

    ys = []
    for g in range(ng):
        bg = bm[:, g * n:(g + 1) * n]
        cg = cm[:, g * n:(g + 1) * n].astype(BF16)
        cb = _dot_nt(cg, bg.astype(BF16))
        st = st_ref[g]
        yg = _dot(cg, st.astype(BF16)) * e_acs[:, g * gw:(g + 1) * gw]
        pieces = []
        for pr in range(gw // 128):
            x_pair = xdt[:, g * gw + pr * 128: g * gw + (pr + 1) * 128].astype(BF16)
            halves = []
            for hh in range(2):
                head = (g * gw + pr * 128) // SSD_HEAD_DIM + hh
                col = acs[:, head * SSD_HEAD_DIM: head * SSD_HEAD_DIM + 1]
                rowv = acs_t[head:head + 1, :]
                decay = jnp.where(causal, jnp.exp(col - rowv), 0.0)
                halves.append(_dot((cb * decay).astype(BF16), x_pair))
            pieces.append(jnp.where(lane < SSD_HEAD_DIM, halves[0], halves[1]))
        ys.append(yg + jnp.concatenate(pieces, axis=1))
        xw = (xdt[:, g * gw:(g + 1) * gw] * w_tail[:, g * gw:(g + 1) * gw]).astype(BF16)
        st_ref[g] = st * jnp.exp(acs_last[:, g * gw:(g + 1) * gw]) + _dot(bg.T.astype(BF16), xw)
    y = jnp.concatenate(ys, axis=1)
    y = y + dsk_ref[...] * xs
    y = y[:rows] * _silu(z_ref[0])
    outs = []
    for g in range(ng):
        outs.append(_rms(y[:, g * gw:(g + 1) * gw], g_ref[:, g * gw:(g + 1) * gw]))
    y_ref[0] = jnp.concatenate(outs, axis=1)

    @pl.when(c == pl.num_programs(1) - 1)
    def _():
        for g in range(ng):
            hout_ref[0, g * gw:(g + 1) * gw, :] = st_ref[g].T


def _ssd(xbc, z, dte, dtc, conv_prev, h0, conv_w, conv_b, dt_bias, a_log, d_skip, g_ssd, *, length):
    bn, lt, cdim = xbc.shape
    width = z.shape[-1]
    n = h0.shape[-1]
    rows = min(lt, length)
    nc = lt // rows
    rep = width // dt_bias.shape[0]
    expand = lambda u: jnp.repeat(u, rep).reshape(1, width)
    lane_pad = lambda u: jnp.pad(u, (0, 128 - u.shape[0])).reshape(1, 128)
    seq = lambda w_: pl.BlockSpec((1, rows, w_), lambda b, c: (b, c, 0))
    const = lambda shape: pl.BlockSpec(shape, lambda b, c: (0,) * len(shape))
    body = functools.partial(_ssd_body, rows=rows, length=length)
    return pl.pallas_call(
        body,
        out_shape=[jax.ShapeDtypeStruct((bn, lt, width), F32), jax.ShapeDtypeStruct((bn, width, n), F32)],
        grid=(bn, nc),
        in_specs=[seq(cdim), seq(width), seq(width), seq(128),
                  pl.BlockSpec((1, 8, cdim), lambda b, c: (b, 0, 0)),
                  pl.BlockSpec((1, width, n), lambda b, c: (b, 0, 0)),
                  const((CONV_WIDTH, cdim)), const((1, cdim)), const((1, width)), const((1, 128)),
                  const((1, width)), const((1, 128)), const((1, width)), const((1, width))],
        out_specs=[seq(width), pl.BlockSpec((1, width, n), lambda b, c: (b, 0, 0))],
        scratch_shapes=[pltpu.VMEM((length + 8, cdim), F32),
                        pltpu.VMEM((SSD_GROUPS, n, width // SSD_GROUPS), F32)],
        compiler_params=_params(("arbitrary", "arbitrary")),
        name="ssd_scan",
    )(xbc, z, dte, dtc, conv_prev, h0, conv_w, conv_b.reshape(1, cdim), expand(dt_bias), lane_pad(dt_bias),
      expand(a_log), lane_pad(a_log), expand(d_skip), g_ssd.reshape(1, width))


def _outproj_body(att_ref, ssd_ref, x_ref, gt_ref, ga_ref, w_ref, *rest, transposed):
    o_ref = rest[-1]
    if transposed:
        a = att_ref[0].reshape(-1, att_ref.shape[-1]).T
    else:
        a = att_ref[...]
    mix = jnp.concatenate([_rms(a, ga_ref[...]), ssd_ref[...]], axis=-1).astype(BF16)
    o_ref[...] = x_ref[...] + gt_ref[0] * _dot(mix, w_ref[...])


def _outproj(att, ssd, x2d, gate, g_att, w_out, unified, *, first_tile, prompt_batch=None):
    t, d = x2d.shape
    tm = TOKEN_TILE
    tok = lambda width: pl.BlockSpec((tm, width), lambda i: (i, 0))
    sw = ssd.shape[-1]
    if prompt_batch is not None:
        b, s = prompt_batch
        tps = s // tm
        nh, dh = att.shape[1], att.shape[2]
        aw = nh * dh
        att_spec = pl.BlockSpec((1, nh, dh, tm), lambda i: (i // tps, 0, 0, i % tps))
        gate_spec = pl.BlockSpec((1, 1, d), lambda i: (i // tps, 0, 0))
    else:
        aw = att.shape[-1]
        att_spec = tok(aw)
        gate_spec = pl.BlockSpec((1, tm, d), lambda i: (i, 0, 0))
    return pl.pallas_call(
        functools.partial(_outproj_body, transposed=prompt_batch is not None),
        out_shape=jax.ShapeDtypeStruct(unified.shape, F32),
        grid=(t // tm,),
        in_specs=[att_spec, tok(sw), tok(d), gate_spec,
                  pl.BlockSpec((1, aw), lambda i: (0, 0)),
                  pl.BlockSpec((aw + sw, d), lambda i: (0, 0)),
                  pl.BlockSpec(memory_space=pl.ANY)],
        out_specs=pl.BlockSpec((tm, d), lambda i: (i + first_tile, 0)),
        input_output_aliases={6: 0},
        compiler_params=_params(("arbitrary",)),
        name="outproj",
    )(att, ssd, x2d, gate, g_att.reshape(1, aw), w_out, unified)


def _route_t(st, sbt):
    e, tm = st.shape
    per = e // N_EXPERT_GROUPS
    row = lax.broadcasted_iota(I32, (e, tm), 0)
    rg = lax.broadcasted_iota(I32, (per, tm), 0)
    gs = []
    for g in range(N_EXPERT_GROUPS):
        blk = sbt[g * per:(g + 1) * per]
        m1 = jnp.max(blk, axis=0, keepdims=True)
        first = jnp.min(jnp.where(blk == m1, rg, per), axis=0, keepdims=True)
        m2 = jnp.max(jnp.where(rg == first, -jnp.inf, blk), axis=0, keepdims=True)
        gs.append(m1 + m2)
    gscore = jnp.concatenate(gs, axis=0)
    gch = _select_topk_rows(gscore, N_EXPERT_GROUPS, TOPK_GROUPS)
    emask = jnp.concatenate([jnp.broadcast_to(gch[g:g + 1], (per, tm)) for g in range(N_EXPERT_GROUPS)], axis=0)
    cur = jnp.where(emask > 0, sbt, -jnp.inf)
    sel, wts = [], []
    for _ in range(TOP_K):
        m = jnp.max(cur, axis=0, keepdims=True)
        first = jnp.min(jnp.where(cur == m, row, e), axis=0, keepdims=True)
        pick = row == first
        sel.append(first)
        wts.append(jnp.sum(jnp.where(pick, st, 0.0), axis=0, keepdims=True))
        cur = jnp.where(pick, -jnp.inf, cur)
    sel = jnp.concatenate(sel, axis=0)
    w = jnp.concatenate(wts, axis=0)
    return sel, w / jnp.sum(w, axis=0, keepdims=True) * ROUTED_SCALE


def _rows_to_tiles(rows, ref, first=0):
    n = rows.shape[0]
    for s in range(8):
        ref[pl.ds(first * 8 + s, n, stride=8), :] = rows[:, s * 128:(s + 1) * 128]


def _tiles_to_rows(ref, n, first=0):
    return jnp.concatenate([ref[pl.ds(first * 8 + s, n, stride=8), :] for s in range(8)], axis=1)


def _moe_pre_body(x_ref, sh_ref, sc_ref, g_ref, wr_ref, rb_ref, wsg_ref, wsu_ref, wsd_ref,
                  t_ref, shared_ref, sel_ref, wt_ref):
    t = _rms(x_ref[...], g_ref[...]) * (1.0 + sc_ref[0]) + sh_ref[0]
    _rows_to_tiles(t, t_ref)
    tb = t.astype(BF16)
    st = jax.nn.sigmoid(_dot(tb, wr_ref[...])).T
    sel, w = _route_t(st, st + rb_ref[...])
    sel_ref[0] = sel
    wt_ref[0] = w
    hs = _silu(_dot(tb, wsg_ref[...])) * _dot(tb, wsu_ref[...])
    shared_ref[...] = _dot(hs.astype(BF16), wsd_ref[...])


def _moe_pre(x2d, shift, scale, mod_of_tile, g, w_router, router_bias, ws_gate, ws_up, ws_down):
    t, d = x2d.shape
    assert d == 8 * 128
    tm = TOKEN_TILE
    nt = t // tm
    e = w_router.shape[1]
    sd = ws_gate.shape[1]
    tok = lambda width: pl.BlockSpec((tm, width), lambda i: (i, 0))
    mod = pl.BlockSpec((1, tm, d), lambda i: (mod_of_tile(i), 0, 0))
    full = lambda shape: pl.BlockSpec(shape, lambda i: (0,) * len(shape))
    return pl.pallas_call(
        _moe_pre_body,
        out_shape=[jax.ShapeDtypeStruct((t * 8, 128), F32), jax.ShapeDtypeStruct((t, d), F32),
                   jax.ShapeDtypeStruct((nt, TOP_K, tm), I32), jax.ShapeDtypeStruct((nt, TOP_K, tm), F32)],
        grid=(nt,),
        in_specs=[tok(d), mod, mod, full((1, d)), full((d, e)), full((e, 1)),
                  full((d, sd)), full((d, sd)), full((sd, d))],
        out_specs=[pl.BlockSpec((tm * 8, 128), lambda i: (i, 0)), tok(d),
                   pl.BlockSpec((1, TOP_K, tm), lambda i: (i, 0, 0)),
                   pl.BlockSpec((1, TOP_K, tm), lambda i: (i, 0, 0))],
        compiler_params=_params(("arbitrary",)),
        name="moe_pre",
    )(x2d, shift, scale, g.reshape(1, d), w_router, router_bias.reshape(e, 1), ws_gate, ws_up, ws_down)


def _experts_body(tile_ref, exp_ref, lo_ref, hi_ref, n_ref, tok_ref, dst_ref, roww_ref, t_hbm, wg_ref, wu_ref, wd_ref,
                  out_hbm, xbuf, ybuf, wgb, wub, wdb, gsem, ssem):
    s = pl.program_id(0)
    n = n_ref[0]
    rows = xbuf.shape[0] // 16
    unroll = 8

    def tile_of(buf, slot, r):
        return buf.at[pl.ds(pl.multiple_of((slot * rows + r) * 8, 8), 8)]

    def all_rows_done(buf, sem, slot):
        whole = buf.at[pl.ds(pl.multiple_of(slot * rows * 8, 8), rows * 8)]
        pltpu.make_async_copy(whole, whole, sem.at[slot]).wait()

    opens_tile = jnp.logical_or(s == 0, tile_ref[s] != tile_ref[jnp.maximum(s - 1, 0)])

    @pl.when(jnp.logical_and(s < n, opens_tile))
    def _():
        slot = tile_ref[s] % 2

        def issue(g, _):
            for u in range(unroll):
                r = g * unroll + u
                pltpu.make_async_copy(t_hbm.at[tok_ref[0, 0, r]], tile_of(xbuf, slot, r),
                                      gsem.at[slot]).start(priority=u % 2)
            return 0
        lax.fori_loop(0, rows // unroll, issue, 0)

    j = s - 1

    @pl.when(jnp.logical_and(j >= 0, j < n))
    def _():
        jc = jnp.maximum(j, 0)
        tl = tile_ref[jc]
        slot = tl % 2
        first = jnp.logical_or(jc == 0, tile_ref[jnp.maximum(jc - 1, 0)] != tl)
        last = jnp.logical_or(jc == n - 1, tile_ref[jc + 1] != tl)

        @pl.when(first)
        def _():
            all_rows_done(xbuf, gsem, slot)

            @pl.when(tl >= 2)
            def _():
                all_rows_done(ybuf, ssem, slot)

        changed = jnp.logical_or(jc == 0, exp_ref[jc] != exp_ref[jnp.maximum(jc - 1, 0)])

        @pl.when(changed)
        def _():
            wgb[...] = wg_ref[0].astype(BF16)
            wub[...] = wu_ref[0].astype(BF16)
            wdb[...] = wd_ref[0].astype(BF16)

        base = slot * rows
        x = _tiles_to_rows(xbuf, rows, base).astype(BF16)
        hmid = _silu(_dot(x, wgb[...])) * _dot(x, wub[...])
        y = _dot(hmid.astype(BF16), wdb[...]) * roww_ref[0]

        @pl.when(first)
        def _():
            _rows_to_tiles(y, ybuf, base)

        @pl.when(jnp.logical_not(first))
        def _():
            r = lax.broadcasted_iota(I32, y.shape, 0)
            mine = jnp.logical_and(r >= lo_ref[jc], r < hi_ref[jc])
            _rows_to_tiles(jnp.where(mine, y, _tiles_to_rows(ybuf, rows, base)), ybuf, base)

        @pl.when(last)
        def _():
            def issue(g, _):
                for u in range(unroll):
                    r = g * unroll + u
                    pltpu.make_async_copy(tile_of(ybuf, slot, r), out_hbm.at[dst_ref[0, 0, r]],
                                          ssem.at[slot]).start(priority=(u + 1) % 2)
                return 0
            lax.fori_loop(0, rows // unroll, issue, 0)

        @pl.when(jc == n - 1)
        def _():
            all_rows_done(ybuf, ssem, slot)

            @pl.when(tl >= 1)
            def _():
                all_rows_done(ybuf, ssem, 1 - slot)


def _experts(plan, t3, w_gate, w_up, w_down):
    item_tile, item_e, item_lo, item_hi, n_items, row_tok, row_dst, row_w = plan
    n_tiles, _, tile = row_tok.shape
    e, d, ed = w_gate.shape
    n_steps = item_tile.shape[0]
    smem_blk = lambda fn: pl.BlockSpec((1, 1, tile), fn, memory_space=pltpu.SMEM)
    cur = lambda s, tl, ex, lo, hi, n: (tl[jnp.minimum(s, n_steps - 1)], 0, 0)
    prev = lambda s, tl, ex, lo, hi, n: (tl[jnp.maximum(s - 1, 0)], 0, 0)
    wmap = lambda s, tl, ex, lo, hi, n: (ex[jnp.maximum(s - 1, 0)], 0, 0)
    return pl.pallas_call(
        _experts_body,
        out_shape=jax.ShapeDtypeStruct((n_tiles * tile, 8, 128), F32),
        grid_spec=pltpu.PrefetchScalarGridSpec(
            num_scalar_prefetch=5,
            grid=(n_steps,),
            in_specs=[smem_blk(cur), smem_blk(prev),
                      pl.BlockSpec((1, tile, 1), prev),
                      pl.BlockSpec(memory_space=pl.ANY),
                      pl.BlockSpec((1, d, ed), wmap),
                      pl.BlockSpec((1, d, ed), wmap),
                      pl.BlockSpec((1, ed, d), wmap)],
            out_specs=pl.BlockSpec(memory_space=pl.ANY),
            scratch_shapes=[pltpu.VMEM((2 * tile * 8, 128), F32), pltpu.VMEM((2 * tile * 8, 128), F32),
                            pltpu.VMEM((d, ed), BF16), pltpu.VMEM((d, ed), BF16), pltpu.VMEM((ed, d), BF16),
                            pltpu.SemaphoreType.DMA((2,)), pltpu.SemaphoreType.DMA((2,))]),
        compiler_params=_params(("arbitrary",)),
        name="moe_experts",
    )(item_tile, item_e, item_lo, item_hi, n_items, row_tok, row_dst, row_w.reshape(n_tiles, tile, 1),
      t3, w_gate, w_up, w_down)


def _routing_plan(sel, wts, n_experts, tile):
    n_tok, k = sel.shape
    tk = n_tok * k
    assert tk % tile == 0
    n_tiles = tk // tile
    e_sorted, order, w_sorted = lax.sort((sel.reshape(-1), jnp.arange(tk, dtype=I32), wts.reshape(-1)), num_keys=1)
    bounds = jnp.searchsorted(e_sorted, jnp.arange(n_experts + 1, dtype=I32), side='left').astype(I32)
    start, end = bounds[:-1], bounds[1:]
    first_tile = start // tile
    n_items_e = jnp.where(end > start, (end - 1) // tile - first_tile + 1, 0)
    item_end = jnp.cumsum(n_items_e)
    n_items = item_end[-1]
    w_idx = jnp.arange(n_tiles + n_experts + 1, dtype=I32)
    live = w_idx < n_items
    item_e = jnp.minimum(jnp.searchsorted(item_end, w_idx, side='right', method='compare_all'),
                         n_experts - 1).astype(I32)
    item_tile = first_tile[item_e] + w_idx - (item_end - n_items_e)[item_e]
    item_lo = jnp.maximum(start[item_e], item_tile * tile) - item_tile * tile
    item_hi = jnp.minimum(end[item_e], (item_tile + 1) * tile) - item_tile * tile
    item_e = jnp.where(live, item_e, e_sorted[-1])
    item_tile = jnp.where(live, item_tile, n_tiles - 1)
    shape = (n_tiles, 1, tile)
    row_tok = (order // k).reshape(shape)
    row_dst = ((order % k) * n_tok + order // k).reshape(shape)
    return (item_tile.astype(I32), item_e.astype(I32), item_lo.astype(I32), item_hi.astype(I32),
            n_items.astype(I32).reshape(1), row_tok, row_dst, w_sorted.reshape(shape))


def _combine_body(x_ref, *refs, final_norm):
    r_refs = refs[:TOP_K]
    shared_ref, gt_ref, gf_ref, o_ref, acc_ref = refs[TOP_K:]
    acc = r_refs[0][...]
    for r_ref in r_refs[1:]:
        acc = acc + r_ref[...]
    acc_ref[...] = acc
    y = x_ref[...] + gt_ref[0] * (_tiles_to_rows(acc_ref, x_ref.shape[0]) + shared_ref[...])
    o_ref[...] = _rms(y, gf_ref[...]) if final_norm else y


def _combine(x2d, routed, shared, gate, mod_of_tile, g_final, *, first_tile, n_tiles, final_norm):
    t, d = x2d.shape
    tm = TOKEN_TILE
    tiles_all = t // tm
    tok = pl.BlockSpec((tm, d), lambda i: (i + first_tile, 0))
    slot = lambda k: pl.BlockSpec((tm * 8, 128), lambda i: (i + first_tile + k * tiles_all, 0))
    return pl.pallas_call(
        functools.partial(_combine_body, final_norm=final_norm),
        out_shape=jax.ShapeDtypeStruct((n_tiles * tm, d), F32),
        grid=(n_tiles,),
        in_specs=[tok] + [slot(k) for k in range(TOP_K)] + [
            tok,
            pl.BlockSpec((1, tm, d), lambda i: (mod_of_tile(i + first_tile), 0, 0)),
            pl.BlockSpec((1, d), lambda i: (0, 0))],
        out_specs=pl.BlockSpec((tm, d), lambda i: (i, 0)),
        scratch_shapes=[pltpu.VMEM((tm * 8, 128), F32)],
        compiler_params=_params(("arbitrary",)),
        name="moe_combine",
    )(x2d, *([routed] * TOP_K), shared, gate, g_final.reshape(1, d))


SELECT_PAGES = 16


def _sample_select_body(pt_ref, *refs, n_sel):
    pages = refs[:SELECT_PAGES]
    qt_ref, sel_ref, gate_ref = refs[SELECT_PAGES:]
    c = pl.program_id(1)
    ppb = MOBA_BLOCK // PAGE_SIZE
    nh = qt_ref.shape[1]
    for n in range(SELECT_PAGES // ppb):
        tot = pages[n * ppb][0]
        for i in range(1, ppb):
            tot = tot + pages[n * ppb + i][0]
        ksum = jnp.sum(tot, axis=2, keepdims=True)
        gate_ref[:, pl.ds(c * (SELECT_PAGES // ppb) + n, 1), :] = jnp.sum(qt_ref[0] * ksum, axis=1, keepdims=True)

    @pl.when(c == pl.num_programs(1) - 1)
    def _():
        nblk = gate_ref.shape[1]
        for h in range(nh):
            gate = gate_ref[h]
            row = lax.broadcasted_iota(I32, gate.shape, 0)
            g = gate
            picks = []
            for _ in range(n_sel):
                m = jnp.max(g, axis=0, keepdims=True)
                first = jnp.min(jnp.where(g == m, row, nblk), axis=0, keepdims=True)
                picks.append(first)
                g = jnp.where(row == first, -jnp.inf, g)
            picks.append(jnp.zeros((8 - n_sel, gate.shape[1]), I32))
            sel_ref[0, h] = jnp.concatenate(picks, axis=0)


def _sample_select(page_table, cache_kt, qt, n_sel):
    bd, n_pages = page_table.shape
    _, nh, dh, page = cache_kt.shape
    tpad = qt.shape[3]
    nblk = n_pages * PAGE_SIZE // MOBA_BLOCK
    page_spec = lambda i: pl.BlockSpec((1, nh, dh, page),
                                       lambda b, c, pt: (pt[b, c * SELECT_PAGES + i], 0, 0, 0))
    return pl.pallas_call(
        functools.partial(_sample_select_body, n_sel=n_sel),
        out_shape=jax.ShapeDtypeStruct((bd, nh, 8, tpad), I32),
        grid_spec=pltpu.PrefetchScalarGridSpec(
            num_scalar_prefetch=1,
            grid=(bd, n_pages // SELECT_PAGES),
            in_specs=[page_spec(i) for i in range(SELECT_PAGES)]
                     + [pl.BlockSpec((1, nh, dh, tpad), lambda b, c, pt: (b, 0, 0, 0))],
            out_specs=pl.BlockSpec((1, nh, 8, tpad), lambda b, c, pt: (b, 0, 0, 0)),
            scratch_shapes=[pltpu.VMEM((nh, nblk, tpad), F32)]),
        compiler_params=_params(("arbitrary", "arbitrary")),
        name="moba_sample_select",
    )(page_table, *([cache_kt] * SELECT_PAGES), qt)


def _sample_attend_body(pt_ref, selb_ref, slope_ref, kt_hbm, vt_hbm, q_ref, kn_ref, vn_ref, o_ref,
                        kbuf, vbuf, ksem, vsem, *, n_tok, n_sel, past):
    ppb = MOBA_BLOCK // PAGE_SIZE
    per_tok = n_sel * ppb
    nkv = n_tok * per_tok
    b = pl.program_id(0)
    h = pl.program_id(1)
    nh = q_ref.shape[1]
    step = b * nh + h
    n_steps = pl.num_programs(0) * nh

    def fetch(stp, slot):
        hh = stp % nh
        bb = stp // nh
        for i in range(nkv):
            logical_block = selb_ref[stp * (nkv // ppb) + i // ppb]
            pg = pt_ref[bb, logical_block * ppb + i % ppb]
            pltpu.make_async_copy(kt_hbm.at[pg, hh], kbuf.at[slot, i], ksem.at[slot]).start(priority=0)
            pltpu.make_async_copy(vt_hbm.at[pg, hh], vbuf.at[slot, i], vsem.at[slot]).start(priority=1)

    @pl.when(step == 0)
    def _():
        fetch(0, 0)

    @pl.when(step + 1 < n_steps)
    def _():
        fetch(step + 1, (step + 1) % 2)

    slot = step % 2
    pltpu.make_async_copy(kbuf.at[slot], kbuf.at[slot], ksem.at[slot]).wait()
    pltpu.make_async_copy(vbuf.at[slot], vbuf.at[slot], vsem.at[slot]).wait()

    slope = slope_ref[h]
    q = (q_ref[0, h] * (HEAD_DIM ** -0.5)).astype(BF16)
    tpad = q.shape[0]
    ncol = per_tok * PAGE_SIZE
    s_new = _dot_nt(q, kn_ref[0, h].astype(BF16))
    ti = lax.broadcasted_iota(I32, s_new.shape, 0)
    oi = lax.broadcasted_iota(I32, s_new.shape, 1)
    s_new = jnp.where(jnp.logical_and(oi <= ti, oi < n_tok), s_new - slope * (ti - oi).astype(F32), MASKED)
    vn = vn_ref[0, h].astype(BF16)
    col = lax.broadcasted_iota(I32, (tpad, ncol), 1)
    rowi = lax.broadcasted_iota(I32, (tpad, ncol), 0)
    outs = []
    for t in range(n_tok):
        kt = jnp.concatenate([kbuf[slot, t * per_tok + i] for i in range(per_tok)], axis=1).astype(BF16)
        vt = jnp.concatenate([vbuf[slot, t * per_tok + i] for i in range(per_tok)], axis=1).astype(BF16)
        s = _dot(q, kt)
        blk_of_col = jnp.zeros((tpad, ncol), I32)
        for r in range(n_sel):
            sb = selb_ref[(step * n_tok + t) * n_sel + r]
            blk_of_col = jnp.where(col // MOBA_BLOCK == r, sb, blk_of_col)
        spos = blk_of_col * MOBA_BLOCK + col % MOBA_BLOCK
        s = jnp.where(rowi == t, s - slope * (past + t - spos).astype(F32), MASKED)
        so = s_new[t:t + 1]
        m = jnp.maximum(jnp.max(jnp.max(s, axis=1, keepdims=True), axis=0, keepdims=True),
                        jnp.max(so, axis=1, keepdims=True))
        p = jnp.exp(s - m)
        po = jnp.exp(so - m)
        l = jnp.sum(jnp.sum(p, axis=1, keepdims=True), axis=0, keepdims=True) + jnp.sum(po, axis=1, keepdims=True)
        o_sel = _dot_nt(p.astype(BF16), vt)[t:t + 1]
        o_new = _dot(jnp.broadcast_to(po, (tpad, tpad)).astype(BF16), vn)[0:1]
        outs.append((o_sel + o_new) / l)
    outs.append(jnp.zeros((tpad - n_tok, HEAD_DIM), F32))
    o_ref[0, h] = jnp.concatenate(outs, axis=0)


def _sample_attend(page_table, selb, slopes, cache_kt, cache_vt, q, k_new, v_new, *, n_tok, n_sel, past):
    bd, nh, tpad, dh = q.shape
    page = cache_kt.shape[3]
    ppb = MOBA_BLOCK // PAGE_SIZE
    nkv = n_tok * n_sel * ppb
    pool = pl.BlockSpec(memory_space=pl.ANY)
    seq_spec = pl.BlockSpec((1, nh, tpad, dh), lambda b, h, ph, sb, sl: (b, 0, 0, 0))
    slices = pltpu.VMEM((2, nkv, dh, page), F32)
    return pl.pallas_call(
        functools.partial(_sample_attend_body, n_tok=n_tok, n_sel=n_sel, past=past),
        out_shape=jax.ShapeDtypeStruct((bd, nh, tpad, dh), F32),
        grid_spec=pltpu.PrefetchScalarGridSpec(
            num_scalar_prefetch=3,
            grid=(bd, nh),
            in_specs=[pool, pool, seq_spec, seq_spec, seq_spec],
            out_specs=seq_spec,
            scratch_shapes=[slices, slices, pltpu.SemaphoreType.DMA((2,)), pltpu.SemaphoreType.DMA((2,))]),
        compiler_params=_params(("arbitrary", "arbitrary")),
        name="moba_sample_attend",
    )(page_table, selb, slopes, cache_kt, cache_vt, q, k_new, v_new)


def _moba_sample(q, k_new, v_new, cache_kt, cache_vt, page_table, slopes):
    bd, nh, n_tok, dh = q.shape
    n_pages = page_table.shape[1]
    past = n_pages * PAGE_SIZE
    assert past % MOBA_BLOCK == 0 and n_pages % SELECT_PAGES == 0 and n_tok <= 8
    n_sel = min(MOBA_TOPK, past // MOBA_BLOCK)
    assert n_sel > 0
    tpad = 8
    padt = lambda u: jnp.pad(u, ((0, 0), (0, 0), (0, tpad - n_tok), (0, 0)))
    qp = padt(q)
    sel = _sample_select(page_table, cache_kt, jnp.swapaxes(qp, -1, -2), n_sel)[:, :, :n_sel, :n_tok]
    sel = jnp.transpose(sel, (0, 1, 3, 2))
    out = _sample_attend(page_table, sel.reshape(-1), slopes, cache_kt, cache_vt, qp, padt(k_new), padt(v_new),
                         n_tok=n_tok, n_sel=n_sel, past=past)
    return out[:, :, :n_tok]


def _alibi_slopes(n_heads):
    return jnp.exp2(-8.0 * (jnp.arange(n_heads, dtype=F32) + 1.0) / n_heads)


def kernel(x_prompt, x_sample, cache_k, cache_v, page_table, state_conv, state_ssm, c_prompt, c_sample,
           w_ada, b_ada, g_mix, w_in, conv_w, conv_b, dt_bias, a_log, d_skip, g_ssd, g_att, w_out,
           g_ffn, w_router, router_bias, w_gate, w_up, w_down, ws_gate, ws_up, ws_down, g_final):
    depth = w_ada.shape[0]
    b, s, d = x_prompt.shape
    bd, ts, _ = x_sample.shape
    tm = TOKEN_TILE
    tp, tsn = b * s, bd * ts
    assert s % tm == 0 and tsn % tm == 0 and tm == MOBA_BLOCK
    n_ssd_heads = dt_bias.shape[1]
    ssd_w = n_ssd_heads * SSD_HEAD_DIM
    att_w = d - ssd_w
    nh = att_w // HEAD_DIM
    conv_dim = conv_w.shape[2]
    n_state = state_ssm.shape[-1]
    n_experts = w_router.shape[2]
    qkvz = 3 * att_w + ssd_w + conv_dim
    slopes = _alibi_slopes(nh)
    prompt_tiles, sample_tiles = tp // tm, tsn // tm
    tps = s // tm
    mod_of_tile = lambda i: jnp.where(i < prompt_tiles, i // tps, b + i - prompt_tiles)

    xp = x_prompt.reshape(tp, d)
    xs = x_sample.reshape(tsn, d)
    kp_l, vp_l, ks_l, vs_l, cp_l, cs_l, hp_l, hs_l = [], [], [], [], [], [], [], []
    for l in range(depth):
        c_all = jnp.concatenate([c_prompt, c_sample], axis=0)
        n_c = c_all.shape[0]
        c_all = jnp.pad(c_all, ((0, -n_c % 8), (0, 0)))
        mods = jnp.split(_adaln(c_all, w_ada[l], b_ada[l])[:n_c], 6, axis=-1)
        mod_p = [u[:b, None, :] for u in mods]
        mod_s = [jnp.repeat(u[b:], ts, axis=0).reshape(sample_tiles, tm, d) for u in mods]
        mod_u = [jnp.concatenate([jnp.broadcast_to(p_, (b, tm, d)), s_], axis=0) for p_, s_ in zip(mod_p, mod_s)]

        dt_cols = w_in[l][:, qkvz:]
        w_all = jnp.concatenate([w_in[l][:, :qkvz], jnp.repeat(dt_cols, SSD_HEAD_DIM, axis=1),
                                 jnp.pad(dt_cols, ((0, 0), (0, 128 - n_ssd_heads)))], axis=1).astype(BF16)
        w_out_b = w_out[l].astype(BF16)
        ssd_par = (conv_w[l], conv_b[l], dt_bias[l], a_log[l], d_skip[l], g_ssd[l])
        dims = dict(att_w=att_w, ssd_w=ssd_w, conv_dim=conv_dim)

        qt, kaug, vt, kpg, vpg, km, z_p, xbc_p, dte_p, dtc_p = _inproj(
            xp, mod_p[0], mod_p[1], g_mix[l], w_all, prompt_batch=(b, s), **dims)
        kmean = km.reshape(b, s // MOBA_BLOCK, nh, HEAD_DIM).transpose(0, 2, 1, 3)
        att_p = _attn_prompt(slopes, qt, kaug, vt, kmean)
        xbc_p3 = xbc_p.reshape(b, s, conv_dim)
        y_p, h_p = _ssd(xbc_p3, z_p.reshape(b, s, ssd_w), dte_p.reshape(b, s, ssd_w), dtc_p.reshape(b, s, 128),
                        jnp.zeros((b, 8, conv_dim), F32), jnp.zeros((b, ssd_w, n_state), F32), *ssd_par,
                        length=MOBA_BLOCK)
        x1 = _outproj(att_p, y_p.reshape(tp, ssd_w), xp, mod_p[2], g_att[l], w_out_b,
                      jnp.zeros((tp + tsn, d), F32), first_tile=0, prompt_batch=(b, s))
        kp_l.append(jnp.swapaxes(kpg, -1, -2))
        vp_l.append(jnp.swapaxes(vpg, -1, -2))
        cp_l.append(xbc_p3[:, s - (CONV_WIDTH - 1):])
        hp_l.append(h_p.reshape(b, n_ssd_heads, SSD_HEAD_DIM, n_state))

        q_s, k_s, v_s, z_s, xbc_s, dte_s, dtc_s = _inproj(xs, mod_s[0], mod_s[1], g_mix[l], w_all, **dims)
        heads = lambda u: u.reshape(bd, ts, nh, HEAD_DIM).transpose(0, 2, 1, 3)
        qh, kh, vh = heads(q_s), heads(k_s), heads(v_s)
        att_s = _moba_sample(qh, kh, vh, jnp.swapaxes(cache_k[l], -1, -2), jnp.swapaxes(cache_v[l], -1, -2),
                             page_table, slopes)
        att_s = att_s.transpose(0, 2, 1, 3).reshape(tsn, att_w)
        xbc_s3 = xbc_s.reshape(bd, ts, conv_dim)
        conv_prev = jnp.pad(state_conv[l], ((0, 0), (8 - (CONV_WIDTH - 1), 0), (0, 0)))
        y_s, h_s = _ssd(xbc_s3, z_s.reshape(bd, ts, ssd_w), dte_s.reshape(bd, ts, ssd_w), dtc_s.reshape(bd, ts, 128),
                        conv_prev, state_ssm[l].reshape(bd, ssd_w, n_state), *ssd_par, length=128)
        x1 = _outproj(att_s, y_s.reshape(tsn, ssd_w), xs, mod_s[2], g_att[l], w_out_b, x1, first_tile=prompt_tiles)
        ks_l.append(kh)
        vs_l.append(vh)
        cs_l.append(jnp.concatenate([state_conv[l], xbc_s3], axis=1)[:, ts:])
        hs_l.append(h_s.reshape(bd, n_ssd_heads, SSD_HEAD_DIM, n_state))

        t_all, shared, sel_t, wts_t = _moe_pre(x1, mod_u[3], mod_u[4], mod_of_tile, g_ffn[l], w_router[l].astype(BF16),
                                               router_bias[l], ws_gate[l].astype(BF16), ws_up[l].astype(BF16),
                                               ws_down[l].astype(BF16))
        n_all = tp + tsn
        sel = sel_t.transpose(0, 2, 1).reshape(n_all, TOP_K)
        wts = wts_t.transpose(0, 2, 1).reshape(n_all, TOP_K)
        routed = _experts(_routing_plan(sel, wts, n_experts, EXPERT_TILE), t_all.reshape(n_all, 8, 128),
                          w_gate[l], w_up[l], w_down[l]).reshape(-1, 128)
        last = l == depth - 1
        fin = functools.partial(_combine, x1, routed, shared, mod_u[5], mod_of_tile, g_final, final_norm=last)
        xp = fin(first_tile=0, n_tiles=prompt_tiles)
        xs = fin(first_tile=prompt_tiles, n_tiles=sample_tiles)

    return (xp.reshape(b, s, d), xs.reshape(bd, ts, d), jnp.stack(kp_l), jnp.stack(vp_l), jnp.stack(ks_l),
            jnp.stack(vs_l), jnp.stack(cp_l), jnp.stack(cs_l), jnp.stack(hp_l), jnp.stack(hs_l))
```
